```python
import jax
import jax.numpy as jnp
from jax import lax
import numpy as np

D_MODEL = 1024
BATCH = 2
SEQ = 8192
DEPTH = 4

N_MIXERS = 3
FFN_HIDDEN = ((8 * D_MODEL + 3 * 256 - 1) // (3 * 256)) * 256
NORM_EPS = 1e-6
ADALN_INIT = 0.5
NEG_INF = -1e30
SOFTMAX_FLOOR = 1e-30

ROPE_THETA = 500000.0
ROT_FRACTION = 4

RET_HEAD_DIM = 256
RET_HEADS = D_MODEL // RET_HEAD_DIM
RET_V_DIM = 2 * RET_HEAD_DIM
RET_CHUNK = 128
RET_ROT_BASE = 10000.0

SWA_HEAD_DIM = 64
SWA_Q_HEADS = D_MODEL // SWA_HEAD_DIM
SWA_KV_HEADS = SWA_Q_HEADS // 8
SWA_WINDOW = 128

NSA_HEAD_DIM = 64
NSA_Q_HEADS = D_MODEL // NSA_HEAD_DIM
NSA_KV_HEADS = 4
NSA_CMP_LEN = 32
NSA_CMP_STRIDE = 16
NSA_SEL_LEN = 64
NSA_N_SEL = 16
NSA_WINDOW = 512
NSA_Q_BLOCK = 128
NSA_CMP_HIDDEN = 2 * NSA_HEAD_DIM
NSA_FORCE_SCORE = 1e4

kernel_name = "hybrid_retention_swa_nsa_trunk"


def rms_norm(x, g):
    x32 = x.astype(jnp.float32)
    y = x32 * lax.rsqrt(jnp.mean(x32 * x32, axis=-1, keepdims=True) + NORM_EPS)
    return (y * g.astype(jnp.float32)).astype(x.dtype)


def rms_normalize(x):
    x32 = x.astype(jnp.float32)
    return x32 * lax.rsqrt(jnp.mean(x32 * x32, axis=-1, keepdims=True) + NORM_EPS)


def partial_rope(x, pos):
    dh = x.shape[-1]
    rot = dh // ROT_FRACTION
    half = rot // 2
    inv = ROPE_THETA ** (-jnp.arange(half, dtype=jnp.float32) / half)
    ang = pos.astype(jnp.float32)[:, None] * inv[None, :]
    cos = jnp.cos(ang)[None, :, None, :]
    sin = jnp.sin(ang)[None, :, None, :]
    x1 = x[..., :half]
    x2 = x[..., half:rot]
    return jnp.concatenate([x1 * cos - x2 * sin, x2 * cos + x1 * sin, x[..., rot:]], axis=-1)


def retention_rotate(x, pos):
    dk = x.shape[-1]
    inv = 1.0 / (RET_ROT_BASE ** jnp.linspace(0.0, 1.0, dk // 2, dtype=jnp.float32))
    ang = pos.astype(jnp.float32)[:, None] * inv[None, :]
    cos = jnp.cos(ang)[None, :, None, :]
    sin = jnp.sin(ang)[None, :, None, :]
    xe = x[..., 0::2]
    xo = x[..., 1::2]
    return jnp.stack([xe * cos - xo * sin, xo * cos + xe * sin], axis=-1).reshape(x.shape)


def masked_softmax(s, mask):
    s = jnp.where(mask, s, NEG_INF)
    p = jnp.exp(s - s.max(axis=-1, keepdims=True)) * mask
    return p / jnp.maximum(p.sum(axis=-1, keepdims=True), SOFTMAX_FLOOR)


def retention_mixer(h, w_in, w_out):
    B, T, _ = h.shape
    dt = h.dtype
    H, DK, DV, C = RET_HEADS, RET_HEAD_DIM, RET_V_DIM, RET_CHUNK
    N = T // C
    proj = h @ w_in
    q, k, v, g = jnp.split(proj, [H * DK, 2 * H * DK, 2 * H * DK + H * DV], axis=-1)
    pos = jnp.arange(T)
    q = retention_rotate(q.reshape(B, T, H, DK).astype(jnp.float32), pos)
    k = retention_rotate(k.reshape(B, T, H, DK).astype(jnp.float32), pos) * DK ** -0.5
    v = v.reshape(B, T, H, DV).astype(jnp.float32)
    log_gamma = jnp.log(1.0 - 2.0 ** (-5.0 - jnp.arange(H, dtype=jnp.float32)))
    idx = jnp.arange(C, dtype=jnp.float32)
    rel = idx[:, None] - idx[None, :]
    intra = jnp.where(rel >= 0, jnp.exp(log_gamma[:, None, None] * jnp.maximum(rel, 0.0)), 0.0)
    q_decay = jnp.exp(log_gamma[:, None] * (idx + 1.0))[None, :, :, None]
    k_decay = jnp.exp(log_gamma[:, None] * (C - 1.0 - idx))[None, :, :, None]
    state_decay = jnp.exp(log_gamma * C)[None, :, None, None]

    def to_chunks(a):
        return a.reshape(B, N, C, H, a.shape[-1]).transpose(1, 0, 3, 2, 4)

    def step(state, qkv):
        qc, kc, vc = qkv
        scores = jnp.einsum('bhnd,bhmd->bhnm', qc, kc) * intra
        inner = jnp.einsum('bhnm,bhme->bhne', scores, vc)
        cross = jnp.einsum('bhnd,bhde->bhne', qc, state) * q_decay
        state = state * state_decay + jnp.einsum('bhmd,bhme->bhde', kc * k_decay, vc)
        return state, inner + cross

    state0 = jnp.zeros((B, H, DK, DV), jnp.float32)
    _, out = lax.scan(step, state0, (to_chunks(q), to_chunks(k), to_chunks(v)))
    out = rms_normalize(out.transpose(1, 0, 3, 2, 4).reshape(B, T, H, DV))
    y = out.reshape(B, T, H * DV).astype(dt) * jax.nn.silu(g)
    return y @ w_out


def swa_sink_mixer(h, w_in, q_norm_g, k_norm_g, sinks, w_out):
    B, T, _ = h.shape
    dt = h.dtype
    Hq, Hk, dh, C = SWA_Q_HEADS, SWA_KV_HEADS, SWA_HEAD_DIM, SWA_WINDOW
    G = Hq // Hk
    NB = T // C
    proj = (h @ w_in).astype(jnp.float32)
    q, k, v = jnp.split(proj, [Hq * dh, (Hq + Hk) * dh], axis=-1)
    pos = jnp.arange(T)
    q = partial_rope(rms_norm(q.reshape(B, T, Hq, dh), q_norm_g), pos)
    k = partial_rope(rms_norm(k.reshape(B, T, Hk, dh), k_norm_g), pos)
    v = v.reshape(B, T, Hk, dh)
    qb = q.reshape(B, NB, C, Hk, G, dh)

    def band(a):
        ap = jnp.pad(a, ((0, 0), (C, 0), (0, 0), (0, 0))).reshape(B, NB + 1, C, Hk, dh)
        return jnp.concatenate([ap[:, :-1], ap[:, 1:]], axis=2)

    kb, vb = band(k), band(v)
    s = jnp.einsum('bnqhgd,bnkhd->bnhgqk', qb, kb) * dh ** -0.5
    blk = jnp.arange(NB)
    key_pos = blk[:, None] * C - C + jnp.arange(2 * C)[None, :]
    q_pos = blk[:, None] * C + jnp.arange(C)[None, :]
    rel = q_pos[:, :, None] - key_pos[:, None, :]
    mask = (rel >= 0) & (rel < SWA_WINDOW) & (key_pos[:, None, :] >= 0)
    s = jnp.where(mask[None, :, None, None], s, NEG_INF)
    sink = sinks.astype(jnp.float32).reshape(Hk, G)[None, None, :, :, None, None]
    m = jnp.maximum(s.max(axis=-1, keepdims=True), sink)
    p = jnp.exp(s - m)
    denom = p.sum(axis=-1, keepdims=True) + jnp.exp(sink - m)
    o = jnp.einsum('bnhgqk,bnkhd->bnqhgd', p / denom, vb)
    return o.reshape(B, T, Hq * dh).astype(dt) @ w_out


def nsa_mixer(h, w_in, q_norm_g, k_norm_g, cmp_pe, cmp_w1, cmp_w2, w_out):
    B, T, _ = h.shape
    dt = h.dtype
    H, Hk, dh = NSA_Q_HEADS, NSA_KV_HEADS, NSA_HEAD_DIM
    G = H // Hk
    L, S, LS, W, Cq = NSA_CMP_LEN, NSA_CMP_STRIDE, NSA_SEL_LEN, NSA_WINDOW, NSA_Q_BLOCK
    NC = (T - L) // S + 1
    NS = T // LS
    NB = T // Cq
    n_sel = min(NSA_N_SEL, NS)
    scale = dh ** -0.5
    proj = (h @ w_in).astype(jnp.float32)
    splits = [H * dh + i * Hk * dh for i in range(7)]
    q, kc, vc, ks, vs, kw, vw, gates = jnp.split(proj, splits, axis=-1)
    pos = jnp.arange(T)
    q = partial_rope(rms_norm(q.reshape(B, T, H, dh), q_norm_g), pos).reshape(B, T, Hk, G, dh)
    ks = partial_rope(rms_norm(ks.reshape(B, T, Hk, dh), k_norm_g[1]), pos)
    kw = partial_rope(rms_norm(kw.reshape(B, T, Hk, dh), k_norm_g[2]), pos)
    vs = vs.reshape(B, T, Hk, dh)
    vw = vw.reshape(B, T, Hk, dh)
    gates = jax.nn.sigmoid(gates.reshape(B, T, Hk, G, 3))

    win = jnp.arange(NC)[:, None] * S + jnp.arange(L)[None, :]

    def compress(a, i):
        blocks = a.reshape(B, T, Hk, dh)[:, win] + cmp_pe[i][None, None, :, None, :]
        flat = blocks.transpose(0, 1, 3, 2, 4).reshape(B, NC, Hk, L * dh)
        return jax.nn.silu(flat @ cmp_w1[i]) @ cmp_w2[i]

    k_cmp = rms_norm(compress(kc, 0), k_norm_g[0])
    v_cmp = compress(vc, 1)
    cmp_start = jnp.arange(NC) * S
    cmp_end = cmp_start + L - 1
    sel_start = jnp.arange(NS) * LS
    overlap = ((cmp_start[:, None] < sel_start[None, :] + LS)
               & (cmp_start[:, None] + L > sel_start[None, :])).astype(jnp.float32)
    ks_blocks = ks.reshape(B, NS, LS, Hk, dh).transpose(0, 3, 1, 2, 4)
    vs_blocks = vs.reshape(B, NS, LS, Hk, dh).transpose(0, 3, 1, 2, 4)
    kw_pad = jnp.pad(kw, ((0, 0), (W, 0), (0, 0), (0, 0)))
    vw_pad = jnp.pad(vw, ((0, 0), (W, 0), (0, 0), (0, 0)))
    b_idx = jnp.arange(B)[:, None, None, None]
    h_idx = jnp.arange(Hk)[None, None, :, None]
    j_sel = jnp.arange(NS)

    def query_block(n):
        start = n * Cq
        t = start + jnp.arange(Cq)
        qb = lax.dynamic_slice_in_dim(q, start, Cq, axis=1)
        gb = lax.dynamic_slice_in_dim(gates, start, Cq, axis=1)
        s_c = jnp.einsum('bqhgd,bchd->bqhgc', qb, k_cmp) * scale
        p_c = masked_softmax(s_c, (cmp_end[None, :] <= t[:, None])[None, :, None, None, :])
        o_c = jnp.einsum('bqhgc,bchd->bqhgd', p_c, v_cmp)
        imp = jnp.einsum('bqhgc,cj->bqhj', p_c, overlap)
        cur = t // LS
        forced = (j_sel[None, :] == 0) | (j_sel[None, :] == cur[:, None]) | (j_sel[None, :] == cur[:, None] - 1)
        causal = sel_start[None, :] <= t[:, None]
        imp = jnp.where(forced[None, :, None, :], NSA_FORCE_SCORE, imp)
        imp = jnp.where(causal[None, :, None, :], imp, NEG_INF)
        _, sel = lax.top_k(imp, n_sel)
        k_sel = ks_blocks[b_idx, h_idx, sel].reshape(B, Cq, Hk, n_sel * LS, dh)
        v_sel = vs_blocks[b_idx, h_idx, sel].reshape(B, Cq, Hk, n_sel * LS, dh)
        tok = (sel[..., None] * LS + jnp.arange(LS)).reshape(B, Cq, Hk, n_sel * LS)
        s_s = jnp.einsum('bqhgd,bqhsd->bqhgs', qb, k_sel) * scale
        p_s = masked_softmax(s_s, (tok <= t[None, :, None, None])[:, :, :, None, :])
        o_s = jnp.einsum('bqhgs,bqhsd->bqhgd', p_s, v_sel)
        k_w = lax.dynamic_slice_in_dim(kw_pad, start, W + Cq, axis=1)
        v_w = lax.dynamic_slice_in_dim(vw_pad, start, W + Cq, axis=1)
        key_pos = start - W + jnp.arange(W + Cq)
        rel = t[:, None] - key_pos[None, :]
        valid_w = (rel >= 0) & (rel < W) & (key_pos[None, :] >= 0)
        s_w = jnp.einsum('bqhgd,bkhd->bqhgk', qb, k_w) * scale
        p_w = masked_softmax(s_w, valid_w[None, :, None, None, :])
        o_w = jnp.einsum('bqhgk,bkhd->bqhgd', p_w, v_w)
        return gb[..., 0:1] * o_c + gb[..., 1:2] * o_s + gb[..., 2:3] * o_w

    out = lax.map(query_block, jnp.arange(NB))
    out = out.transpose(1, 0, 2, 3, 4, 5).reshape(B, T, H * dh)
    return out.astype(dt) @ w_out


def swiglu(h, w_in, w_out):
    a, b = jnp.split(h @ w_in, 2, axis=-1)
    return (jax.nn.silu(a) * b) @ w_out


def hybrid_layer(x, c, mixer, mixer_params, mod_w, mod_b, norm1_g, norm2_g, ffn_w_in, ffn_w_out):
    mod = (jax.nn.silu(c) @ mod_w + mod_b)[:, None, :]
    shift1, scale1, gate1, shift2, scale2, gate2 = jnp.split(mod, 6, axis=-1)
    h = rms_norm(x, norm1_g) * (1 + scale1) + shift1
    x = x + gate1 * mixer(h, *mixer_params)
    h = rms_norm(x, norm2_g) * (1 + scale2) + shift2
    return x + gate2 * swiglu(h, ffn_w_in, ffn_w_out)


def setup_inputs(seed: int = 0) -> dict:
    key = jax.random.key(seed)
    keys = iter(jax.random.split(key, 128))

    def normal(shape, scale):
        return jax.random.normal(next(keys), shape, jnp.float32) * scale

    def dense(fan_in, fan_out):
        return normal((fan_in, fan_out), fan_in ** -0.5)

    def gain(shape):
        return 1.0 + normal(shape, 0.02)

    p = {"x": normal((BATCH, SEQ, D_MODEL), 1.0), "c": normal((BATCH, D_MODEL), 1.0)}
    for i in range(DEPTH):
        pre = f"l{i}_"
        p[pre + "mod_w"] = normal((D_MODEL, 6 * D_MODEL), ADALN_INIT * D_MODEL ** -0.5)
        p[pre + "mod_b"] = normal((6 * D_MODEL,), 0.02)
        p[pre + "norm1_g"] = gain((D_MODEL,))
        kind = i % N_MIXERS
        if kind == 0:
            p[pre + "ret_w_in"] = dense(D_MODEL, 2 * RET_HEADS * RET_HEAD_DIM + 2 * RET_HEADS * RET_V_DIM)
            p[pre + "ret_w_out"] = dense(RET_HEADS * RET_V_DIM, D_MODEL)
        elif kind == 1:
            p[pre + "swa_w_in"] = dense(D_MODEL, (SWA_Q_HEADS + 2 * SWA_KV_HEADS) * SWA_HEAD_DIM)
            p[pre + "swa_q_norm_g"] = gain((SWA_HEAD_DIM,))
            p[pre + "swa_k_norm_g"] = gain((SWA_HEAD_DIM,))
            p[pre + "swa_sinks"] = normal((SWA_Q_HEADS,), 0.5)
            p[pre + "swa_w_out"] = dense(SWA_Q_HEADS * SWA_HEAD_DIM, D_MODEL)
        else:
            p[pre + "nsa_w_in"] = dense(D_MODEL, NSA_Q_HEADS * NSA_HEAD_DIM + 6 * NSA_KV_HEADS * NSA_HEAD_DIM + 3 * NSA_Q_HEADS)
            p[pre + "nsa_q_norm_g"] = gain((NSA_HEAD_DIM,))
            p[pre + "nsa_k_norm_g"] = gain((3, NSA_HEAD_DIM))
            p[pre + "nsa_cmp_pe"] = normal((2, NSA_CMP_LEN, NSA_HEAD_DIM), 0.1)
            p[pre + "nsa_cmp_w1"] = normal((2, NSA_CMP_LEN * NSA_HEAD_DIM, NSA_CMP_HIDDEN), (NSA_CMP_LEN * NSA_HEAD_DIM) ** -0.5)
            p[pre + "nsa_cmp_w2"] = normal((2, NSA_CMP_HIDDEN, NSA_HEAD_DIM), NSA_CMP_HIDDEN ** -0.5)
            p[pre + "nsa_w_out"] = dense(NSA_Q_HEADS * NSA_HEAD_DIM, D_MODEL)
        p[pre + "norm2_g"] = gain((D_MODEL,))
        p[pre + "ffn_w_in"] = dense(D_MODEL, 2 * FFN_HIDDEN)
        p[pre + "ffn_w_out"] = dense(FFN_HIDDEN, D_MODEL)
    return p


def reference(x, c,
              l0_mod_w, l0_mod_b, l0_norm1_g, l0_ret_w_in, l0_ret_w_out, l0_norm2_g, l0_ffn_w_in, l0_ffn_w_out,
              l1_mod_w, l1_mod_b, l1_norm1_g, l1_swa_w_in, l1_swa_q_norm_g, l1_swa_k_norm_g, l1_swa_sinks, l1_swa_w_out, l1_norm2_g, l1_ffn_w_in, l1_ffn_w_out,
              l2_mod_w, l2_mod_b, l2_norm1_g, l2_nsa_w_in, l2_nsa_q_norm_g, l2_nsa_k_norm_g, l2_nsa_cmp_pe, l2_nsa_cmp_w1, l2_nsa_cmp_w2, l2_nsa_w_out, l2_norm2_g, l2_ffn_w_in, l2_ffn_w_out,
              l3_mod_w, l3_mod_b, l3_norm1_g, l3_ret_w_in, l3_ret_w_out, l3_norm2_g, l3_ffn_w_in, l3_ffn_w_out):
    layers = [
        (l0_mod_w, l0_mod_b, l0_norm1_g, (l0_ret_w_in, l0_ret_w_out), l0_norm2_g, l0_ffn_w_in, l0_ffn_w_out),
        (l1_mod_w, l1_mod_b, l1_norm1_g, (l1_swa_w_in, l1_swa_q_norm_g, l1_swa_k_norm_g, l1_swa_sinks, l1_swa_w_out),
         l1_norm2_g, l1_ffn_w_in, l1_ffn_w_out),
        (l2_mod_w, l2_mod_b, l2_norm1_g, (l2_nsa_w_in, l2_nsa_q_norm_g, l2_nsa_k_norm_g, l2_nsa_cmp_pe, l2_nsa_cmp_w1,
                                          l2_nsa_cmp_w2, l2_nsa_w_out),
         l2_norm2_g, l2_ffn_w_in, l2_ffn_w_out),
        (l3_mod_w, l3_mod_b, l3_norm1_g, (l3_ret_w_in, l3_ret_w_out), l3_norm2_g, l3_ffn_w_in, l3_ffn_w_out),
    ]
    mixers = (retention_mixer, swa_sink_mixer, nsa_mixer)
    for i in range(DEPTH):
        mod_w, mod_b, norm1_g, mixer_params, norm2_g, ffn_w_in, ffn_w_out = layers[i]
        x = hybrid_layer(x, c, mixers[i % N_MIXERS], mixer_params, mod_w, mod_b, norm1_g, norm2_g, ffn_w_in, ffn_w_out)
    return x
```

```python
import functools

import jax
import jax.numpy as jnp
import numpy as np
from jax import lax
from jax.experimental import pallas as pl
from jax.experimental.pallas import tpu as pltpu

F32 = jnp.float32
BF16 = jnp.bfloat16

NORM_EPS = 1e-6
NEG_INF = -1e30
SOFTMAX_FLOOR = 1e-30
REMOVED = -3e38
MASK_BIAS = -(2.0 ** 30)

V7X_VMEM_LIMIT_BYTES = 56 * 2 ** 20
LANES = 128
QBLK = 128

ROPE_THETA = 500000.0
RET_ROT_BASE = 10000.0
RET_HEADS = 4
RET_CHUNK = 256
SWA_KV_HEADS = 2
SWA_WINDOW = 128
NSA_KV_HEADS = 4
NSA_CMP_LEN = 32
NSA_CMP_STRIDE = 16
NSA_SEL_LEN = 64
NSA_N_SEL = 16
NSA_WINDOW = 512
NSA_FORCE_SCORE = 1e4
HEAD_DIM = 64
SEL_CHUNK = 512

NT_DIMS = (((1,), (1,)), ((), ()))


def _params(*sem):
    return pltpu.CompilerParams(dimension_semantics=sem, vmem_limit_bytes=V7X_VMEM_LIMIT_BYTES)


def _dot(a, b):
    return jnp.dot(a, b, preferred_element_type=F32)


def _dot_nt(a, b):
    return lax.dot_general(a, b, NT_DIMS, preferred_element_type=F32)


def _mod_kernel(c_ref, w_ref, b_ref, o_ref):
    c = c_ref[...]
    s = c * jax.nn.sigmoid(c)
    nb = s.shape[0]
    s8 = jnp.concatenate([s, jnp.zeros((8 - nb, s.shape[1]), F32)], axis=0)
    r = jnp.dot(s8, w_ref[...], preferred_element_type=F32, precision=lax.Precision.HIGHEST)
    o_ref[...] = r[:nb] + b_ref[...]


def adaln_mod(c, mod_w, mod_b):
    B, D = c.shape
    N = mod_w.shape[1]
    tn = N // 4
    out = pl.pallas_call(
        _mod_kernel,
        grid=(N // tn,),
        in_specs=[pl.BlockSpec((B, D), lambda j: (0, 0)),
                  pl.BlockSpec((D, tn), lambda j: (0, j)),
                  pl.BlockSpec((1, tn), lambda j: (0, j))],
        out_specs=pl.BlockSpec((B, tn), lambda j: (0, j)),
        out_shape=jax.ShapeDtypeStruct((B, N), F32),
        compiler_params=_params("arbitrary"),
        name="adaln_mod",
    )(c, mod_w, mod_b.reshape(1, N))
    return out.reshape(B, 6, 1, D)


def _normmod(x, g, scale, shift):
    ms = jnp.mean(x * x, axis=-1, keepdims=True)
    y = x * lax.rsqrt(ms + NORM_EPS) * g
    return y * (1.0 + scale) + shift


def _normmod_matmul_kernel(x_ref, g_ref, scale_ref, shift_ref, w_ref, o_ref, h_ref):
    @pl.when(pl.program_id(1) == 0)
    def _():
        h_ref[...] = _normmod(x_ref[...], g_ref[...], scale_ref[...], shift_ref[...]).astype(BF16)

    o_ref[...] = _dot(h_ref[...], w_ref[...]).astype(o_ref.dtype)


def normmod_matmul(x, g, mod, shift_idx, w, T, tm=1024, tn=512):
    BT, D = x.shape
    N = w.shape[1]
    tm = min(tm, T)
    tn = tn if N % tn == 0 else LANES
    tpb = T // tm
    return pl.pallas_call(
        _normmod_matmul_kernel,
        grid=(BT // tm, N // tn),
        in_specs=[pl.BlockSpec((tm, D), lambda i, j: (i, 0)),
                  pl.BlockSpec((1, D), lambda i, j: (0, 0)),
                  pl.BlockSpec((None, None, 1, D), lambda i, j: (i // tpb, shift_idx + 1, 0, 0)),
                  pl.BlockSpec((None, None, 1, D), lambda i, j: (i // tpb, shift_idx, 0, 0)),
                  pl.BlockSpec((D, tn), lambda i, j: (0, j))],
        out_specs=pl.BlockSpec((tm, tn), lambda i, j: (i, j)),
        out_shape=jax.ShapeDtypeStruct((BT, N), BF16),
        scratch_shapes=[pltpu.VMEM((tm, D), BF16)],
        compiler_params=_params("parallel", "arbitrary"),
        name="normmod_matmul",
    )(x, g.reshape(1, D), mod, mod, w)


def _outproj_kernel(y_ref, w_ref, x_ref, gate_ref, o_ref):
    o_ref[...] = x_ref[...] + gate_ref[...] * _dot(y_ref[...], w_ref[...])


def outproj_residual(y, w, x, mod, gate_idx, T, tm=512):
    BT, K = y.shape
    D = w.shape[1]
    tm = min(tm, T)
    tpb = T // tm
    return pl.pallas_call(
        _outproj_kernel,
        grid=(BT // tm,),
        in_specs=[pl.BlockSpec((tm, K), lambda i: (i, 0)),
                  pl.BlockSpec((K, D), lambda i: (0, 0)),
                  pl.BlockSpec((tm, D), lambda i: (i, 0)),
                  pl.BlockSpec((None, None, 1, D), lambda i: (i // tpb, gate_idx, 0, 0))],
        out_specs=pl.BlockSpec((tm, D), lambda i: (i, 0)),
        out_shape=jax.ShapeDtypeStruct((BT, D), F32),
        compiler_params=_params("parallel"),
        name="outproj_residual",
    )(y, w, x, mod)


def _ffn_kernel(x_ref, g_ref, scale_ref, shift_ref, gate_ref, wa_ref, wb_ref, wo_ref, o_ref, h_ref):
    k = pl.program_id(1)

    @pl.when(k == 0)
    def _():
        h_ref[...] = _normmod(x_ref[...], g_ref[...], scale_ref[...], shift_ref[...]).astype(BF16)
        o_ref[...] = jnp.zeros_like(o_ref)

    h = h_ref[...]
    a = _dot(h, wa_ref[...])
    b = _dot(h, wb_ref[...])
    u = (a * jax.nn.sigmoid(a) * b).astype(BF16)
    o_ref[...] += _dot(u, wo_ref[...])

    @pl.when(k == pl.num_programs(1) - 1)
    def _():
        o_ref[...] = x_ref[...] + gate_ref[...] * o_ref[...]


def ffn_block(x, g, mod, w_in, w_out, T, tm=1024, tc=256):
    BT, D = x.shape
    F = w_out.shape[0]
    tm = min(tm, T)
    tpb = T // tm
    nk = F // tc
    modspec = lambda idx: pl.BlockSpec((None, None, 1, D), lambda i, k: (i // tpb, idx, 0, 0))
    return pl.pallas_call(
        _ffn_kernel,
        grid=(BT // tm, nk),
        in_specs=[pl.BlockSpec((tm, D), lambda i, k: (i, 0)),
                  pl.BlockSpec((1, D), lambda i, k: (0, 0)),
                  modspec(4), modspec(3), modspec(5),
                  pl.BlockSpec((D, tc), lambda i, k: (0, k)),
                  pl.BlockSpec((D, tc), lambda i, k: (0, k + nk)),
                  pl.BlockSpec((tc, D), lambda i, k: (k, 0))],
        out_specs=pl.BlockSpec((tm, D), lambda i, k: (i, 0)),
        out_shape=jax.ShapeDtypeStruct((BT, D), F32),
        scratch_shapes=[pltpu.VMEM((tm, D), BF16)],
        compiler_params=_params("parallel", "arbitrary"),
        name="ffn_block",
    )(x, g.reshape(1, D), mod, mod, mod, w_in, w_in, w_out)


def _ret_core_kernel(q_ref, k_ref, v_ref, g_ref, cos_ref, sin_ref, o_ref, state_ref, *, chunks):
    C = RET_CHUNK
    DK = q_ref.shape[-1]
    half = DK // 2

    @pl.when(pl.program_id(2) == 0)
    def _():
        state_ref[...] = jnp.zeros_like(state_ref)

    head = pl.program_id(1)
    log_gammas = [float(np.log(1.0 - 2.0 ** (-5.0 - i))) for i in range(RET_HEADS)]
    lg = jnp.float32(log_gammas[-1])
    for i in range(RET_HEADS - 1):
        lg = jnp.where(head == i, log_gammas[i], lg)
    rel = (lax.broadcasted_iota(jnp.int32, (C, C), 0) - lax.broadcasted_iota(jnp.int32, (C, C), 1)).astype(F32)
    intra = jnp.where(rel >= 0, jnp.exp(lg * jnp.maximum(rel, 0.0)), 0.0)
    idx = lax.broadcasted_iota(jnp.int32, (C, 1), 0).astype(F32)
    q_decay = jnp.exp(lg * (idx + 1.0))
    k_decay = jnp.exp(lg * (C - 1.0 - idx))
    state_decay = jnp.exp(lg * float(C))

    def rotate(x, cos, sin):
        x1, x2 = x[:, :half], x[:, half:]
        return jnp.concatenate([x1 * cos - x2 * sin, x2 * cos + x1 * sin], axis=1)

    for c in range(chunks):
        sl = slice(c * C, (c + 1) * C)
        cos, sin = cos_ref[sl, :], sin_ref[sl, :]
        q = rotate(q_ref[sl, :].astype(F32), cos, sin)
        k = rotate(k_ref[sl, :].astype(F32), cos, sin) * (DK ** -0.5)
        v = v_ref[sl, :]
        qb = q.astype(BF16)
        scores = _dot_nt(qb, k.astype(BF16)) * intra
        inner = _dot(scores.astype(BF16), v)
        state = state_ref[...]
        cross = _dot(qb, state.astype(BF16)) * q_decay
        kd_t = (k * k_decay).T.astype(BF16)
        state_ref[...] = state * state_decay + _dot(kd_t, v)
        out = inner + cross
        ms = jnp.mean(out * out, axis=-1, keepdims=True)
        y = out * lax.rsqrt(ms + NORM_EPS)
        g = g_ref[sl, :].astype(F32)
        o_ref[sl, :] = (y * (g * jax.nn.sigmoid(g))).astype(BF16)


def retention_core(proj, cos, sin, B, T, tb=1024):
    H = RET_HEADS
    DK = proj.shape[-1] // (6 * H)
    DV = 2 * DK
    tb = min(tb, T)
    kern = functools.partial(_ret_core_kernel, chunks=tb // RET_CHUNK)
    return pl.pallas_call(
        kern,
        grid=(B, H, T // tb),
        in_specs=[pl.BlockSpec((None, tb, DK), lambda b, h, t: (b, t, h)),
                  pl.BlockSpec((None, tb, DK), lambda b, h, t: (b, t, H + h)),
                  pl.BlockSpec((None, tb, DV), lambda b, h, t: (b, t, H + h)),
                  pl.BlockSpec((None, tb, DV), lambda b, h, t: (b, t, 2 * H + h)),
                  pl.BlockSpec((tb, DK // 2), lambda b, h, t: (t, 0)),
                  pl.BlockSpec((tb, DK // 2), lambda b, h, t: (t, 0))],
        out_specs=pl.BlockSpec((None, tb, DV), lambda b, h, t: (b, t, h)),
        out_shape=jax.ShapeDtypeStruct((B, T, H * DV), BF16),
        scratch_shapes=[pltpu.VMEM((DK, DV), F32)],
        compiler_params=_params("parallel", "parallel", "arbitrary"),
        name="retention_core",
    )(proj, proj, proj, proj, cos, sin)


def _head_prep_kernel(x_ref, gain_ref, c_ref, s1_ref, s2_ref, bd_ref, o_ref, *, norm, rope, dup, scale):
    x = x_ref[...].astype(F32)
    if norm:
        sq = x * x
        hi = sq.astype(BF16)
        lo = (sq - hi.astype(F32)).astype(BF16)
        bd = bd_ref[...]
        ss = _dot(hi, bd) + _dot(lo, bd)
        x = x * lax.rsqrt(ss * (1.0 / HEAD_DIM) + NORM_EPS) * gain_ref[...]
    if rope:
        x = x * c_ref[...] + pltpu.roll(x, LANES - 8, 1) * s1_ref[...] + pltpu.roll(x, 8, 1) * s2_ref[...]
    if scale != 1.0:
        x = x * scale
    if dup:
        lane_half = lax.broadcasted_iota(jnp.int32, x.shape, 1) // HEAD_DIM
        x = jnp.where(lane_half == pl.program_id(2), x, pltpu.roll(x, HEAD_DIM, 1))
    o_ref[...] = x.astype(BF16)


def head_prep(proj, col0, ntiles, gain, tables, B, T, *, norm, rope, dup, scale=1.0, tm=1024):
    BT = proj.shape[0]
    tm = min(tm, T)
    tpb = T // tm
    c, s1, s2 = tables
    bd = jnp.asarray(np.kron(np.eye(2), np.ones((HEAD_DIM, HEAD_DIM))), BF16)
    gain2 = jnp.concatenate([gain, gain]).reshape(1, LANES).astype(F32)
    kern = functools.partial(_head_prep_kernel, norm=norm, rope=rope, dup=dup, scale=scale)
    if dup:
        grid = (BT // tm, ntiles, 2)
        imap = lambda f: (lambda i, j, h: f(i, j))
        out_spec = pl.BlockSpec((None, None, tm, LANES), lambda i, j, h: (i // tpb, 2 * j + h, i % tpb, 0))
        out_shape = jax.ShapeDtypeStruct((B, 2 * ntiles, T, LANES), BF16)
        sem = ("parallel", "parallel", "arbitrary")
    else:
        grid = (BT // tm, ntiles)
        imap = lambda f: f
        out_spec = pl.BlockSpec((tm, LANES), lambda i, j: (i, j))
        out_shape = jax.ShapeDtypeStruct((BT, ntiles * LANES), BF16)
        sem = ("parallel", "parallel")
    tab = pl.BlockSpec((tm, LANES), imap(lambda i, j: (i % tpb, 0)))
    return pl.pallas_call(
        kern,
        grid=grid,
        in_specs=[pl.BlockSpec((tm, LANES), imap(lambda i, j: (i, col0 + j))),
                  pl.BlockSpec((1, LANES), imap(lambda i, j: (0, 0))),
                  tab, tab, tab,
                  pl.BlockSpec((LANES, LANES), imap(lambda i, j: (0, 0)))],
        out_specs=out_spec,
        out_shape=out_shape,
        compiler_params=_params(*sem),
        name="head_prep",
    )(proj, gain2, c, s1, s2, bd)


def rope_tables(T):
    half = HEAD_DIM // 4 // 2
    inv = ROPE_THETA ** (-jnp.arange(half, dtype=F32) / half)
    ang = jnp.arange(T).astype(F32)[:, None] * inv[None, :]
    cos, sin = jnp.cos(ang), jnp.sin(ang)
    ones = jnp.ones((T, HEAD_DIM - 2 * half), F32)
    zeros8 = jnp.zeros((T, half), F32)
    zeros = jnp.zeros((T, HEAD_DIM - 2 * half), F32)
    c = jnp.concatenate([cos, cos, ones], axis=1)
    s1 = jnp.concatenate([-sin, zeros8, zeros], axis=1)
    s2 = jnp.concatenate([zeros8, sin, zeros], axis=1)
    return tuple(jnp.concatenate([t, t], axis=1) for t in (c, s1, s2))


def retention_tables(T, dk):
    inv = 1.0 / (RET_ROT_BASE ** jnp.linspace(0.0, 1.0, dk // 2, dtype=F32))
    ang = jnp.arange(T).astype(F32)[:, None] * inv[None, :]
    return jnp.cos(ang), jnp.sin(ang)


def _stack_heads(q, extra=None):
    first_half = lax.broadcasted_iota(jnp.int32, (q.shape[0], LANES), 1) < HEAD_DIM
    zero = jnp.zeros((q.shape[0], LANES), q.dtype)
    rows = []
    for p in range(q.shape[1] // LANES):
        qp = q[:, p * LANES:(p + 1) * LANES]
        for part in (jnp.where(first_half, qp, zero), jnp.where(first_half, zero, qp)):
            rows.append(part if extra is None else jnp.concatenate([part, extra], axis=1))
    return jnp.concatenate(rows, axis=0)


def _unstack_heads(o, cq):
    first_half = lax.broadcasted_iota(jnp.int32, (cq, LANES), 1) < HEAD_DIM
    G = o.shape[0] // cq
    pairs = [jnp.where(first_half, o[(2 * p) * cq:(2 * p + 1) * cq], o[(2 * p + 1) * cq:(2 * p + 2) * cq])
             for p in range(G // 2)]
    return jnp.concatenate(pairs, axis=1)


def _banded_kernel(*refs, G, nprev, window, use_sink, qb):
    if use_sink:
        sink_ref, q_ref, k_ref, v_ref, o_ref = refs
    else:
        q_ref, k_ref, v_ref, o_ref = refs
    Cq = QBLK
    nk = (nprev + 1) * Cq
    hk = pl.program_id(1)

    def block(i, carry):
        n = pl.program_id(2) * qb + i
        r0 = pl.multiple_of(i * Cq, Cq)
        start = pl.multiple_of(jnp.maximum(n - nprev, 0) * Cq, Cq)
        kwin = k_ref[pl.ds(start, nk), :]
        vwin = v_ref[pl.ds(start, nk), :]
        lhs = _stack_heads(q_ref[pl.ds(r0, Cq), :])
        s_all = _dot_nt(lhs, kwin)
        t = n * Cq + lax.broadcasted_iota(jnp.int32, (Cq, nk), 0)
        rel = t - (start + lax.broadcasted_iota(jnp.int32, (Cq, nk), 1))
        valid = jnp.logical_and(rel >= 0, rel < window)
        outs = []
        for g in range(G):
            s = jnp.where(valid, s_all[g * Cq:(g + 1) * Cq], NEG_INF)
            m = jnp.max(s, axis=-1, keepdims=True)
            if use_sink:
                sink = sink_ref[hk * G + g]
                m = jnp.maximum(m, sink)
            p = jnp.exp(s - m)
            denom = jnp.sum(p, axis=-1, keepdims=True)
            if use_sink:
                denom = denom + jnp.exp(sink - m)
            else:
                denom = jnp.maximum(denom, SOFTMAX_FLOOR)
            outs.append(_dot(p.astype(BF16), vwin) / denom)
        o_ref[pl.ds(r0, Cq), :] = _unstack_heads(jnp.concatenate(outs, axis=0), Cq).astype(BF16)
        return carry

    lax.fori_loop(0, qb, block, 0)


def banded_attention(q, k2, v2, sinks, B, T, *, G, nprev, window, qb=4):
    Hk = k2.shape[1]
    qb = min(qb, T // QBLK)
    use_sink = sinks is not None
    kern = functools.partial(_banded_kernel, G=G, nprev=nprev, window=window, use_sink=use_sink, qb=qb)
    in_specs = [pl.BlockSpec((None, qb * QBLK, G * HEAD_DIM), lambda b, h, n: (b, n, h)),
                pl.BlockSpec((None, None, T, LANES), lambda b, h, n: (b, h, 0, 0)),
                pl.BlockSpec((None, None, T, LANES), lambda b, h, n: (b, h, 0, 0))]
    args = [q, k2, v2]
    if use_sink:
        in_specs = [pl.BlockSpec(memory_space=pltpu.SMEM)] + in_specs
        args = [sinks.astype(F32)] + args
    return pl.pallas_call(
        kern,
        grid=(B, Hk, T // (qb * QBLK)),
        in_specs=in_specs,
        out_specs=pl.BlockSpec((None, qb * QBLK, G * HEAD_DIM), lambda b, h, n: (b, n, h)),
        out_shape=jax.ShapeDtypeStruct((B, T, Hk * G * HEAD_DIM), BF16),
        compiler_params=_params("parallel", "parallel", "arbitrary"),
        name="banded_attention",
    )(*args)


def _nsa_compress_kernel(x_ref, pe_ref, w1_ref, w2_ref, gain_ref, bd_ref, o_ref, *, norm):
    x = x_ref[...]
    gw = x.shape[1]
    u0 = _dot(x, w1_ref[:gw, :])
    u1 = _dot(x, w1_ref[gw:, :])
    n = u1.shape[0]
    pe8 = jnp.broadcast_to(pe_ref[...], (8, pe_ref.shape[1]))
    const = _dot(pe8, w1_ref[...])[:1]
    pre = u0 + pltpu.roll(u1, n - 1, 0) + const
    hid = (pre * jax.nn.sigmoid(pre)).astype(BF16)
    y = _dot(hid, w2_ref[...])
    if norm:
        sq = y * y
        hi = sq.astype(BF16)
        lo = (sq - hi.astype(F32)).astype(BF16)
        bd = bd_ref[...]
        ss = _dot(hi, bd) + _dot(lo, bd)
        y = y * lax.rsqrt(ss * (1.0 / HEAD_DIM) + NORM_EPS) * gain_ref[...]
    o_ref[...] = y.astype(BF16)


def nsa_compress(xg, pe, w1, w2, gain, *, norm):
    B, Hk, NG, GW = xg.shape
    bd = jnp.asarray(np.kron(np.eye(2), np.ones((HEAD_DIM, HEAD_DIM))), BF16)
    gain2 = jnp.concatenate([gain, gain]).reshape(1, LANES).astype(F32)
    w2d = jnp.concatenate([w2, w2], axis=1).astype(BF16)
    hid = w1.shape[1]
    return pl.pallas_call(
        functools.partial(_nsa_compress_kernel, norm=norm),
        grid=(B, Hk),
        in_specs=[pl.BlockSpec((None, None, NG, GW), lambda b, h: (b, h, 0, 0)),
                  pl.BlockSpec((1, 2 * GW), lambda b, h: (0, 0)),
                  pl.BlockSpec((2 * GW, hid), lambda b, h: (0, 0)),
                  pl.BlockSpec((hid, LANES), lambda b, h: (0, 0)),
                  pl.BlockSpec((1, LANES), lambda b, h: (0, 0)),
                  pl.BlockSpec((LANES, LANES), lambda b, h: (0, 0))],
        out_specs=pl.BlockSpec((None, None, NG, LANES), lambda b, h: (b, h, 0, 0)),
        out_shape=jax.ShapeDtypeStruct((B, Hk, NG, LANES), BF16),
        compiler_params=_params("parallel", "parallel"),
        name="nsa_compress",
    )(xg, pe.reshape(1, 2 * GW).astype(BF16), w1.astype(BF16), w2d, gain2, bd)


def _nsa_cmp_kernel(q_ref, kc_ref, vc_ref, ov_ref, oc_ref, mneg_ref, *, n_sel):
    Cq = QBLK
    G = q_ref.shape[1] // HEAD_DIM
    n = pl.program_id(2)
    ncp = kc_ref.shape[0]
    s_all = _dot_nt(_stack_heads(q_ref[...]), kc_ref[...])
    t = n * Cq + lax.broadcasted_iota(jnp.int32, (Cq, ncp), 0)
    cmp_end = lax.broadcasted_iota(jnp.int32, (Cq, ncp), 1) * NSA_CMP_STRIDE + (NSA_CMP_LEN - 1)
    valid = cmp_end <= t
    outs = []
    psum = None
    for g in range(G):
        s = jnp.where(valid, s_all[g * Cq:(g + 1) * Cq], NEG_INF)
        p = jnp.where(valid, jnp.exp(s - jnp.max(s, axis=-1, keepdims=True)), 0.0)
        p = p / jnp.maximum(jnp.sum(p, axis=-1, keepdims=True), SOFTMAX_FLOOR)
        outs.append(_dot(p.astype(BF16), vc_ref[...]))
        psum = p if psum is None else psum + p
    oc_ref[...] = _unstack_heads(jnp.concatenate(outs, axis=0), Cq).astype(BF16)

    hi = psum.astype(BF16)
    lo = (psum - hi.astype(F32)).astype(BF16)
    ov = ov_ref[...]
    imp = (_dot(hi, ov) + _dot(lo, ov)).T
    nsb = imp.shape[0]
    j = lax.broadcasted_iota(jnp.int32, (nsb, Cq), 0)
    tt = n * Cq + lax.broadcasted_iota(jnp.int32, (nsb, Cq), 1)
    cur = tt // NSA_SEL_LEN
    forced = jnp.logical_or(j == 0, jnp.logical_or(j == cur, j == cur - 1))
    v = jnp.where(forced, NSA_FORCE_SCORE, imp)
    v = jnp.where(j * NSA_SEL_LEN <= tt, v, NEG_INF)
    sel = jnp.zeros((nsb, Cq), F32)
    jf = j.astype(F32)
    for _ in range(n_sel):
        mx = jnp.max(v, axis=0, keepdims=True)
        first = jf == jnp.min(jnp.where(v == mx, jf, float(nsb)), axis=0, keepdims=True)
        sel = jnp.where(first, 1.0, sel)
        v = jnp.where(first, REMOVED, v)
    mneg_ref[...] = jnp.where(sel > 0.0, 0.0, MASK_BIAS).T.astype(BF16)


def nsa_cmp_topk(q, kcmp, vcmp, ov, B, T, n_sel):
    Hk = kcmp.shape[1]
    GW = q.shape[-1] // Hk
    ncp = kcmp.shape[2]
    cmp_spec = pl.BlockSpec((None, None, ncp, LANES), lambda b, h, n: (b, h, 0, 0))
    return pl.pallas_call(
        functools.partial(_nsa_cmp_kernel, n_sel=n_sel),
        grid=(B, Hk, T // QBLK),
        in_specs=[pl.BlockSpec((None, QBLK, GW), lambda b, h, n: (b, n, h)),
                  cmp_spec, cmp_spec,
                  pl.BlockSpec((ncp, LANES), lambda b, h, n: (0, 0))],
        out_specs=[pl.BlockSpec((None, QBLK, GW), lambda b, h, n: (b, n, h)),
                   pl.BlockSpec((None, None, QBLK, LANES), lambda b, h, n: (b, h, n, 0))],
        out_shape=[jax.ShapeDtypeStruct((B, T, q.shape[-1]), BF16),
                   jax.ShapeDtypeStruct((B, Hk, T, LANES), BF16)],
        compiler_params=_params("parallel", "parallel", "arbitrary"),
        name="nsa_cmp_topk",
    )(q, kcmp, vcmp, ov)


def _nsa_sel_kernel(q_ref, mneg_ref, k_ref, e_ref, v_ref, o_ref, lhs_ref, m_ref, l_ref, acc_ref):
    Cq = QBLK
    KC = SEL_CHUNK
    n = pl.program_id(2)
    lhs_ref[...] = _stack_heads(q_ref[...], extra=mneg_ref[...])
    rows = lhs_ref.shape[0]
    m_ref[...] = jnp.full(m_ref.shape, NEG_INF, F32)
    l_ref[...] = jnp.zeros_like(l_ref)
    acc_ref[...] = jnp.zeros_like(acc_ref)

    def chunk(c, diagonal):
        off = pl.multiple_of(c * KC, KC)
        rhs = jnp.concatenate([k_ref[pl.ds(off, KC), :], e_ref[pl.ds(off, KC), :]], axis=1)
        s = _dot_nt(lhs_ref[...], rhs)
        if diagonal:
            t = n * Cq + (lax.broadcasted_iota(jnp.int32, (rows, KC), 0) & (Cq - 1))
            s = jnp.where(off + lax.broadcasted_iota(jnp.int32, (rows, KC), 1) <= t, s, NEG_INF)
        m_prev = m_ref[...]
        m_new = jnp.maximum(m_prev, jnp.max(s, axis=-1, keepdims=True))
        alpha = jnp.exp(m_prev - m_new)
        p = jnp.exp(s - jnp.concatenate([m_new] * (KC // LANES), axis=1))
        l_ref[...] = alpha * l_ref[...] + jnp.sum(p, axis=-1, keepdims=True)
        acc_ref[...] = alpha * acc_ref[...] + _dot(p.astype(BF16), v_ref[pl.ds(off, KC), :])
        m_ref[...] = m_new

    nfull = (n * Cq) // KC

    def body(c, carry):
        chunk(c, False)
        return carry

    lax.fori_loop(0, nfull, body, 0)
    chunk(nfull, True)
    o = acc_ref[...] / jnp.maximum(l_ref[...], SOFTMAX_FLOOR)
    o_ref[...] = _unstack_heads(o, Cq).astype(BF16)


def nsa_selected(q, mneg, k2, e, v2, B, T):
    Hk = k2.shape[1]
    GW = q.shape[-1] // Hk
    G = GW // HEAD_DIM
    kv_spec = pl.BlockSpec((None, None, T, LANES), lambda b, h, n: (b, h, 0, 0))
    return pl.pallas_call(
        _nsa_sel_kernel,
        grid=(B, Hk, T // QBLK),
        in_specs=[pl.BlockSpec((None, QBLK, GW), lambda b, h, n: (b, n, h)),
                  pl.BlockSpec((None, None, QBLK, LANES), lambda b, h, n: (b, h, n, 0)),
                  kv_spec,
                  pl.BlockSpec((T, LANES), lambda b, h, n: (0, 0)),
                  kv_spec],
        out_specs=pl.BlockSpec((None, QBLK, GW), lambda b, h, n: (b, n, h)),
        out_shape=jax.ShapeDtypeStruct((B, T, q.shape[-1]), BF16),
        scratch_shapes=[pltpu.VMEM((G * QBLK, 2 * LANES), BF16),
                        pltpu.VMEM((G * QBLK, LANES), F32),
                        pltpu.VMEM((G * QBLK, LANES), F32),
                        pltpu.VMEM((G * QBLK, LANES), F32)],
        compiler_params=_params("parallel", "parallel", "arbitrary"),
        name="nsa_selected",
    )(q, mneg, k2, e, v2)


def _nsa_out_kernel(oc_ref, os_ref, ow_ref, gates_ref, e_ref, w_ref, x_ref, gate_ref, o_ref):
    D = oc_ref.shape[1]
    sg = jax.nn.sigmoid(gates_ref[...].astype(F32)).astype(BF16)
    gx = _dot(sg, e_ref[...])
    y = (gx[:, :D] * oc_ref[...].astype(F32) + gx[:, D:2 * D] * os_ref[...].astype(F32)
         + gx[:, 2 * D:] * ow_ref[...].astype(F32))
    o_ref[...] = x_ref[...] + gate_ref[...] * _dot(y.astype(BF16), w_ref[...])


def nsa_out(oc, os_, ow, proj, gate_col_tile, expand, w, x, mod, gate_idx, T, tm=512):
    BT, D = x.shape
    tm = min(tm, T)
    tpb = T // tm
    row = pl.BlockSpec((tm, D), lambda i: (i, 0))
    return pl.pallas_call(
        _nsa_out_kernel,
        grid=(BT // tm,),
        in_specs=[row, row, row,
                  pl.BlockSpec((tm, LANES), lambda i: (i, gate_col_tile)),
                  pl.BlockSpec((LANES, 3 * D), lambda i: (0, 0)),
                  pl.BlockSpec((D, D), lambda i: (0, 0)),
                  row,
                  pl.BlockSpec((None, None, 1, D), lambda i: (i // tpb, gate_idx, 0, 0))],
        out_specs=row,
        out_shape=jax.ShapeDtypeStruct((BT, D), F32),
        compiler_params=_params("parallel"),
        name="nsa_out",
    )(oc, os_, ow, proj, expand, w, x, mod)


def retention_mixer(x, mod, norm_g, w_in, w_out, B, T):
    H = RET_HEADS
    D = x.shape[1]
    dk = D // H
    head_perm = np.concatenate([np.arange(0, dk, 2), np.arange(1, dk, 2)])
    perm = np.concatenate([hd * dk + head_perm for hd in range(2 * H)] + [np.arange(2 * H * dk, w_in.shape[1])])
    proj = normmod_matmul(x, norm_g, mod, 0, w_in[:, perm].astype(BF16), T)
    cos, sin = retention_tables(T, dk)
    y = retention_core(proj.reshape(B, T, -1), cos, sin, B, T)
    return outproj_residual(y.reshape(B * T, -1), w_out.astype(BF16), x, mod, 2, T)


def swa_mixer(x, mod, norm_g, w_in, q_norm_g, k_norm_g, sinks, w_out, B, T):
    D = x.shape[1]
    Hk = SWA_KV_HEADS
    G = D // HEAD_DIM // Hk
    proj = normmod_matmul(x, norm_g, mod, 0, w_in.astype(BF16), T, tn=256)
    tables = rope_tables(T)
    nq = D // LANES
    nkv = Hk * HEAD_DIM // LANES
    q = head_prep(proj, 0, nq, q_norm_g, tables, B, T, norm=True, rope=True, dup=False, scale=HEAD_DIM ** -0.5)
    k2 = head_prep(proj, nq, nkv, k_norm_g, tables, B, T, norm=True, rope=True, dup=True)
    v2 = head_prep(proj, nq + nkv, nkv, k_norm_g, tables, B, T, norm=False, rope=False, dup=True)
    o = banded_attention(q.reshape(B, T, D), k2, v2, sinks, B, T, G=G, nprev=SWA_WINDOW // QBLK, window=SWA_WINDOW)
    return outproj_residual(o.reshape(B * T, D), w_out.astype(BF16), x, mod, 2, T)


def nsa_mixer(x, mod, norm_g, w_in, q_norm_g, k_norm_g, cmp_pe, cmp_w1, cmp_w2, w_out, B, T):
    D = x.shape[1]
    Hk = NSA_KV_HEADS
    H = D // HEAD_DIM
    G = H // Hk
    kvw = Hk * HEAD_DIM
    n_main = D + 6 * kvw
    n_pad = -(-w_in.shape[1] // LANES) * LANES
    w_pad = jnp.pad(w_in, ((0, 0), (0, n_pad - w_in.shape[1]))).astype(BF16)
    proj = normmod_matmul(x, norm_g, mod, 0, w_pad, T, tn=LANES * 3)
    tables = rope_tables(T)
    nq = D // LANES
    nkv = kvw // LANES
    q = head_prep(proj, 0, nq, q_norm_g, tables, B, T, norm=True, rope=True, dup=False, scale=HEAD_DIM ** -0.5)
    q = q.reshape(B, T, D)
    col = lambda i: nq + i * nkv
    ks2 = head_prep(proj, col(2), nkv, k_norm_g[1], tables, B, T, norm=True, rope=True, dup=True)
    vs2 = head_prep(proj, col(3), nkv, k_norm_g[1], tables, B, T, norm=False, rope=False, dup=True)
    kw2 = head_prep(proj, col(4), nkv, k_norm_g[2], tables, B, T, norm=True, rope=True, dup=True)
    vw2 = head_prep(proj, col(5), nkv, k_norm_g[2], tables, B, T, norm=False, rope=False, dup=True)

    S = NSA_CMP_STRIDE
    p3 = proj.reshape(B, T, n_pad)

    def groups(c0):
        a = p3[:, :, c0:c0 + kvw].reshape(B, T // S, S, Hk, HEAD_DIM)
        return a.transpose(0, 3, 1, 2, 4).reshape(B, Hk, T // S, S * HEAD_DIM)

    kcmp = nsa_compress(groups(D), cmp_pe[0], cmp_w1[0], cmp_w2[0], k_norm_g[0], norm=True)
    vcmp = nsa_compress(groups(D + kvw), cmp_pe[1], cmp_w1[1], cmp_w2[1], k_norm_g[0], norm=False)

    ncp = T // S
    ns = T // NSA_SEL_LEN
    cs = np.arange(ncp)[:, None] * S
    js = np.arange(LANES)[None, :]
    overlap = ((cs < js * NSA_SEL_LEN + NSA_SEL_LEN) & (cs + NSA_CMP_LEN > js * NSA_SEL_LEN)
               & (js < ns) & (np.arange(ncp)[:, None] < ncp - 1))
    ov = jnp.asarray(overlap, BF16)
    oc, mneg = nsa_cmp_topk(q, kcmp, vcmp, ov, B, T, min(NSA_N_SEL, ns))

    blk_onehot = jnp.asarray(np.arange(T)[:, None] // NSA_SEL_LEN == js, BF16)
    os_ = nsa_selected(q, mneg, ks2, blk_onehot, vs2, B, T)
    ow = banded_attention(q, kw2, vw2, None, B, T, G=G, nprev=NSA_WINDOW // QBLK, window=NSA_WINDOW)

    rows = np.arange(LANES)[:, None]
    cols = np.arange(3 * D)[None, :]
    expand = jnp.asarray(rows == 3 * ((cols % D) // HEAD_DIM) + cols // D, BF16)
    flat = lambda a: a.reshape(B * T, D)
    return nsa_out(flat(oc), flat(os_), flat(ow), proj, n_main // LANES, expand, w_out.astype(BF16), x, mod, 2, T)


def kernel(x, c, l0_mod_w, l0_mod_b, l0_norm1_g, l0_ret_w_in, l0_ret_w_out, l0_norm2_g, l0_ffn_w_in, l0_ffn_w_out, l1_mod_w, l1_mod_b, l1_norm1_g, l1_swa_w_in, l1_swa_q_norm_g, l1_swa_k_norm_g, l1_swa_sinks, l1_swa_w_out, l1_norm2_g, l1_ffn_w_in, l1_ffn_w_out, l2_mod_w, l2_mod_b, l2_norm1_g, l2_nsa_w_in, l2_nsa_q_norm_g, l2_nsa_k_norm_g, l2_nsa_cmp_pe, l2_nsa_cmp_w1, l2_nsa_cmp_w2, l2_nsa_w_out, l2_norm2_g, l2_ffn_w_in, l2_ffn_w_out, l3_mod_w, l3_mod_b, l3_norm1_g, l3_ret_w_in, l3_ret_w_out, l3_norm2_g, l3_ffn_w_in, l3_ffn_w_out):
    B, T, D = x.shape
    layers = [
        (l0_mod_w, l0_mod_b, l0_norm1_g, retention_mixer, (l0_ret_w_in, l0_ret_w_out), l0_norm2_g, l0_ffn_w_in, l0_ffn_w_out),
        (l1_mod_w, l1_mod_b, l1_norm1_g, swa_mixer,
         (l1_swa_w_in, l1_swa_q_norm_g, l1_swa_k_norm_g, l1_swa_sinks, l1_swa_w_out), l1_norm2_g, l1_ffn_w_in, l1_ffn_w_out),
        (l2_mod_w, l2_mod_b, l2_norm1_g, nsa_mixer,
         (l2_nsa_w_in, l2_nsa_q_norm_g, l2_nsa_k_norm_g, l2_nsa_cmp_pe, l2_nsa_cmp_w1, l2_nsa_cmp_w2, l2_nsa_w_out),
         l2_norm2_g, l2_ffn_w_in, l2_ffn_w_out),
        (l3_mod_w, l3_mod_b, l3_norm1_g, retention_mixer, (l3_ret_w_in, l3_ret_w_out), l3_norm2_g, l3_ffn_w_in, l3_ffn_w_out),
    ]
    h = x.reshape(B * T, D)
    for mod_w, mod_b, norm1_g, mixer, mixer_params, norm2_g, ffn_w_in, ffn_w_out in layers:
        mod = adaln_mod(c, mod_w, mod_b)
        h = mixer(h, mod, norm1_g, *mixer_params, B, T)
        h = ffn_block(h, norm2_g, mod, ffn_w_in.astype(BF16), ffn_w_out.astype(BF16), T)
    return h.reshape(B, T, D)
```

```python
import functools

import jax
import jax.numpy as jnp
import numpy as np
from jax import lax
from jax.experimental import pallas as pl
from jax.experimental.pallas import tpu as pltpu

F32 = jnp.float32
BF16 = jnp.bfloat16

NORM_EPS = 1e-6
NEG_INF = -1e30
SOFTMAX_FLOOR = 1e-30
REMOVED = -3e38
MASK_BIAS = -(2.0 ** 30)

V7X_VMEM_LIMIT_BYTES = 56 * 2 ** 20
LANES = 128
QBLK = 128

ROPE_THETA = 500000.0
RET_ROT_BASE = 10000.0
RET_HEADS = 4
RET_CHUNK = 256
SWA_KV_HEADS = 2
SWA_WINDOW = 128
NSA_KV_HEADS = 4
NSA_CMP_LEN = 32
NSA_CMP_STRIDE = 16
NSA_SEL_LEN = 64
NSA_N_SEL = 16
NSA_WINDOW = 512
NSA_FORCE_SCORE = 1e4
HEAD_DIM = 64
SEL_CHUNK = 1024
SEL_TAIL = 512
LOG2_E = 1.4426950408889634
Q_SCALE = HEAD_DIM ** -0.5 * LOG2_E

NT_DIMS = (((1,), (1,)), ((), ()))


def _params(*sem):
    return pltpu.CompilerParams(dimension_semantics=sem, vmem_limit_bytes=V7X_VMEM_LIMIT_BYTES)


def _dot(a, b):
    return jnp.dot(a, b, preferred_element_type=F32)


def _dot_nt(a, b):
    return lax.dot_general(a, b, NT_DIMS, preferred_element_type=F32)


def _mod_kernel(c_ref, w_ref, b_ref, o_ref):
    c = c_ref[...]
    s = c * jax.nn.sigmoid(c)
    nb = s.shape[0]
    s8 = jnp.concatenate([s, jnp.zeros((8 - nb, s.shape[1]), F32)], axis=0)
    r = jnp.dot(s8, w_ref[...], preferred_element_type=F32, precision=lax.Precision.HIGHEST)
    o_ref[...] = r[:nb] + b_ref[...]


def adaln_mod(c, mod_w, mod_b):
    B, D = c.shape
    N = mod_w.shape[1]
    tn = N // 4
    out = pl.pallas_call(
        _mod_kernel,
        grid=(N // tn,),
        in_specs=[pl.BlockSpec((B, D), lambda j: (0, 0)),
                  pl.BlockSpec((D, tn), lambda j: (0, j)),
                  pl.BlockSpec((1, tn), lambda j: (0, j))],
        out_specs=pl.BlockSpec((B, tn), lambda j: (0, j)),
        out_shape=jax.ShapeDtypeStruct((B, N), F32),
        compiler_params=_params("arbitrary"),
        name="adaln_mod",
    )(c, mod_w, mod_b.reshape(1, N))
    return out.reshape(B, 6, 1, D)


def _normmod(x, g, scale, shift):
    ms = jnp.mean(x * x, axis=-1, keepdims=True)
    y = x * lax.rsqrt(ms + NORM_EPS) * g
    return y * (1.0 + scale) + shift


def _normmod_matmul_kernel(x_ref, g_ref, scale_ref, shift_ref, w_ref, o_ref, h_ref):
    @pl.when(pl.program_id(1) == 0)
    def _():
        h_ref[...] = _normmod(x_ref[...], g_ref[...], scale_ref[...], shift_ref[...]).astype(BF16)

    o_ref[...] = _dot(h_ref[...], w_ref[...]).astype(o_ref.dtype)


def normmod_matmul(x, g, mod, shift_idx, w, T, tm=1024, tn=512):
    BT, D = x.shape
    N = w.shape[1]
    tm = min(tm, T)
    tn = tn if N % tn == 0 else LANES
    tpb = T // tm
    return pl.pallas_call(
        _normmod_matmul_kernel,
        grid=(BT // tm, N // tn),
        in_specs=[pl.BlockSpec((tm, D), lambda i, j: (i, 0)),
                  pl.BlockSpec((1, D), lambda i, j: (0, 0)),
                  pl.BlockSpec((None, None, 1, D), lambda i, j: (i // tpb, shift_idx + 1, 0, 0)),
                  pl.BlockSpec((None, None, 1, D), lambda i, j: (i // tpb, shift_idx, 0, 0)),
                  pl.BlockSpec((D, tn), lambda i, j: (0, j))],
        out_specs=pl.BlockSpec((tm, tn), lambda i, j: (i, j)),
        out_shape=jax.ShapeDtypeStruct((BT, N), BF16),
        scratch_shapes=[pltpu.VMEM((tm, D), BF16)],
        compiler_params=_params("parallel", "arbitrary"),
        name="normmod_matmul",
    )(x, g.reshape(1, D), mod, mod, w)


def _outproj_kernel(y_ref, w_ref, x_ref, gate_ref, o_ref):
    o_ref[...] = x_ref[...] + gate_ref[...] * _dot(y_ref[...], w_ref[...])


def outproj_residual(y, w, x, mod, gate_idx, T, tm=512):
    BT, K = y.shape
    D = w.shape[1]
    tm = min(tm, T)
    tpb = T // tm
    return pl.pallas_call(
        _outproj_kernel,
        grid=(BT // tm,),
        in_specs=[pl.BlockSpec((tm, K), lambda i: (i, 0)),
                  pl.BlockSpec((K, D), lambda i: (0, 0)),
                  pl.BlockSpec((tm, D), lambda i: (i, 0)),
                  pl.BlockSpec((None, None, 1, D), lambda i: (i // tpb, gate_idx, 0, 0))],
        out_specs=pl.BlockSpec((tm, D), lambda i: (i, 0)),
        out_shape=jax.ShapeDtypeStruct((BT, D), F32),
        compiler_params=_params("parallel"),
        name="outproj_residual",
    )(y, w, x, mod)


def _ffn_kernel(x_ref, g_ref, scale_ref, shift_ref, gate_ref, wa_ref, wb_ref, wo_ref, o_ref, h_ref):
    k = pl.program_id(1)

    @pl.when(k == 0)
    def _():
        h_ref[...] = _normmod(x_ref[...], g_ref[...], scale_ref[...], shift_ref[...]).astype(BF16)
        o_ref[...] = jnp.zeros_like(o_ref)

    h = h_ref[...]
    a = _dot(h, wa_ref[...])
    b = _dot(h, wb_ref[...])
    u = (a * jax.nn.sigmoid(a) * b).astype(BF16)
    o_ref[...] += _dot(u, wo_ref[...])

    @pl.when(k == pl.num_programs(1) - 1)
    def _():
        o_ref[...] = x_ref[...] + gate_ref[...] * o_ref[...]


def ffn_block(x, g, mod, w_in, w_out, T, tm=1024, tc=256):
    BT, D = x.shape
    F = w_out.shape[0]
    tm = min(tm, T)
    tpb = T // tm
    nk = F // tc
    modspec = lambda idx: pl.BlockSpec((None, None, 1, D), lambda i, k: (i // tpb, idx, 0, 0))
    return pl.pallas_call(
        _ffn_kernel,
        grid=(BT // tm, nk),
        in_specs=[pl.BlockSpec((tm, D), lambda i, k: (i, 0)),
                  pl.BlockSpec((1, D), lambda i, k: (0, 0)),
                  modspec(4), modspec(3), modspec(5),
                  pl.BlockSpec((D, tc), lambda i, k: (0, k)),
                  pl.BlockSpec((D, tc), lambda i, k: (0, k + nk)),
                  pl.BlockSpec((tc, D), lambda i, k: (k, 0))],
        out_specs=pl.BlockSpec((tm, D), lambda i, k: (i, 0)),
        out_shape=jax.ShapeDtypeStruct((BT, D), F32),
        scratch_shapes=[pltpu.VMEM((tm, D), BF16)],
        compiler_params=_params("parallel", "arbitrary"),
        name="ffn_block",
    )(x, g.reshape(1, D), mod, mod, mod, w_in, w_in, w_out)


def _ret_core_kernel(q_ref, k_ref, v_ref, g_ref, cos_ref, sin_ref, o_ref, state_ref, *, chunks):
    C = RET_CHUNK
    DK = q_ref.shape[-1]
    half = DK // 2

    @pl.when(pl.program_id(2) == 0)
    def _():
        state_ref[...] = jnp.zeros_like(state_ref)

    head = pl.program_id(1)
    log_gammas = [float(np.log(1.0 - 2.0 ** (-5.0 - i))) for i in range(RET_HEADS)]
    lg = jnp.float32(log_gammas[-1])
    for i in range(RET_HEADS - 1):
        lg = jnp.where(head == i, log_gammas[i], lg)
    rel = (lax.broadcasted_iota(jnp.int32, (C, C), 0) - lax.broadcasted_iota(jnp.int32, (C, C), 1)).astype(F32)
    intra = jnp.where(rel >= 0, jnp.exp(lg * jnp.maximum(rel, 0.0)), 0.0)
    idx = lax.broadcasted_iota(jnp.int32, (C, 1), 0).astype(F32)
    q_decay = jnp.exp(lg * (idx + 1.0))
    k_decay = jnp.exp(lg * (C - 1.0 - idx))
    state_decay = jnp.exp(lg * float(C))

    def rotate(x, cos, sin):
        x1, x2 = x[:, :half], x[:, half:]
        return jnp.concatenate([x1 * cos - x2 * sin, x2 * cos + x1 * sin], axis=1)

    for c in range(chunks):
        sl = slice(c * C, (c + 1) * C)
        cos, sin = cos_ref[sl, :], sin_ref[sl, :]
        q = rotate(q_ref[sl, :].astype(F32), cos, sin)
        k = rotate(k_ref[sl, :].astype(F32), cos, sin) * (DK ** -0.5)
        v = v_ref[sl, :]
        qb = q.astype(BF16)
        scores = _dot_nt(qb, k.astype(BF16)) * intra
        inner = _dot(scores.astype(BF16), v)
        state = state_ref[...]
        cross = _dot(qb, state.astype(BF16)) * q_decay
        kd_t = (k * k_decay).T.astype(BF16)
        state_ref[...] = state * state_decay + _dot(kd_t, v)
        out = inner + cross
        ms = jnp.mean(out * out, axis=-1, keepdims=True)
        y = out * lax.rsqrt(ms + NORM_EPS)
        g = g_ref[sl, :].astype(F32)
        o_ref[sl, :] = (y * (g * jax.nn.sigmoid(g))).astype(BF16)


def retention_core(proj, cos, sin, B, T, tb=1024):
    H = RET_HEADS
    DK = proj.shape[-1] // (6 * H)
    DV = 2 * DK
    tb = min(tb, T)
    kern = functools.partial(_ret_core_kernel, chunks=tb // RET_CHUNK)
    return pl.pallas_call(
        kern,
        grid=(B, H, T // tb),
        in_specs=[pl.BlockSpec((None, tb, DK), lambda b, h, t: (b, t, h)),
                  pl.BlockSpec((None, tb, DK), lambda b, h, t: (b, t, H + h)),
                  pl.BlockSpec((None, tb, DV), lambda b, h, t: (b, t, H + h)),
                  pl.BlockSpec((None, tb, DV), lambda b, h, t: (b, t, 2 * H + h)),
                  pl.BlockSpec((tb, DK // 2), lambda b, h, t: (t, 0)),
                  pl.BlockSpec((tb, DK // 2), lambda b, h, t: (t, 0))],
        out_specs=pl.BlockSpec((None, tb, DV), lambda b, h, t: (b, t, h)),
        out_shape=jax.ShapeDtypeStruct((B, T, H * DV), BF16),
        scratch_shapes=[pltpu.VMEM((DK, DV), F32)],
        compiler_params=_params("parallel", "parallel", "arbitrary"),
        name="retention_core",
    )(proj, proj, proj, proj, cos, sin)


def _head_prep_kernel(x_ref, gain_ref, c_ref, s1_ref, s2_ref, bd_ref, o_ref, *, norm, rope, dup, scale):
    x = x_ref[...].astype(F32)
    if norm:
        sq = x * x
        hi = sq.astype(BF16)
        lo = (sq - hi.astype(F32)).astype(BF16)
        bd = bd_ref[...]
        ss = _dot(hi, bd) + _dot(lo, bd)
        x = x * lax.rsqrt(ss * (1.0 / HEAD_DIM) + NORM_EPS) * gain_ref[...]
    if rope:
        x = x * c_ref[...] + pltpu.roll(x, LANES - 8, 1) * s1_ref[...] + pltpu.roll(x, 8, 1) * s2_ref[...]
    if scale != 1.0:
        x = x * scale
    if dup:
        lane_half = lax.broadcasted_iota(jnp.int32, x.shape, 1) // HEAD_DIM
        x = jnp.where(lane_half == pl.program_id(2), x, pltpu.roll(x, HEAD_DIM, 1))
        if dup == "ones":
            x = jnp.where(lane_half == 0, x, 1.0)
    o_ref[...] = x.astype(BF16)


def head_prep(proj, col0, ntiles, gain, tables, B, T, *, norm, rope, dup, scale=1.0, tm=1024):
    BT = proj.shape[0]
    tm = min(tm, T)
    tpb = T // tm
    c, s1, s2 = tables
    bd = jnp.asarray(np.kron(np.eye(2), np.ones((HEAD_DIM, HEAD_DIM))), BF16)
    gain2 = jnp.concatenate([gain, gain]).reshape(1, LANES).astype(F32)
    kern = functools.partial(_head_prep_kernel, norm=norm, rope=rope, dup=dup, scale=scale)
    if dup:
        grid = (BT // tm, ntiles, 2)
        imap = lambda f: (lambda i, j, h: f(i, j))
        out_spec = pl.BlockSpec((None, None, tm, LANES), lambda i, j, h: (i // tpb, 2 * j + h, i % tpb, 0))
        out_shape = jax.ShapeDtypeStruct((B, 2 * ntiles, T, LANES), BF16)
        sem = ("parallel", "parallel", "arbitrary")
    else:
        grid = (BT // tm, ntiles)
        imap = lambda f: f
        out_spec = pl.BlockSpec((tm, LANES), lambda i, j: (i, j))
        out_shape = jax.ShapeDtypeStruct((BT, ntiles * LANES), BF16)
        sem = ("parallel", "parallel")
    tab = pl.BlockSpec((tm, LANES), imap(lambda i, j: (i % tpb, 0)))
    return pl.pallas_call(
        kern,
        grid=grid,
        in_specs=[pl.BlockSpec((tm, LANES), imap(lambda i, j: (i, col0 + j))),
                  pl.BlockSpec((1, LANES), imap(lambda i, j: (0, 0))),
                  tab, tab, tab,
                  pl.BlockSpec((LANES, LANES), imap(lambda i, j: (0, 0)))],
        out_specs=out_spec,
        out_shape=out_shape,
        compiler_params=_params(*sem),
        name="head_prep",
    )(proj, gain2, c, s1, s2, bd)


def rope_tables(T):
    half = HEAD_DIM // 4 // 2
    inv = ROPE_THETA ** (-jnp.arange(half, dtype=F32) / half)
    ang = jnp.arange(T).astype(F32)[:, None] * inv[None, :]
    cos, sin = jnp.cos(ang), jnp.sin(ang)
    ones = jnp.ones((T, HEAD_DIM - 2 * half), F32)
    zeros8 = jnp.zeros((T, half), F32)
    zeros = jnp.zeros((T, HEAD_DIM - 2 * half), F32)
    c = jnp.concatenate([cos, cos, ones], axis=1)
    s1 = jnp.concatenate([-sin, zeros8, zeros], axis=1)
    s2 = jnp.concatenate([zeros8, sin, zeros], axis=1)
    return tuple(jnp.concatenate([t, t], axis=1) for t in (c, s1, s2))


def retention_tables(T, dk):
    inv = 1.0 / (RET_ROT_BASE ** jnp.linspace(0.0, 1.0, dk // 2, dtype=F32))
    ang = jnp.arange(T).astype(F32)[:, None] * inv[None, :]
    return jnp.cos(ang), jnp.sin(ang)


def _stack_heads(q, extra=None):
    first_half = lax.broadcasted_iota(jnp.int32, (q.shape[0], LANES), 1) < HEAD_DIM
    zero = jnp.zeros((q.shape[0], LANES), q.dtype)
    rows = []
    for p in range(q.shape[1] // LANES):
        qp = q[:, p * LANES:(p + 1) * LANES]
        for part in (jnp.where(first_half, qp, zero), jnp.where(first_half, zero, qp)):
            rows.append(part if extra is None else jnp.concatenate([part, extra], axis=1))
    return jnp.concatenate(rows, axis=0)


def _unstack_heads(o, cq):
    first_half = lax.broadcasted_iota(jnp.int32, (cq, LANES), 1) < HEAD_DIM
    G = o.shape[0] // cq
    pairs = [jnp.where(first_half, o[(2 * p) * cq:(2 * p + 1) * cq], o[(2 * p + 1) * cq:(2 * p + 2) * cq])
             for p in range(G // 2)]
    return jnp.concatenate(pairs, axis=1)


def _unstack_heads_lo(o, cq):
    first_half = lax.broadcasted_iota(jnp.int32, (cq, LANES), 1) < HEAD_DIM
    G = o.shape[0] // cq
    pairs = [jnp.where(first_half, o[(2 * p) * cq:(2 * p + 1) * cq],
                       pltpu.roll(o[(2 * p + 1) * cq:(2 * p + 2) * cq], HEAD_DIM, 1))
             for p in range(G // 2)]
    return jnp.concatenate(pairs, axis=1)


def _banded_kernel(*refs, G, nprev, window, use_sink, qb, unroll):
    if use_sink:
        sink_ref, q_ref, k_ref, v_ref, o_ref = refs
    else:
        q_ref, k_ref, v_ref, o_ref = refs
    Cq = QBLK
    nk = (nprev + 1) * Cq
    hk = pl.program_id(1)

    def block(i, carry):
        n = pl.program_id(2) * qb + i
        r0 = pl.multiple_of(i * Cq, Cq)
        start = pl.multiple_of(jnp.maximum(n - nprev, 0) * Cq, Cq)
        kwin = k_ref[pl.ds(start, nk), :]
        vwin = v_ref[pl.ds(start, nk), :]
        lhs = _stack_heads(q_ref[pl.ds(r0, Cq), :])
        s_all = _dot_nt(lhs, kwin)
        t = n * Cq + lax.broadcasted_iota(jnp.int32, (Cq, nk), 0)
        rel = t - (start + lax.broadcasted_iota(jnp.int32, (Cq, nk), 1))
        valid = jnp.logical_and(rel >= 0, rel < window)
        outs = []
        for g in range(G):
            s = jnp.where(valid, s_all[g * Cq:(g + 1) * Cq], NEG_INF)
            m = jnp.max(s, axis=-1, keepdims=True)
            if use_sink:
                sink = sink_ref[hk * G + g] * LOG2_E
                m = jnp.maximum(m, sink)
            p = jnp.exp2((s - m).astype(BF16))
            acc = _dot(p, vwin)
            denom = pltpu.roll(acc, HEAD_DIM, 1)
            if use_sink:
                denom = denom + jnp.exp2(sink - m)
            else:
                denom = jnp.maximum(denom, SOFTMAX_FLOOR)
            outs.append(acc / denom)
        o_ref[pl.ds(r0, Cq), :] = _unstack_heads_lo(jnp.concatenate(outs, axis=0), Cq).astype(BF16)
        return carry

    lax.fori_loop(0, qb, block, 0, unroll=unroll)


def banded_attention(q, k2, v2, sinks, B, T, *, G, nprev, window, unroll, qb=4):
    Hk = k2.shape[1]
    qb = min(qb, T // QBLK)
    use_sink = sinks is not None
    kern = functools.partial(_banded_kernel, G=G, nprev=nprev, window=window, use_sink=use_sink, qb=qb, unroll=unroll)
    in_specs = [pl.BlockSpec((None, qb * QBLK, G * HEAD_DIM), lambda b, h, n: (b, n, h)),
                pl.BlockSpec((None, None, T, LANES), lambda b, h, n: (b, h, 0, 0)),
                pl.BlockSpec((None, None, T, LANES), lambda b, h, n: (b, h, 0, 0))]
    args = [q, k2, v2]
    if use_sink:
        in_specs = [pl.BlockSpec(memory_space=pltpu.SMEM)] + in_specs
        args = [sinks.astype(F32)] + args
    return pl.pallas_call(
        kern,
        grid=(B, Hk, T // (qb * QBLK)),
        in_specs=in_specs,
        out_specs=pl.BlockSpec((None, qb * QBLK, G * HEAD_DIM), lambda b, h, n: (b, n, h)),
        out_shape=jax.ShapeDtypeStruct((B, T, Hk * G * HEAD_DIM), BF16),
        compiler_params=_params("parallel", "parallel", "arbitrary"),
        name="banded_attention",
    )(*args)


def _nsa_compress_kernel(x_ref, pe_ref, w1_ref, w2_ref, gain_ref, bd_ref, o_ref, *, norm):
    x = x_ref[...]
    gw = x.shape[1]
    u0 = _dot(x, w1_ref[:gw, :])
    u1 = _dot(x, w1_ref[gw:, :])
    n = u1.shape[0]
    pe8 = jnp.broadcast_to(pe_ref[...], (8, pe_ref.shape[1]))
    const = _dot(pe8, w1_ref[...])[:1]
    pre = u0 + pltpu.roll(u1, n - 1, 0) + const
    hid = (pre * jax.nn.sigmoid(pre)).astype(BF16)
    y = _dot(hid, w2_ref[...])
    if norm:
        sq = y * y
        hi = sq.astype(BF16)
        lo = (sq - hi.astype(F32)).astype(BF16)
        bd = bd_ref[...]
        ss = _dot(hi, bd) + _dot(lo, bd)
        y = y * lax.rsqrt(ss * (1.0 / HEAD_DIM) + NORM_EPS) * gain_ref[...]
    o_ref[...] = y.astype(BF16)


def nsa_compress(xg, pe, w1, w2, gain, *, norm):
    B, Hk, NG, GW = xg.shape
    bd = jnp.asarray(np.kron(np.eye(2), np.ones((HEAD_DIM, HEAD_DIM))), BF16)
    gain2 = jnp.concatenate([gain, gain]).reshape(1, LANES).astype(F32)
    w2d = jnp.concatenate([w2, w2], axis=1).astype(BF16)
    hid = w1.shape[1]
    return pl.pallas_call(
        functools.partial(_nsa_compress_kernel, norm=norm),
        grid=(B, Hk),
        in_specs=[pl.BlockSpec((None, None, NG, GW), lambda b, h: (b, h, 0, 0)),
                  pl.BlockSpec((1, 2 * GW), lambda b, h: (0, 0)),
                  pl.BlockSpec((2 * GW, hid), lambda b, h: (0, 0)),
                  pl.BlockSpec((hid, LANES), lambda b, h: (0, 0)),
                  pl.BlockSpec((1, LANES), lambda b, h: (0, 0)),
                  pl.BlockSpec((LANES, LANES), lambda b, h: (0, 0))],
        out_specs=pl.BlockSpec((None, None, NG, LANES), lambda b, h: (b, h, 0, 0)),
        out_shape=jax.ShapeDtypeStruct((B, Hk, NG, LANES), BF16),
        compiler_params=_params("parallel", "parallel"),
        name="nsa_compress",
    )(xg, pe.reshape(1, 2 * GW).astype(BF16), w1.astype(BF16), w2d, gain2, bd)


def _topk_rows_bias(v, n_sel):
    nslab = v.shape[0] // 8
    sub = lax.broadcasted_iota(jnp.int32, (8, v.shape[1]), 0).astype(F32)
    vals = [v[8 * k:8 * k + 8] for k in range(nslab)]
    ids = [sub + 8.0 * k for k in range(nslab)]
    for _ in range(n_sel):
        cand = list(zip(vals, ids))
        while len(cand) > 1:
            nxt = []
            for (va, ia), (vb, ib) in zip(cand[0::2], cand[1::2]):
                take_b = vb > va
                nxt.append((jnp.where(take_b, vb, va), jnp.where(take_b, ib, ia)))
            cand = nxt + ([cand[-1]] if len(cand) % 2 else [])
        bv, bi = cand[0]
        for shift in (4, 2, 1):
            rv, ri = pltpu.roll(bv, shift, 0), pltpu.roll(bi, shift, 0)
            take_r = jnp.logical_or(rv > bv, jnp.logical_and(rv == bv, ri < bi))
            bv, bi = jnp.where(take_r, rv, bv), jnp.where(take_r, ri, bi)
        vals = [jnp.where(ids[k] == bi, REMOVED, vals[k]) for k in range(nslab)]
    return jnp.concatenate([jnp.where(vk == REMOVED, 0.0, MASK_BIAS) for vk in vals], axis=0)


def _nsa_cmp_kernel(q_ref, kc_ref, vc_ref, ov_ref, oc_ref, mneg_ref, *, n_sel, qb):
    Cq = QBLK
    G = q_ref.shape[1] // HEAD_DIM
    ncp = kc_ref.shape[0]
    ov = ov_ref[...]
    for i in range(qb):
        n = pl.program_id(2) * qb + i
        rows = slice(i * Cq, (i + 1) * Cq)
        s_all = _dot_nt(_stack_heads(q_ref[rows, :]), kc_ref[...])
        t = n * Cq + lax.broadcasted_iota(jnp.int32, (Cq, ncp), 0)
        cmp_end = lax.broadcasted_iota(jnp.int32, (Cq, ncp), 1) * NSA_CMP_STRIDE + (NSA_CMP_LEN - 1)
        valid = cmp_end <= t
        outs = []
        psum = None
        for g in range(G):
            s = jnp.where(valid, s_all[g * Cq:(g + 1) * Cq], NEG_INF)
            p = jnp.where(valid, jnp.exp2(s - jnp.max(s, axis=-1, keepdims=True)), 0.0)
            p = p / jnp.maximum(jnp.sum(p, axis=-1, keepdims=True), SOFTMAX_FLOOR)
            outs.append(_dot(p.astype(BF16), vc_ref[...]))
            psum = p if psum is None else psum + p
        oc_ref[rows, :] = _unstack_heads(jnp.concatenate(outs, axis=0), Cq).astype(BF16)

        hi = psum.astype(BF16)
        lo = (psum - hi.astype(F32)).astype(BF16)
        imp = (_dot(hi, ov) + _dot(lo, ov)).T
        nsb = imp.shape[0]
        j = lax.broadcasted_iota(jnp.int32, (nsb, Cq), 0)
        tt = n * Cq + lax.broadcasted_iota(jnp.int32, (nsb, Cq), 1)
        cur = tt // NSA_SEL_LEN
        forced = jnp.logical_or(j == 0, jnp.logical_or(j == cur, j == cur - 1))
        v = jnp.where(forced, NSA_FORCE_SCORE, imp)
        v = jnp.where(j * NSA_SEL_LEN <= tt, v, NEG_INF)
        mneg_ref[rows, :] = _topk_rows_bias(v, n_sel).T.astype(BF16)


def nsa_cmp_topk(q, kcmp, vcmp, ov, B, T, n_sel, qb=2):
    Hk = kcmp.shape[1]
    GW = q.shape[-1] // Hk
    ncp = kcmp.shape[2]
    cmp_spec = pl.BlockSpec((None, None, ncp, LANES), lambda b, h, n: (b, h, 0, 0))
    tq = qb * QBLK
    return pl.pallas_call(
        functools.partial(_nsa_cmp_kernel, n_sel=n_sel, qb=qb),
        grid=(B, Hk, T // tq),
        in_specs=[pl.BlockSpec((None, tq, GW), lambda b, h, n: (b, n, h)),
                  cmp_spec, cmp_spec,
                  pl.BlockSpec((ncp, LANES), lambda b, h, n: (0, 0))],
        out_specs=[pl.BlockSpec((None, tq, GW), lambda b, h, n: (b, n, h)),
                   pl.BlockSpec((None, None, tq, LANES), lambda b, h, n: (b, h, n, 0))],
        out_shape=[jax.ShapeDtypeStruct((B, T, q.shape[-1]), BF16),
                   jax.ShapeDtypeStruct((B, Hk, T, LANES), BF16)],
        compiler_params=_params("parallel", "parallel", "arbitrary"),
        name="nsa_cmp_topk",
    )(q, kcmp, vcmp, ov)


def _nsa_sel_kernel(q_ref, mneg_ref, k_ref, e_ref, v_ref, o_ref, lhs_ref, m_ref, acc_ref):
    Cq = QBLK
    n = pl.program_id(2)
    lhs_ref[...] = _stack_heads(q_ref[...], extra=mneg_ref[...])
    rows = lhs_ref.shape[0]
    m_ref[...] = jnp.full(m_ref.shape, NEG_INF, F32)
    acc_ref[...] = jnp.zeros_like(acc_ref)

    def chunk(off, kc, causal):
        rhs = jnp.concatenate([k_ref[pl.ds(off, kc), :], e_ref[pl.ds(off, kc), :]], axis=1)
        s = _dot_nt(lhs_ref[...], rhs)
        if causal:
            t = n * Cq + (lax.broadcasted_iota(jnp.int32, (rows, kc), 0) & (Cq - 1))
            s = jnp.where(off + lax.broadcasted_iota(jnp.int32, (rows, kc), 1) <= t, s, NEG_INF)
        m_prev = m_ref[...]
        m_new = jnp.maximum(m_prev, jnp.max(s, axis=-1, keepdims=True))
        p = jnp.exp2((s - jnp.concatenate([m_new] * (kc // LANES), axis=1)).astype(BF16))
        acc_ref[...] = jnp.exp2(m_prev - m_new) * acc_ref[...] + _dot(p, v_ref[pl.ds(off, kc), :])
        m_ref[...] = m_new

    per_chunk = SEL_CHUNK // Cq
    nmain = n // per_chunk

    def body(c, carry):
        chunk(pl.multiple_of(2 * c * SEL_CHUNK, SEL_CHUNK), SEL_CHUNK, False)
        chunk(pl.multiple_of((2 * c + 1) * SEL_CHUNK, SEL_CHUNK), SEL_CHUNK, False)
        return carry

    lax.fori_loop(0, nmain // 2, body, 0)

    @pl.when(nmain % 2 == 1)
    def _():
        chunk(pl.multiple_of((nmain - 1) * SEL_CHUNK, SEL_CHUNK), SEL_CHUNK, False)

    tail0 = pl.multiple_of(nmain * SEL_CHUNK, SEL_CHUNK)
    chunk(tail0, SEL_TAIL, True)
    for i in range(1, SEL_CHUNK // SEL_TAIL):
        @pl.when(n - nmain * per_chunk >= i * (SEL_TAIL // Cq))
        def _():
            chunk(tail0 + i * SEL_TAIL, SEL_TAIL, True)

    acc = acc_ref[...]
    o = acc / jnp.maximum(pltpu.roll(acc, HEAD_DIM, 1), SOFTMAX_FLOOR)
    o_ref[...] = _unstack_heads_lo(o, Cq).astype(BF16)


def nsa_selected(q, mneg, k2, e, v2, B, T):
    Hk = k2.shape[1]
    GW = q.shape[-1] // Hk
    G = GW // HEAD_DIM
    kv_spec = pl.BlockSpec((None, None, T, LANES), lambda b, h, n: (b, h, 0, 0))
    return pl.pallas_call(
        _nsa_sel_kernel,
        grid=(B, Hk, T // QBLK),
        in_specs=[pl.BlockSpec((None, QBLK, GW), lambda b, h, n: (b, n, h)),
                  pl.BlockSpec((None, None, QBLK, LANES), lambda b, h, n: (b, h, n, 0)),
                  kv_spec,
                  pl.BlockSpec((T, LANES), lambda b, h, n: (0, 0)),
                  kv_spec],
        out_specs=pl.BlockSpec((None, QBLK, GW), lambda b, h, n: (b, n, h)),
        out_shape=jax.ShapeDtypeStruct((B, T, q.shape[-1]), BF16),
        scratch_shapes=[pltpu.VMEM((G * QBLK, 2 * LANES), BF16),
                        pltpu.VMEM((G * QBLK, LANES), F32),
                        pltpu.VMEM((G * QBLK, LANES), F32)],
        compiler_params=_params("parallel", "parallel", "arbitrary"),
        name="nsa_selected",
    )(q, mneg, k2, e, v2)


def _nsa_out_kernel(oc_ref, os_ref, ow_ref, gates_ref, e_ref, w_ref, x_ref, gate_ref, o_ref):
    D = oc_ref.shape[1]
    sg = jax.nn.sigmoid(gates_ref[...].astype(F32)).astype(BF16)
    gx = _dot(sg, e_ref[...])
    y = (gx[:, :D] * oc_ref[...].astype(F32) + gx[:, D:2 * D] * os_ref[...].astype(F32)
         + gx[:, 2 * D:] * ow_ref[...].astype(F32))
    o_ref[...] = x_ref[...] + gate_ref[...] * _dot(y.astype(BF16), w_ref[...])


def nsa_out(oc, os_, ow, proj, gate_col_tile, expand, w, x, mod, gate_idx, T, tm=512):
    BT, D = x.shape
    tm = min(tm, T)
    tpb = T // tm
    row = pl.BlockSpec((tm, D), lambda i: (i, 0))
    return pl.pallas_call(
        _nsa_out_kernel,
        grid=(BT // tm,),
        in_specs=[row, row, row,
                  pl.BlockSpec((tm, LANES), lambda i: (i, gate_col_tile)),
                  pl.BlockSpec((LANES, 3 * D), lambda i: (0, 0)),
                  pl.BlockSpec((D, D), lambda i: (0, 0)),
                  row,
                  pl.BlockSpec((None, None, 1, D), lambda i: (i // tpb, gate_idx, 0, 0))],
        out_specs=row,
        out_shape=jax.ShapeDtypeStruct((BT, D), F32),
        compiler_params=_params("parallel"),
        name="nsa_out",
    )(oc, os_, ow, proj, expand, w, x, mod)


def retention_mixer(x, mod, norm_g, w_in, w_out, B, T):
    H = RET_HEADS
    D = x.shape[1]
    dk = D // H
    head_perm = np.concatenate([np.arange(0, dk, 2), np.arange(1, dk, 2)])
    perm = np.concatenate([hd * dk + head_perm for hd in range(2 * H)] + [np.arange(2 * H * dk, w_in.shape[1])])
    proj = normmod_matmul(x, norm_g, mod, 0, w_in[:, perm].astype(BF16), T)
    cos, sin = retention_tables(T, dk)
    y = retention_core(proj.reshape(B, T, -1), cos, sin, B, T)
    return outproj_residual(y.reshape(B * T, -1), w_out.astype(BF16), x, mod, 2, T)


def swa_mixer(x, mod, norm_g, w_in, q_norm_g, k_norm_g, sinks, w_out, B, T):
    D = x.shape[1]
    Hk = SWA_KV_HEADS
    G = D // HEAD_DIM // Hk
    proj = normmod_matmul(x, norm_g, mod, 0, w_in.astype(BF16), T, tn=256)
    tables = rope_tables(T)
    nq = D // LANES
    nkv = Hk * HEAD_DIM // LANES
    q = head_prep(proj, 0, nq, q_norm_g, tables, B, T, norm=True, rope=True, dup=False, scale=Q_SCALE)
    k2 = head_prep(proj, nq, nkv, k_norm_g, tables, B, T, norm=True, rope=True, dup=True)
    v2 = head_prep(proj, nq + nkv, nkv, k_norm_g, tables, B, T, norm=False, rope=False, dup="ones")
    o = banded_attention(q.reshape(B, T, D), k2, v2, sinks, B, T, G=G, nprev=SWA_WINDOW // QBLK, window=SWA_WINDOW,
                         unroll=False)
    return outproj_residual(o.reshape(B * T, D), w_out.astype(BF16), x, mod, 2, T)


def nsa_mixer(x, mod, norm_g, w_in, q_norm_g, k_norm_g, cmp_pe, cmp_w1, cmp_w2, w_out, B, T):
    D = x.shape[1]
    Hk = NSA_KV_HEADS
    H = D // HEAD_DIM
    G = H // Hk
    kvw = Hk * HEAD_DIM
    n_main = D + 6 * kvw
    n_pad = -(-w_in.shape[1] // LANES) * LANES
    w_pad = jnp.pad(w_in, ((0, 0), (0, n_pad - w_in.shape[1]))).astype(BF16)
    proj = normmod_matmul(x, norm_g, mod, 0, w_pad, T, tn=LANES * 3)
    tables = rope_tables(T)
    nq = D // LANES
    nkv = kvw // LANES
    q = head_prep(proj, 0, nq, q_norm_g, tables, B, T, norm=True, rope=True, dup=False, scale=Q_SCALE)
    q = q.reshape(B, T, D)
    col = lambda i: nq + i * nkv
    ks2 = head_prep(proj, col(2), nkv, k_norm_g[1], tables, B, T, norm=True, rope=True, dup=True)
    vs2 = head_prep(proj, col(3), nkv, k_norm_g[1], tables, B, T, norm=False, rope=False, dup="ones")
    kw2 = head_prep(proj, col(4), nkv, k_norm_g[2], tables, B, T, norm=True, rope=True, dup=True)
    vw2 = head_prep(proj, col(5), nkv, k_norm_g[2], tables, B, T, norm=False, rope=False, dup="ones")

    S = NSA_CMP_STRIDE
    p3 = proj.reshape(B, T, n_pad)

    def groups(c0):
        a = p3[:, :, c0:c0 + kvw].reshape(B, T // S, S, Hk, HEAD_DIM)
        return a.transpose(0, 3, 1, 2, 4).reshape(B, Hk, T // S, S * HEAD_DIM)

    kcmp = nsa_compress(groups(D), cmp_pe[0], cmp_w1[0], cmp_w2[0], k_norm_g[0], norm=True)
    vcmp = nsa_compress(groups(D + kvw), cmp_pe[1], cmp_w1[1], cmp_w2[1], k_norm_g[0], norm=False)

    ncp = T // S
    ns = T // NSA_SEL_LEN
    cs = np.arange(ncp)[:, None] * S
    js = np.arange(LANES)[None, :]
    overlap = ((cs < js * NSA_SEL_LEN + NSA_SEL_LEN) & (cs + NSA_CMP_LEN > js * NSA_SEL_LEN)
               & (js < ns) & (np.arange(ncp)[:, None] < ncp - 1))
    ov = jnp.asarray(overlap, BF16)
    oc, mneg = nsa_cmp_topk(q, kcmp, vcmp, ov, B, T, min(NSA_N_SEL, ns))

    blk_onehot = jnp.asarray(np.arange(T)[:, None] // NSA_SEL_LEN == js, BF16)
    os_ = nsa_selected(q, mneg, ks2, blk_onehot, vs2, B, T)
    ow = banded_attention(q, kw2, vw2, None, B, T, G=G, nprev=NSA_WINDOW // QBLK, window=NSA_WINDOW, unroll=True)

    rows = np.arange(LANES)[:, None]
    cols = np.arange(3 * D)[None, :]
    expand = jnp.asarray(rows == 3 * ((cols % D) // HEAD_DIM) + cols // D, BF16)
    flat = lambda a: a.reshape(B * T, D)
    return nsa_out(flat(oc), flat(os_), flat(ow), proj, n_main // LANES, expand, w_out.astype(BF16), x, mod, 2, T)


def kernel(x, c, l0_mod_w, l0_mod_b, l0_norm1_g, l0_ret_w_in, l0_ret_w_out, l0_norm2_g, l0_ffn_w_in, l0_ffn_w_out, l1_mod_w, l1_mod_b, l1_norm1_g, l1_swa_w_in, l1_swa_q_norm_g, l1_swa_k_norm_g, l1_swa_sinks, l1_swa_w_out, l1_norm2_g, l1_ffn_w_in, l1_ffn_w_out, l2_mod_w, l2_mod_b, l2_norm1_g, l2_nsa_w_in, l2_nsa_q_norm_g, l2_nsa_k_norm_g, l2_nsa_cmp_pe, l2_nsa_cmp_w1, l2_nsa_cmp_w2, l2_nsa_w_out, l2_norm2_g, l2_ffn_w_in, l2_ffn_w_out, l3_mod_w, l3_mod_b, l3_norm1_g, l3_ret_w_in, l3_ret_w_out, l3_norm2_g, l3_ffn_w_in, l3_ffn_w_out):
    B, T, D = x.shape
    layers = [
        (l0_mod_w, l0_mod_b, l0_norm1_g, retention_mixer, (l0_ret_w_in, l0_ret_w_out), l0_norm2_g, l0_ffn_w_in, l0_ffn_w_out),
        (l1_mod_w, l1_mod_b, l1_norm1_g, swa_mixer,
         (l1_swa_w_in, l1_swa_q_norm_g, l1_swa_k_norm_g, l1_swa_sinks, l1_swa_w_out), l1_norm2_g, l1_ffn_w_in, l1_ffn_w_out),
        (l2_mod_w, l2_mod_b, l2_norm1_g, nsa_mixer,
         (l2_nsa_w_in, l2_nsa_q_norm_g, l2_nsa_k_norm_g, l2_nsa_cmp_pe, l2_nsa_cmp_w1, l2_nsa_cmp_w2, l2_nsa_w_out),
         l2_norm2_g, l2_ffn_w_in, l2_ffn_w_out),
        (l3_mod_w, l3_mod_b, l3_norm1_g, retention_mixer, (l3_ret_w_in, l3_ret_w_out), l3_norm2_g, l3_ffn_w_in, l3_ffn_w_out),
    ]
    h = x.reshape(B * T, D)
    for mod_w, mod_b, norm1_g, mixer, mixer_params, norm2_g, ffn_w_in, ffn_w_out in layers:
        mod = adaln_mod(c, mod_w, mod_b)
        h = mixer(h, mod, norm1_g, *mixer_params, B, T)
        h = ffn_block(h, norm2_g, mod, ffn_w_in.astype(BF16), ffn_w_out.astype(BF16), T)
    return h.reshape(B, T, D)
```

```python
import functools

import jax
import jax.numpy as jnp
import numpy as np
from jax import lax
from jax.experimental import pallas as pl
from jax.experimental.pallas import tpu as pltpu

F32 = jnp.float32
BF16 = jnp.bfloat16

NORM_EPS = 1e-6
NEG_INF = -1e30
SOFTMAX_FLOOR = 1e-30
REMOVED = -3e38
MASK_BIAS = -(2.0 ** 30)

V7X_VMEM_LIMIT_BYTES = 56 * 2 ** 20
LANES = 128
QBLK = 128

ROPE_THETA = 500000.0
RET_ROT_BASE = 10000.0
RET_HEADS = 4
RET_CHUNK = 256
SWA_KV_HEADS = 2
SWA_WINDOW = 128
NSA_KV_HEADS = 4
NSA_CMP_LEN = 32
NSA_CMP_STRIDE = 16
NSA_SEL_LEN = 64
NSA_N_SEL = 16
NSA_WINDOW = 512
NSA_FORCE_SCORE = 1e4
HEAD_DIM = 64
SEL_CHUNK = 1024
SEL_TAIL = 512
SEL_Q = 256
Q_SCALE = HEAD_DIM ** -0.5

NT_DIMS = (((1,), (1,)), ((), ()))


def _params(*sem):
    return pltpu.CompilerParams(dimension_semantics=sem, vmem_limit_bytes=V7X_VMEM_LIMIT_BYTES)


def _dot(a, b):
    return jnp.dot(a, b, preferred_element_type=F32)


def _dot_nt(a, b):
    return lax.dot_general(a, b, NT_DIMS, preferred_element_type=F32)


def _mod_kernel(c_ref, w_ref, b_ref, o_ref):
    c = c_ref[...]
    s = c * jax.nn.sigmoid(c)
    nb = s.shape[0]
    s8 = jnp.concatenate([s, jnp.zeros((8 - nb, s.shape[1]), F32)], axis=0)
    r = jnp.dot(s8, w_ref[...], preferred_element_type=F32, precision=lax.Precision.HIGHEST)
    o_ref[...] = r[:nb] + b_ref[...]


def adaln_mod(c, mod_w, mod_b):
    B, D = c.shape
    N = mod_w.shape[1]
    tn = N // 4
    out = pl.pallas_call(
        _mod_kernel,
        grid=(N // tn,),
        in_specs=[pl.BlockSpec((B, D), lambda j: (0, 0)),
                  pl.BlockSpec((D, tn), lambda j: (0, j)),
                  pl.BlockSpec((1, tn), lambda j: (0, j))],
        out_specs=pl.BlockSpec((B, tn), lambda j: (0, j)),
        out_shape=jax.ShapeDtypeStruct((B, N), F32),
        compiler_params=_params("arbitrary"),
        name="adaln_mod",
    )(c, mod_w, mod_b.reshape(1, N))
    return out.reshape(B, 6, 1, D)


def _normmod(x, g, scale, shift):
    ms = jnp.mean(x * x, axis=-1, keepdims=True)
    y = x * lax.rsqrt(ms + NORM_EPS) * g
    return y * (1.0 + scale) + shift


def _resident(shape):
    return pl.BlockSpec(shape, lambda *_: (0,) * len(shape), pipeline_mode=pl.Buffered(1))


def _normmod_matmul_kernel(x_ref, g_ref, scale_ref, shift_ref, w_ref, o_ref, *, tn):
    h = _normmod(x_ref[...], g_ref[...], scale_ref[...], shift_ref[...]).astype(BF16)
    for c in range(w_ref.shape[1] // tn):
        cols = slice(c * tn, (c + 1) * tn)
        o_ref[:, cols] = _dot(h, w_ref[:, cols]).astype(o_ref.dtype)


def normmod_matmul(x, g, mod, shift_idx, w, T, tm=512, tn=512):
    BT, D = x.shape
    N = w.shape[1]
    tm = min(tm, T)
    tn = tn if N % tn == 0 else LANES
    tpb = T // tm
    return pl.pallas_call(
        functools.partial(_normmod_matmul_kernel, tn=tn),
        grid=(BT // tm,),
        in_specs=[pl.BlockSpec((tm, D), lambda i: (i, 0)),
                  _resident((1, D)),
                  pl.BlockSpec((None, None, 1, D), lambda i: (i // tpb, shift_idx + 1, 0, 0)),
                  pl.BlockSpec((None, None, 1, D), lambda i: (i // tpb, shift_idx, 0, 0)),
                  _resident((D, N))],
        out_specs=pl.BlockSpec((tm, N), lambda i: (i, 0)),
        out_shape=jax.ShapeDtypeStruct((BT, N), BF16),
        compiler_params=_params("parallel"),
        name="normmod_matmul",
    )(x, g.reshape(1, D), mod, mod, w)


def _outproj_kernel(y_ref, w_ref, x_ref, gate_ref, o_ref):
    o_ref[...] = x_ref[...] + gate_ref[...] * _dot(y_ref[...], w_ref[...])


def outproj_residual(y, w, x, mod, gate_idx, T, tm=512):
    BT, K = y.shape
    D = w.shape[1]
    tm = min(tm, T)
    tpb = T // tm
    return pl.pallas_call(
        _outproj_kernel,
        grid=(BT // tm,),
        in_specs=[pl.BlockSpec((tm, K), lambda i: (i, 0)),
                  _resident((K, D)),
                  pl.BlockSpec((tm, D), lambda i: (i, 0)),
                  pl.BlockSpec((None, None, 1, D), lambda i: (i // tpb, gate_idx, 0, 0))],
        out_specs=pl.BlockSpec((tm, D), lambda i: (i, 0)),
        out_shape=jax.ShapeDtypeStruct((BT, D), F32),
        compiler_params=_params("parallel"),
        name="outproj_residual",
    )(y, w, x, mod)


def _ffn_kernel(x_ref, g_ref, scale_ref, shift_ref, gate_ref, wi_ref, wo_ref, o_ref, *, tc):
    x = x_ref[...]
    h = _normmod(x, g_ref[...], scale_ref[...], shift_ref[...]).astype(BF16)
    F = wo_ref.shape[0]
    acc = None
    for c in range(F // tc):
        a = _dot(h, wi_ref[:, c * tc:(c + 1) * tc])
        b = _dot(h, wi_ref[:, F + c * tc:F + (c + 1) * tc])
        u = (a * jax.nn.sigmoid(a) * b).astype(BF16)
        part = _dot(u, wo_ref[c * tc:(c + 1) * tc, :])
        acc = part if acc is None else acc + part
    o_ref[...] = x + gate_ref[...] * acc


def ffn_block(x, g, mod, w_in, w_out, T, tm=512, tc=256):
    BT, D = x.shape
    F = w_out.shape[0]
    tm = min(tm, T)
    tpb = T // tm
    modspec = lambda idx: pl.BlockSpec((None, None, 1, D), lambda i: (i // tpb, idx, 0, 0))
    return pl.pallas_call(
        functools.partial(_ffn_kernel, tc=tc),
        grid=(BT // tm,),
        in_specs=[pl.BlockSpec((tm, D), lambda i: (i, 0)),
                  _resident((1, D)),
                  modspec(4), modspec(3), modspec(5),
                  _resident((D, 2 * F)),
                  _resident((F, D))],
        out_specs=pl.BlockSpec((tm, D), lambda i: (i, 0)),
        out_shape=jax.ShapeDtypeStruct((BT, D), F32),
        compiler_params=_params("parallel"),
        name="ffn_block",
    )(x, g.reshape(1, D), mod, mod, mod, w_in, w_out)


def _ret_core_kernel(q_ref, k_ref, v_ref, g_ref, cos_ref, sin_ref, o_ref, state_ref, *, chunks):
    C = RET_CHUNK
    DK = q_ref.shape[-1]
    half = DK // 2

    @pl.when(pl.program_id(2) == 0)
    def _():
        state_ref[...] = jnp.zeros_like(state_ref)

    head = pl.program_id(1)
    log_gammas = [float(np.log(1.0 - 2.0 ** (-5.0 - i))) for i in range(RET_HEADS)]
    lg = jnp.float32(log_gammas[-1])
    for i in range(RET_HEADS - 1):
        lg = jnp.where(head == i, log_gammas[i], lg)
    rel = (lax.broadcasted_iota(jnp.int32, (C, C), 0) - lax.broadcasted_iota(jnp.int32, (C, C), 1)).astype(F32)
    intra = jnp.where(rel >= 0, jnp.exp(lg * jnp.maximum(rel, 0.0)), 0.0)
    idx = lax.broadcasted_iota(jnp.int32, (C, 1), 0).astype(F32)
    q_decay = jnp.exp(lg * (idx + 1.0))
    k_decay = jnp.exp(lg * (C - 1.0 - idx))
    state_decay = jnp.exp(lg * float(C))

    def rotate(x, cos, sin):
        x1, x2 = x[:, :half], x[:, half:]
        return jnp.concatenate([x1 * cos - x2 * sin, x2 * cos + x1 * sin], axis=1)

    for c in range(chunks):
        sl = slice(c * C, (c + 1) * C)
        cos, sin = cos_ref[sl, :], sin_ref[sl, :]
        q = rotate(q_ref[sl, :].astype(F32), cos, sin)
        k = rotate(k_ref[sl, :].astype(F32), cos, sin) * (DK ** -0.5)
        v = v_ref[sl, :]
        qb = q.astype(BF16)
        scores = _dot_nt(qb, k.astype(BF16)) * intra
        inner = _dot(scores.astype(BF16), v)
        state = state_ref[...]
        cross = _dot(qb, state.astype(BF16)) * q_decay
        kd_t = (k * k_decay).T.astype(BF16)
        state_ref[...] = state * state_decay + _dot(kd_t, v)
        out = inner + cross
        ms = jnp.mean(out * out, axis=-1, keepdims=True)
        y = out * lax.rsqrt(ms + NORM_EPS)
        g = g_ref[sl, :].astype(F32)
        o_ref[sl, :] = (y * (g * jax.nn.sigmoid(g))).astype(BF16)


def retention_core(proj, cos, sin, B, T, tb=1024):
    H = RET_HEADS
    DK = proj.shape[-1] // (6 * H)
    DV = 2 * DK
    tb = min(tb, T)
    kern = functools.partial(_ret_core_kernel, chunks=tb // RET_CHUNK)
    return pl.pallas_call(
        kern,
        grid=(B, H, T // tb),
        in_specs=[pl.BlockSpec((None, tb, DK), lambda b, h, t: (b, t, h)),
                  pl.BlockSpec((None, tb, DK), lambda b, h, t: (b, t, H + h)),
                  pl.BlockSpec((None, tb, DV), lambda b, h, t: (b, t, H + h)),
                  pl.BlockSpec((None, tb, DV), lambda b, h, t: (b, t, 2 * H + h)),
                  pl.BlockSpec((tb, DK // 2), lambda b, h, t: (t, 0)),
                  pl.BlockSpec((tb, DK // 2), lambda b, h, t: (t, 0))],
        out_specs=pl.BlockSpec((None, tb, DV), lambda b, h, t: (b, t, h)),
        out_shape=jax.ShapeDtypeStruct((B, T, H * DV), BF16),
        scratch_shapes=[pltpu.VMEM((DK, DV), F32)],
        compiler_params=_params("parallel", "parallel", "arbitrary"),
        name="retention_core",
    )(proj, proj, proj, proj, cos, sin)


def _head_prep_kernel(x_ref, gains_ref, c_ref, s1_ref, s2_ref, bd_ref, *out_refs, plan):
    c, s1, s2 = c_ref[...], s1_ref[...], s2_ref[...]
    bd = bd_ref[...]
    first_half = lax.broadcasted_iota(jnp.int32, c.shape, 1) < HEAD_DIM
    for src, gain_row, rope, scale, dst in plan:
        x = x_ref[:, src * LANES:(src + 1) * LANES].astype(F32)
        if gain_row is not None:
            ss = _dot((x * x).astype(BF16), bd)
            x = x * lax.rsqrt(ss * (1.0 / HEAD_DIM) + NORM_EPS) * gains_ref[gain_row:gain_row + 1, :]
        if rope:
            x = x * c + pltpu.roll(x, LANES - 8, 1) * s1 + pltpu.roll(x, 8, 1) * s2
        if scale != 1.0:
            x = x * scale
        if dst[0] == "flat":
            out_refs[dst[1]][:, dst[2] * LANES:(dst[2] + 1) * LANES] = x.astype(BF16)
        else:
            _, out, head0, ones = dst
            swapped = pltpu.roll(x, HEAD_DIM, 1)
            a = jnp.where(first_half, x, 1.0 if ones else swapped)
            b = jnp.where(first_half, swapped, 1.0 if ones else x)
            out_refs[out][head0] = a.astype(BF16)
            out_refs[out][head0 + 1] = b.astype(BF16)


def head_prep(proj, gains, tables, plan, n_flat_tiles, head_outs, B, T, tm=512):
    BT, N = proj.shape
    tm = min(tm, T)
    tpb = T // tm
    bd = jnp.asarray(np.kron(np.eye(2), np.ones((HEAD_DIM, HEAD_DIM))), BF16)
    gains2 = jnp.concatenate([gains, gains], axis=1).astype(F32)
    tab = pl.BlockSpec((tm, LANES), lambda i: (i % tpb, 0))
    out_specs = [pl.BlockSpec((tm, n_flat_tiles * LANES), lambda i: (i, 0))]
    out_shape = [jax.ShapeDtypeStruct((BT, n_flat_tiles * LANES), BF16)]
    for nh in head_outs:
        out_specs.append(pl.BlockSpec((None, nh, tm, LANES), lambda i: (i // tpb, 0, i % tpb, 0)))
        out_shape.append(jax.ShapeDtypeStruct((B, nh, T, LANES), BF16))
    return pl.pallas_call(
        functools.partial(_head_prep_kernel, plan=tuple(plan)),
        grid=(BT // tm,),
        in_specs=[pl.BlockSpec((tm, N), lambda i: (i, 0)),
                  _resident(gains2.shape),
                  tab, tab, tab,
                  _resident((LANES, LANES))],
        out_specs=out_specs,
        out_shape=out_shape,
        compiler_params=_params("parallel"),
        name="head_prep",
    )(proj, gains2, *tables, bd)


def rope_tables(T):
    half = HEAD_DIM // 4 // 2
    inv = ROPE_THETA ** (-jnp.arange(half, dtype=F32) / half)
    lane = np.arange(LANES) % HEAD_DIM
    lo, hi = lane < half, (lane >= half) & (lane < 2 * half)
    inv_lane = jnp.where(lo | hi, inv[lane % half], 0.0)
    ang = jnp.arange(T).astype(F32)[:, None] * inv_lane[None, :]
    sin = jnp.sin(ang)
    return jnp.cos(ang), jnp.where(lo, -sin, 0.0), jnp.where(hi, sin, 0.0)


def retention_tables(T, dk):
    inv = 1.0 / (RET_ROT_BASE ** jnp.linspace(0.0, 1.0, dk // 2, dtype=F32))
    ang = jnp.arange(T).astype(F32)[:, None] * inv[None, :]
    return jnp.cos(ang), jnp.sin(ang)


def _stack_heads(q, extra=None):
    first_half = lax.broadcasted_iota(jnp.int32, (q.shape[0], LANES), 1) < HEAD_DIM
    zero = jnp.zeros((q.shape[0], LANES), q.dtype)
    rows = []
    for p in range(q.shape[1] // LANES):
        qp = q[:, p * LANES:(p + 1) * LANES]
        for part in (jnp.where(first_half, qp, zero), jnp.where(first_half, zero, qp)):
            rows.append(part if extra is None else jnp.concatenate([part, extra], axis=1))
    return jnp.concatenate(rows, axis=0)


def _unstack_heads(o, cq):
    first_half = lax.broadcasted_iota(jnp.int32, (cq, LANES), 1) < HEAD_DIM
    G = o.shape[0] // cq
    pairs = [jnp.where(first_half, o[(2 * p) * cq:(2 * p + 1) * cq], o[(2 * p + 1) * cq:(2 * p + 2) * cq])
             for p in range(G // 2)]
    return jnp.concatenate(pairs, axis=1)


def _unstack_heads_lo(o, cq):
    first_half = lax.broadcasted_iota(jnp.int32, (cq, LANES), 1) < HEAD_DIM
    G = o.shape[0] // cq
    pairs = [jnp.where(first_half, o[(2 * p) * cq:(2 * p + 1) * cq],
                       pltpu.roll(o[(2 * p + 1) * cq:(2 * p + 2) * cq], HEAD_DIM, 1))
             for p in range(G // 2)]
    return jnp.concatenate(pairs, axis=1)


def _banded_kernel(*refs, G, nprev, window, use_sink, qb, unroll):
    if use_sink:
        sink_ref, q_ref, k_ref, v_ref, o_ref = refs
    else:
        q_ref, k_ref, v_ref, o_ref = refs
    Cq = QBLK
    nk = (nprev + 1) * Cq
    hk = pl.program_id(1)

    def block(i, carry):
        n = pl.program_id(2) * qb + i
        r0 = pl.multiple_of(i * Cq, Cq)
        start = pl.multiple_of(jnp.maximum(n - nprev, 0) * Cq, Cq)
        kwin = k_ref[pl.ds(start, nk), :]
        vwin = v_ref[pl.ds(start, nk), :]
        lhs = _stack_heads(q_ref[pl.ds(r0, Cq), :])
        s_all = _dot_nt(lhs, kwin)
        t = n * Cq + lax.broadcasted_iota(jnp.int32, (Cq, nk), 0)
        rel = t - (start + lax.broadcasted_iota(jnp.int32, (Cq, nk), 1))
        valid = jnp.logical_and(rel >= 0, rel < window)
        outs = []
        for g in range(G):
            s = jnp.where(valid, s_all[g * Cq:(g + 1) * Cq], NEG_INF)
            m = jnp.max(s, axis=-1, keepdims=True)
            if use_sink:
                sink = sink_ref[hk * G + g]
                m = jnp.maximum(m, sink)
            p = jnp.exp((s - m).astype(BF16))
            acc = _dot(p, vwin)
            denom = pltpu.roll(acc, HEAD_DIM, 1)
            if use_sink:
                denom = denom + jnp.exp(sink - m)
            else:
                denom = jnp.maximum(denom, SOFTMAX_FLOOR)
            outs.append(acc / denom)
        o_ref[pl.ds(r0, Cq), :] = _unstack_heads_lo(jnp.concatenate(outs, axis=0), Cq).astype(BF16)
        return carry

    lax.fori_loop(0, qb, block, 0, unroll=unroll)


def banded_attention(q, k2, v2, sinks, B, T, *, G, nprev, window, unroll, qb=4):
    Hk = k2.shape[1]
    qb = min(qb, T // QBLK)
    use_sink = sinks is not None
    kern = functools.partial(_banded_kernel, G=G, nprev=nprev, window=window, use_sink=use_sink, qb=qb, unroll=unroll)
    in_specs = [pl.BlockSpec((None, qb * QBLK, G * HEAD_DIM), lambda b, h, n: (b, n, h)),
                pl.BlockSpec((None, None, T, LANES), lambda b, h, n: (b, h, 0, 0)),
                pl.BlockSpec((None, None, T, LANES), lambda b, h, n: (b, h, 0, 0))]
    args = [q, k2, v2]
    if use_sink:
        in_specs = [pl.BlockSpec(memory_space=pltpu.SMEM)] + in_specs
        args = [sinks.astype(F32)] + args
    return pl.pallas_call(
        kern,
        grid=(B, Hk, T // (qb * QBLK)),
        in_specs=in_specs,
        out_specs=pl.BlockSpec((None, qb * QBLK, G * HEAD_DIM), lambda b, h, n: (b, n, h)),
        out_shape=jax.ShapeDtypeStruct((B, T, Hk * G * HEAD_DIM), BF16),
        compiler_params=_params("parallel", "parallel", "arbitrary"),
        name="banded_attention",
    )(*args)


def _nsa_compress_kernel(x_ref, pe_ref, w1_ref, w2_ref, gain_ref, bd_ref, o_ref, *, norm):
    x = x_ref[...]
    gw = x.shape[1]
    u0 = _dot(x, w1_ref[:gw, :])
    u1 = _dot(x, w1_ref[gw:, :])
    n = u1.shape[0]
    pe8 = jnp.broadcast_to(pe_ref[...], (8, pe_ref.shape[1]))
    const = _dot(pe8, w1_ref[...])[:1]
    pre = u0 + pltpu.roll(u1, n - 1, 0) + const
    hid = (pre * jax.nn.sigmoid(pre)).astype(BF16)
    y = _dot(hid, w2_ref[...])
    if norm:
        sq = y * y
        hi = sq.astype(BF16)
        lo = (sq - hi.astype(F32)).astype(BF16)
        bd = bd_ref[...]
        ss = _dot(hi, bd) + _dot(lo, bd)
        y = y * lax.rsqrt(ss * (1.0 / HEAD_DIM) + NORM_EPS) * gain_ref[...]
    o_ref[...] = y.astype(BF16)


def nsa_compress(xg, pe, w1, w2, gain, *, norm):
    B, Hk, NG, GW = xg.shape
    bd = jnp.asarray(np.kron(np.eye(2), np.ones((HEAD_DIM, HEAD_DIM))), BF16)
    gain2 = jnp.concatenate([gain, gain]).reshape(1, LANES).astype(F32)
    w2d = jnp.concatenate([w2, w2], axis=1).astype(BF16)
    hid = w1.shape[1]
    return pl.pallas_call(
        functools.partial(_nsa_compress_kernel, norm=norm),
        grid=(B, Hk),
        in_specs=[pl.BlockSpec((None, None, NG, GW), lambda b, h: (b, h, 0, 0)),
                  pl.BlockSpec((1, 2 * GW), lambda b, h: (0, 0)),
                  pl.BlockSpec((2 * GW, hid), lambda b, h: (0, 0)),
                  pl.BlockSpec((hid, LANES), lambda b, h: (0, 0)),
                  pl.BlockSpec((1, LANES), lambda b, h: (0, 0)),
                  pl.BlockSpec((LANES, LANES), lambda b, h: (0, 0))],
        out_specs=pl.BlockSpec((None, None, NG, LANES), lambda b, h: (b, h, 0, 0)),
        out_shape=jax.ShapeDtypeStruct((B, Hk, NG, LANES), BF16),
        compiler_params=_params("parallel", "parallel"),
        name="nsa_compress",
    )(xg, pe.reshape(1, 2 * GW).astype(BF16), w1.astype(BF16), w2d, gain2, bd)


def _topk_rows_bias(v, n_sel):
    nslab = v.shape[0] // 8
    sub = lax.broadcasted_iota(jnp.int32, (8, v.shape[1]), 0).astype(F32)
    vals = [v[8 * k:8 * k + 8] for k in range(nslab)]
    ids = [sub + 8.0 * k for k in range(nslab)]
    for _ in range(n_sel):
        cand = list(zip(vals, ids))
        while len(cand) > 1:
            nxt = []
            for (va, ia), (vb, ib) in zip(cand[0::2], cand[1::2]):
                take_b = vb > va
                nxt.append((jnp.where(take_b, vb, va), jnp.where(take_b, ib, ia)))
            cand = nxt + ([cand[-1]] if len(cand) % 2 else [])
        bv, bi = cand[0]
        for shift in (4, 2, 1):
            rv, ri = pltpu.roll(bv, shift, 0), pltpu.roll(bi, shift, 0)
            take_r = jnp.logical_or(rv > bv, jnp.logical_and(rv == bv, ri < bi))
            bv, bi = jnp.where(take_r, rv, bv), jnp.where(take_r, ri, bi)
        vals = [jnp.where(ids[k] == bi, REMOVED, vals[k]) for k in range(nslab)]
    return jnp.concatenate([jnp.where(vk == REMOVED, 0.0, MASK_BIAS) for vk in vals], axis=0)


def _nsa_cmp_kernel(q_ref, kc_ref, vc_ref, ov_ref, oc_ref, mneg_ref, *, n_sel, qb):
    Cq = QBLK
    G = q_ref.shape[1] // HEAD_DIM
    ncp = kc_ref.shape[0]
    ov = ov_ref[...]
    for i in range(qb):
        n = pl.program_id(2) * qb + i
        rows = slice(i * Cq, (i + 1) * Cq)
        s_all = _dot_nt(_stack_heads(q_ref[rows, :]), kc_ref[...])
        t = n * Cq + lax.broadcasted_iota(jnp.int32, (Cq, ncp), 0)
        cmp_end = lax.broadcasted_iota(jnp.int32, (Cq, ncp), 1) * NSA_CMP_STRIDE + (NSA_CMP_LEN - 1)
        valid = cmp_end <= t
        outs = []
        psum = None
        for g in range(G):
            s = jnp.where(valid, s_all[g * Cq:(g + 1) * Cq], NEG_INF)
            p = jnp.where(valid, jnp.exp(s - jnp.max(s, axis=-1, keepdims=True)), 0.0)
            p = p / jnp.maximum(jnp.sum(p, axis=-1, keepdims=True), SOFTMAX_FLOOR)
            outs.append(_dot(p.astype(BF16), vc_ref[...]))
            psum = p if psum is None else psum + p
        oc_ref[rows, :] = _unstack_heads(jnp.concatenate(outs, axis=0), Cq).astype(BF16)

        hi = psum.astype(BF16)
        lo = (psum - hi.astype(F32)).astype(BF16)
        imp = (_dot(hi, ov) + _dot(lo, ov)).T
        nsb = imp.shape[0]
        j = lax.broadcasted_iota(jnp.int32, (nsb, Cq), 0)
        tt = n * Cq + lax.broadcasted_iota(jnp.int32, (nsb, Cq), 1)
        cur = tt // NSA_SEL_LEN
        forced = jnp.logical_or(j == 0, jnp.logical_or(j == cur, j == cur - 1))
        v = jnp.where(forced, NSA_FORCE_SCORE, imp)
        v = jnp.where(j * NSA_SEL_LEN <= tt, v, NEG_INF)
        mneg_ref[rows, :] = _topk_rows_bias(v, n_sel).T.astype(BF16)


def nsa_cmp_topk(q, kcmp, vcmp, ov, B, T, n_sel, qb=2):
    Hk = kcmp.shape[1]
    GW = q.shape[-1] // Hk
    ncp = kcmp.shape[2]
    cmp_spec = pl.BlockSpec((None, None, ncp, LANES), lambda b, h, n: (b, h, 0, 0))
    tq = qb * QBLK
    return pl.pallas_call(
        functools.partial(_nsa_cmp_kernel, n_sel=n_sel, qb=qb),
        grid=(B, Hk, T // tq),
        in_specs=[pl.BlockSpec((None, tq, GW), lambda b, h, n: (b, n, h)),
                  cmp_spec, cmp_spec,
                  pl.BlockSpec((ncp, LANES), lambda b, h, n: (0, 0))],
        out_specs=[pl.BlockSpec((None, tq, GW), lambda b, h, n: (b, n, h)),
                   pl.BlockSpec((None, None, tq, LANES), lambda b, h, n: (b, h, n, 0))],
        out_shape=[jax.ShapeDtypeStruct((B, T, q.shape[-1]), BF16),
                   jax.ShapeDtypeStruct((B, Hk, T, LANES), BF16)],
        compiler_params=_params("parallel", "parallel", "arbitrary"),
        name="nsa_cmp_topk",
    )(q, kcmp, vcmp, ov)


def _nsa_sel_kernel(q_ref, mneg_ref, k_ref, e_ref, v_ref, o_ref, lhs_ref, m_ref, acc_ref):
    Cq = q_ref.shape[0]
    n = pl.program_id(2)
    lhs_ref[...] = _stack_heads(q_ref[...], extra=mneg_ref[...])
    rows = lhs_ref.shape[0]
    m_ref[...] = jnp.full(m_ref.shape, NEG_INF, F32)
    acc_ref[...] = jnp.zeros_like(acc_ref)

    def chunk(off, kc, causal):
        rhs = jnp.concatenate([k_ref[pl.ds(off, kc), :], e_ref[pl.ds(off, kc), :]], axis=1)
        s = _dot_nt(lhs_ref[...], rhs)
        if causal:
            t = n * Cq + (lax.broadcasted_iota(jnp.int32, (rows, kc), 0) & (Cq - 1))
            s = jnp.where(off + lax.broadcasted_iota(jnp.int32, (rows, kc), 1) <= t, s, NEG_INF)
        m_prev = m_ref[...]
        m_new = jnp.maximum(m_prev, jnp.max(s, axis=-1, keepdims=True))
        p = jnp.exp((s - jnp.concatenate([m_new] * (kc // LANES), axis=1)).astype(BF16))
        acc_ref[...] = jnp.exp(m_prev - m_new) * acc_ref[...] + _dot(p, v_ref[pl.ds(off, kc), :])
        m_ref[...] = m_new

    per_chunk = SEL_CHUNK // Cq
    nmain = n // per_chunk

    def body(c, carry):
        chunk(pl.multiple_of(2 * c * SEL_CHUNK, SEL_CHUNK), SEL_CHUNK, False)
        chunk(pl.multiple_of((2 * c + 1) * SEL_CHUNK, SEL_CHUNK), SEL_CHUNK, False)
        return carry

    lax.fori_loop(0, nmain // 2, body, 0)

    @pl.when(nmain % 2 == 1)
    def _():
        chunk(pl.multiple_of((nmain - 1) * SEL_CHUNK, SEL_CHUNK), SEL_CHUNK, False)

    tail0 = pl.multiple_of(nmain * SEL_CHUNK, SEL_CHUNK)
    chunk(tail0, SEL_TAIL, True)
    for i in range(1, SEL_CHUNK // SEL_TAIL):
        @pl.when(n - nmain * per_chunk >= i * (SEL_TAIL // Cq))
        def _():
            chunk(tail0 + i * SEL_TAIL, SEL_TAIL, True)

    acc = acc_ref[...]
    o = acc / jnp.maximum(pltpu.roll(acc, HEAD_DIM, 1), SOFTMAX_FLOOR)
    o_ref[...] = _unstack_heads_lo(o, Cq).astype(BF16)


def nsa_selected(q, mneg, k2, e, v2, B, T):
    Hk = k2.shape[1]
    GW = q.shape[-1] // Hk
    G = GW // HEAD_DIM
    kv_spec = pl.BlockSpec((None, None, T, LANES), lambda b, h, n: (b, h, 0, 0))
    tq = SEL_Q
    return pl.pallas_call(
        _nsa_sel_kernel,
        grid=(B, Hk, T // tq),
        in_specs=[pl.BlockSpec((None, tq, GW), lambda b, h, n: (b, n, h)),
                  pl.BlockSpec((None, None, tq, LANES), lambda b, h, n: (b, h, n, 0)),
                  kv_spec,
                  _resident((T, LANES)),
                  kv_spec],
        out_specs=pl.BlockSpec((None, tq, GW), lambda b, h, n: (b, n, h)),
        out_shape=jax.ShapeDtypeStruct((B, T, q.shape[-1]), BF16),
        scratch_shapes=[pltpu.VMEM((G * tq, 2 * LANES), BF16),
                        pltpu.VMEM((G * tq, LANES), F32),
                        pltpu.VMEM((G * tq, LANES), F32)],
        compiler_params=_params("parallel", "parallel", "arbitrary"),
        name="nsa_selected",
    )(q, mneg, k2, e, v2)


def _nsa_out_kernel(oc_ref, os_ref, ow_ref, gates_ref, e_ref, w_ref, x_ref, gate_ref, o_ref):
    D = oc_ref.shape[1]
    sg = jax.nn.sigmoid(gates_ref[...].astype(F32)).astype(BF16)
    gx = _dot(sg, e_ref[...])
    y = (gx[:, :D] * oc_ref[...].astype(F32) + gx[:, D:2 * D] * os_ref[...].astype(F32)
         + gx[:, 2 * D:] * ow_ref[...].astype(F32))
    o_ref[...] = x_ref[...] + gate_ref[...] * _dot(y.astype(BF16), w_ref[...])


def nsa_out(oc, os_, ow, proj, gate_col_tile, expand, w, x, mod, gate_idx, T, tm=512):
    BT, D = x.shape
    tm = min(tm, T)
    tpb = T // tm
    row = pl.BlockSpec((tm, D), lambda i: (i, 0))
    return pl.pallas_call(
        _nsa_out_kernel,
        grid=(BT // tm,),
        in_specs=[row, row, row,
                  pl.BlockSpec((tm, LANES), lambda i: (i, gate_col_tile)),
                  _resident((LANES, 3 * D)),
                  _resident((D, D)),
                  row,
                  pl.BlockSpec((None, None, 1, D), lambda i: (i // tpb, gate_idx, 0, 0))],
        out_specs=row,
        out_shape=jax.ShapeDtypeStruct((BT, D), F32),
        compiler_params=_params("parallel"),
        name="nsa_out",
    )(oc, os_, ow, proj, expand, w, x, mod)


def retention_mixer(x, mod, norm_g, w_in, w_out, B, T):
    H = RET_HEADS
    D = x.shape[1]
    dk = D // H
    head_perm = np.concatenate([np.arange(0, dk, 2), np.arange(1, dk, 2)])
    perm = np.concatenate([hd * dk + head_perm for hd in range(2 * H)] + [np.arange(2 * H * dk, w_in.shape[1])])
    proj = normmod_matmul(x, norm_g, mod, 0, w_in[:, perm].astype(BF16), T)
    cos, sin = retention_tables(T, dk)
    y = retention_core(proj.reshape(B, T, -1), cos, sin, B, T)
    return outproj_residual(y.reshape(B * T, -1), w_out.astype(BF16), x, mod, 2, T)


def swa_mixer(x, mod, norm_g, w_in, q_norm_g, k_norm_g, sinks, w_out, B, T):
    D = x.shape[1]
    Hk = SWA_KV_HEADS
    G = D // HEAD_DIM // Hk
    proj = normmod_matmul(x, norm_g, mod, 0, w_in.astype(BF16), T, tn=256)
    tables = rope_tables(T)
    nq = D // LANES
    nkv = Hk * HEAD_DIM // LANES
    plan = [(j, 0, True, Q_SCALE, ("flat", 0, j)) for j in range(nq)]
    plan += [(nq + j, 1, True, 1.0, ("heads", 1, 2 * j, False)) for j in range(nkv)]
    plan += [(nq + nkv + j, None, False, 1.0, ("heads", 2, 2 * j, True)) for j in range(nkv)]
    q, k2, v2 = head_prep(proj, jnp.stack([q_norm_g, k_norm_g]), tables, plan, nq, [Hk, Hk], B, T)
    o = banded_attention(q.reshape(B, T, D), k2, v2, sinks, B, T, G=G, nprev=SWA_WINDOW // QBLK, window=SWA_WINDOW,
                         unroll=False)
    return outproj_residual(o.reshape(B * T, D), w_out.astype(BF16), x, mod, 2, T)


def nsa_mixer(x, mod, norm_g, w_in, q_norm_g, k_norm_g, cmp_pe, cmp_w1, cmp_w2, w_out, B, T):
    D = x.shape[1]
    Hk = NSA_KV_HEADS
    H = D // HEAD_DIM
    G = H // Hk
    kvw = Hk * HEAD_DIM
    n_main = D + 6 * kvw
    n_pad = -(-w_in.shape[1] // LANES) * LANES
    w_pad = jnp.pad(w_in, ((0, 0), (0, n_pad - w_in.shape[1]))).astype(BF16)
    proj = normmod_matmul(x, norm_g, mod, 0, w_pad, T, tn=LANES * 3)
    tables = rope_tables(T)
    nq = D // LANES
    nkv = kvw // LANES
    col = lambda i: nq + i * nkv
    plan = [(j, 0, True, Q_SCALE, ("flat", 0, j)) for j in range(nq)]
    for j in range(nkv):
        plan += [(col(2) + j, 1, True, 1.0, ("heads", 1, 2 * j, False)),
                 (col(3) + j, None, False, 1.0, ("heads", 2, 2 * j, True)),
                 (col(4) + j, 2, True, 1.0, ("heads", 3, 2 * j, False)),
                 (col(5) + j, None, False, 1.0, ("heads", 4, 2 * j, True))]
    gains = jnp.stack([q_norm_g, k_norm_g[1], k_norm_g[2]])
    q, ks2, vs2, kw2, vw2 = head_prep(proj, gains, tables, plan, nq, [Hk] * 4, B, T)
    q = q.reshape(B, T, D)

    S = NSA_CMP_STRIDE
    p3 = proj.reshape(B, T, n_pad)

    def groups(c0):
        a = p3[:, :, c0:c0 + kvw].reshape(B, T // S, S, Hk, HEAD_DIM)
        return a.transpose(0, 3, 1, 2, 4).reshape(B, Hk, T // S, S * HEAD_DIM)

    kcmp = nsa_compress(groups(D), cmp_pe[0], cmp_w1[0], cmp_w2[0], k_norm_g[0], norm=True)
    vcmp = nsa_compress(groups(D + kvw), cmp_pe[1], cmp_w1[1], cmp_w2[1], k_norm_g[0], norm=False)

    ncp = T // S
    ns = T // NSA_SEL_LEN
    cs = np.arange(ncp)[:, None] * S
    js = np.arange(LANES)[None, :]
    overlap = ((cs < js * NSA_SEL_LEN + NSA_SEL_LEN) & (cs + NSA_CMP_LEN > js * NSA_SEL_LEN)
               & (js < ns) & (np.arange(ncp)[:, None] < ncp - 1))
    ov = jnp.asarray(overlap, BF16)
    oc, mneg = nsa_cmp_topk(q, kcmp, vcmp, ov, B, T, min(NSA_N_SEL, ns))

    blk_onehot = jnp.asarray(np.arange(T)[:, None] // NSA_SEL_LEN == js, BF16)
    os_ = nsa_selected(q, mneg, ks2, blk_onehot, vs2, B, T)
    ow = banded_attention(q, kw2, vw2, None, B, T, G=G, nprev=NSA_WINDOW // QBLK, window=NSA_WINDOW, unroll=True)

    rows = np.arange(LANES)[:, None]
    cols = np.arange(3 * D)[None, :]
    expand = jnp.asarray(rows == 3 * ((cols % D) // HEAD_DIM) + cols // D, BF16)
    flat = lambda a: a.reshape(B * T, D)
    return nsa_out(flat(oc), flat(os_), flat(ow), proj, n_main // LANES, expand, w_out.astype(BF16), x, mod, 2, T)


def kernel(x, c, l0_mod_w, l0_mod_b, l0_norm1_g, l0_ret_w_in, l0_ret_w_out, l0_norm2_g, l0_ffn_w_in, l0_ffn_w_out, l1_mod_w, l1_mod_b, l1_norm1_g, l1_swa_w_in, l1_swa_q_norm_g, l1_swa_k_norm_g, l1_swa_sinks, l1_swa_w_out, l1_norm2_g, l1_ffn_w_in, l1_ffn_w_out, l2_mod_w, l2_mod_b, l2_norm1_g, l2_nsa_w_in, l2_nsa_q_norm_g, l2_nsa_k_norm_g, l2_nsa_cmp_pe, l2_nsa_cmp_w1, l2_nsa_cmp_w2, l2_nsa_w_out, l2_norm2_g, l2_ffn_w_in, l2_ffn_w_out, l3_mod_w, l3_mod_b, l3_norm1_g, l3_ret_w_in, l3_ret_w_out, l3_norm2_g, l3_ffn_w_in, l3_ffn_w_out):
    B, T, D = x.shape
    layers = [
        (l0_mod_w, l0_mod_b, l0_norm1_g, retention_mixer, (l0_ret_w_in, l0_ret_w_out), l0_norm2_g, l0_ffn_w_in, l0_ffn_w_out),
        (l1_mod_w, l1_mod_b, l1_norm1_g, swa_mixer,
         (l1_swa_w_in, l1_swa_q_norm_g, l1_swa_k_norm_g, l1_swa_sinks, l1_swa_w_out), l1_norm2_g, l1_ffn_w_in, l1_ffn_w_out),
        (l2_mod_w, l2_mod_b, l2_norm1_g, nsa_mixer,
         (l2_nsa_w_in, l2_nsa_q_norm_g, l2_nsa_k_norm_g, l2_nsa_cmp_pe, l2_nsa_cmp_w1, l2_nsa_cmp_w2, l2_nsa_w_out),
         l2_norm2_g, l2_ffn_w_in, l2_ffn_w_out),
        (l3_mod_w, l3_mod_b, l3_norm1_g, retention_mixer, (l3_ret_w_in, l3_ret_w_out), l3_norm2_g, l3_ffn_w_in, l3_ffn_w_out),
    ]
    h = x.reshape(B * T, D)
    for mod_w, mod_b, norm1_g, mixer, mixer_params, norm2_g, ffn_w_in, ffn_w_out in layers:
        mod = adaln_mod(c, mod_w, mod_b)
        h = mixer(h, mod, norm1_g, *mixer_params, B, T)
        h = ffn_block(h, norm2_g, mod, ffn_w_in.astype(BF16), ffn_w_out.astype(BF16), T)
    return h.reshape(B, T, D)
```

```python
import functools

import jax
import jax.numpy as jnp
import numpy as np
from jax import lax
from jax.experimental import pallas as pl
from jax.experimental.pallas import tpu as pltpu

F32 = jnp.float32
BF16 = jnp.bfloat16

NORM_EPS = 1e-6
NEG_INF = -1e30
SOFTMAX_FLOOR = 1e-30
REMOVED = -3e38
MASK_BIAS = -(2.0 ** 30)

V7X_VMEM_LIMIT_BYTES = 56 * 2 ** 20
LANES = 128
QBLK = 128

ROPE_THETA = 500000.0
RET_ROT_BASE = 10000.0
RET_HEADS = 4
RET_CHUNK = 256
SWA_KV_HEADS = 2
SWA_WINDOW = 128
NSA_KV_HEADS = 4
NSA_CMP_LEN = 32
NSA_CMP_STRIDE = 16
NSA_SEL_LEN = 64
NSA_N_SEL = 16
NSA_WINDOW = 512
NSA_FORCE_SCORE = 1e4
HEAD_DIM = 64
SEL_CHUNK = 1024
SEL_TAIL = 512
SEL_Q = 256
Q_SCALE = HEAD_DIM ** -0.5

NT_DIMS = (((1,), (1,)), ((), ()))


def _params(*sem):
    return pltpu.CompilerParams(dimension_semantics=sem, vmem_limit_bytes=V7X_VMEM_LIMIT_BYTES)


def _dot(a, b):
    return jnp.dot(a, b, preferred_element_type=F32)


def _dot_nt(a, b):
    return lax.dot_general(a, b, NT_DIMS, preferred_element_type=F32)


def _mod_kernel(c_ref, w_ref, b_ref, o_ref):
    c = c_ref[...]
    s = c * jax.nn.sigmoid(c)
    nb = s.shape[0]
    s8 = jnp.concatenate([s, jnp.zeros((8 - nb, s.shape[1]), F32)], axis=0)
    r = jnp.dot(s8, w_ref[...], preferred_element_type=F32, precision=lax.Precision.HIGHEST)
    o_ref[...] = r[:nb] + b_ref[...]


def adaln_mod(c, mod_w, mod_b):
    B, D = c.shape
    N = mod_w.shape[1]
    tn = N // 4
    out = pl.pallas_call(
        _mod_kernel,
        grid=(N // tn,),
        in_specs=[pl.BlockSpec((B, D), lambda j: (0, 0)),
                  pl.BlockSpec((D, tn), lambda j: (0, j)),
                  pl.BlockSpec((1, tn), lambda j: (0, j))],
        out_specs=pl.BlockSpec((B, tn), lambda j: (0, j)),
        out_shape=jax.ShapeDtypeStruct((B, N), F32),
        compiler_params=_params("arbitrary"),
        name="adaln_mod",
    )(c, mod_w, mod_b.reshape(1, N))
    return out.reshape(B, 6, 1, D)


def _normmod(x, g, scale, shift):
    ms = jnp.mean(x * x, axis=-1, keepdims=True)
    y = x * lax.rsqrt(ms + NORM_EPS) * g
    return y * (1.0 + scale) + shift


def _resident(shape):
    return pl.BlockSpec(shape, lambda *_: (0,) * len(shape), pipeline_mode=pl.Buffered(1))


def _normmod_matmul_kernel(x_ref, g_ref, scale_ref, shift_ref, w_ref, *rest, tn, rot_scales):
    h = _normmod(x_ref[...], g_ref[...], scale_ref[...], shift_ref[...]).astype(BF16)
    o_ref = rest[-1]
    if rot_scales:
        cos, sin = rest[0][...], rest[1][...]
        hw = 2 * cos.shape[1]
    for c in range(w_ref.shape[1] // tn):
        cols = slice(c * tn, (c + 1) * tn)
        acc = _dot(h, w_ref[:, cols])
        if c * tn < len(rot_scales) * (hw if rot_scales else 0):
            parts = []
            for hd in range(tn // hw):
                x1 = acc[:, hd * hw:hd * hw + hw // 2]
                x2 = acc[:, hd * hw + hw // 2:(hd + 1) * hw]
                sc = rot_scales[c * tn // hw + hd]
                parts += [(x1 * cos - x2 * sin) * sc, (x2 * cos + x1 * sin) * sc]
            acc = jnp.concatenate(parts, axis=1)
        o_ref[:, cols] = acc.astype(o_ref.dtype)


def normmod_matmul(x, g, mod, shift_idx, w, T, tm=512, tn=512, rot=None):
    BT, D = x.shape
    N = w.shape[1]
    tm = min(tm, T)
    tn = tn if N % tn == 0 else LANES
    tpb = T // tm
    in_specs = [pl.BlockSpec((tm, D), lambda i: (i, 0)),
                _resident((1, D)),
                pl.BlockSpec((None, None, 1, D), lambda i: (i // tpb, shift_idx + 1, 0, 0)),
                pl.BlockSpec((None, None, 1, D), lambda i: (i // tpb, shift_idx, 0, 0)),
                _resident((D, N))]
    args = [x, g.reshape(1, D), mod, mod, w]
    rot_scales = ()
    if rot is not None:
        cos, sin, rot_scales = rot
        in_specs += [pl.BlockSpec((tm, cos.shape[1]), lambda i: (i % tpb, 0))] * 2
        args += [cos, sin]
    return pl.pallas_call(
        functools.partial(_normmod_matmul_kernel, tn=tn, rot_scales=tuple(rot_scales)),
        grid=(BT // tm,),
        in_specs=in_specs,
        out_specs=pl.BlockSpec((tm, N), lambda i: (i, 0)),
        out_shape=jax.ShapeDtypeStruct((BT, N), BF16),
        compiler_params=_params("parallel"),
        name="normmod_matmul",
    )(*args)


def _outproj_kernel(y_ref, w_ref, x_ref, gate_ref, o_ref):
    o_ref[...] = x_ref[...] + gate_ref[...] * _dot(y_ref[...], w_ref[...])


def outproj_residual(y, w, x, mod, gate_idx, T, tm=512):
    BT, K = y.shape
    D = w.shape[1]
    tm = min(tm, T)
    tpb = T // tm
    return pl.pallas_call(
        _outproj_kernel,
        grid=(BT // tm,),
        in_specs=[pl.BlockSpec((tm, K), lambda i: (i, 0)),
                  _resident((K, D)),
                  pl.BlockSpec((tm, D), lambda i: (i, 0)),
                  pl.BlockSpec((None, None, 1, D), lambda i: (i // tpb, gate_idx, 0, 0))],
        out_specs=pl.BlockSpec((tm, D), lambda i: (i, 0)),
        out_shape=jax.ShapeDtypeStruct((BT, D), F32),
        compiler_params=_params("parallel"),
        name="outproj_residual",
    )(y, w, x, mod)


def _ffn_kernel(x_ref, g_ref, scale_ref, shift_ref, gate_ref, wi_ref, wo_ref, o_ref, *, tc):
    x = x_ref[...]
    h = _normmod(x, g_ref[...], scale_ref[...], shift_ref[...]).astype(BF16)
    F = wo_ref.shape[0]
    acc = None
    for c in range(F // tc):
        a = _dot(h, wi_ref[:, c * tc:(c + 1) * tc])
        b = _dot(h, wi_ref[:, F + c * tc:F + (c + 1) * tc])
        u = (a * jax.nn.sigmoid(a) * b).astype(BF16)
        part = _dot(u, wo_ref[c * tc:(c + 1) * tc, :])
        acc = part if acc is None else acc + part
    o_ref[...] = x + gate_ref[...] * acc


def ffn_block(x, g, mod, w_in, w_out, T, tm=512, tc=256):
    BT, D = x.shape
    F = w_out.shape[0]
    tm = min(tm, T)
    tpb = T // tm
    modspec = lambda idx: pl.BlockSpec((None, None, 1, D), lambda i: (i // tpb, idx, 0, 0))
    return pl.pallas_call(
        functools.partial(_ffn_kernel, tc=tc),
        grid=(BT // tm,),
        in_specs=[pl.BlockSpec((tm, D), lambda i: (i, 0)),
                  _resident((1, D)),
                  modspec(4), modspec(3), modspec(5),
                  _resident((D, 2 * F)),
                  _resident((F, D))],
        out_specs=pl.BlockSpec((tm, D), lambda i: (i, 0)),
        out_shape=jax.ShapeDtypeStruct((BT, D), F32),
        compiler_params=_params("parallel"),
        name="ffn_block",
    )(x, g.reshape(1, D), mod, mod, mod, w_in, w_out)


def _ret_core_kernel(q_ref, k_ref, v_ref, g_ref, o_ref, state_ref, *, chunks):
    C = RET_CHUNK

    @pl.when(pl.program_id(2) == 0)
    def _():
        state_ref[...] = jnp.zeros_like(state_ref)

    head = pl.program_id(1)
    log_gammas = [float(np.log(1.0 - 2.0 ** (-5.0 - i))) for i in range(RET_HEADS)]
    lg = jnp.float32(log_gammas[-1])
    for i in range(RET_HEADS - 1):
        lg = jnp.where(head == i, log_gammas[i], lg)
    rel = (lax.broadcasted_iota(jnp.int32, (C, C), 0) - lax.broadcasted_iota(jnp.int32, (C, C), 1)).astype(F32)
    intra = jnp.where(rel >= 0, jnp.exp(lg * jnp.maximum(rel, 0.0)), 0.0)
    idx = lax.broadcasted_iota(jnp.int32, (C, 1), 0).astype(F32)
    q_decay = jnp.exp(lg * (idx + 1.0))
    k_decay = jnp.exp(lg * (C - 1.0 - idx))
    state_decay = jnp.exp(lg * float(C))

    for c in range(chunks):
        sl = slice(c * C, (c + 1) * C)
        qb = q_ref[sl, :]
        kb = k_ref[sl, :]
        v = v_ref[sl, :]
        scores = _dot_nt(qb, kb) * intra
        inner = _dot(scores.astype(BF16), v)
        state = state_ref[...]
        cross = _dot(qb, state.astype(BF16)) * q_decay
        kd_t = (kb.astype(F32) * k_decay).T.astype(BF16)
        state_ref[...] = state * state_decay + _dot(kd_t, v)
        out = inner + cross
        ms = jnp.mean(out * out, axis=-1, keepdims=True)
        y = out * lax.rsqrt(ms + NORM_EPS)
        g = g_ref[sl, :]
        o_ref[sl, :] = y.astype(BF16) * (g * jax.nn.sigmoid(g))


def retention_core(proj, B, T, tb=1024):
    H = RET_HEADS
    DK = proj.shape[-1] // (6 * H)
    DV = 2 * DK
    tb = min(tb, T)
    kern = functools.partial(_ret_core_kernel, chunks=tb // RET_CHUNK)
    return pl.pallas_call(
        kern,
        grid=(B, H, T // tb),
        in_specs=[pl.BlockSpec((None, tb, DK), lambda b, h, t: (b, t, h)),
                  pl.BlockSpec((None, tb, DK), lambda b, h, t: (b, t, H + h)),
                  pl.BlockSpec((None, tb, DV), lambda b, h, t: (b, t, H + h)),
                  pl.BlockSpec((None, tb, DV), lambda b, h, t: (b, t, 2 * H + h))],
        out_specs=pl.BlockSpec((None, tb, DV), lambda b, h, t: (b, t, h)),
        out_shape=jax.ShapeDtypeStruct((B, T, H * DV), BF16),
        scratch_shapes=[pltpu.VMEM((DK, DV), F32)],
        compiler_params=_params("parallel", "parallel", "arbitrary"),
        name="retention_core",
    )(proj, proj, proj, proj)


def _head_prep_kernel(x_ref, gains_ref, c_ref, s1_ref, s2_ref, bd_ref, *out_refs, plan):
    c, s1, s2 = c_ref[...], s1_ref[...], s2_ref[...]
    bd = bd_ref[...]
    first_half = lax.broadcasted_iota(jnp.int32, c.shape, 1) < HEAD_DIM
    for src, gain_row, rope, scale, dst in plan:
        x = x_ref[:, src * LANES:(src + 1) * LANES].astype(F32)
        if gain_row is not None:
            ss = _dot((x * x).astype(BF16), bd)
            x = x * lax.rsqrt(ss * (1.0 / HEAD_DIM) + NORM_EPS) * gains_ref[gain_row:gain_row + 1, :]
        if rope:
            x = x * c + pltpu.roll(x, LANES - 8, 1) * s1 + pltpu.roll(x, 8, 1) * s2
        if scale != 1.0:
            x = x * scale
        if dst[0] == "flat":
            out_refs[dst[1]][:, dst[2] * LANES:(dst[2] + 1) * LANES] = x.astype(BF16)
        else:
            _, out, head0, ones = dst
            swapped = pltpu.roll(x, HEAD_DIM, 1)
            a = jnp.where(first_half, x, 1.0 if ones else swapped)
            b = jnp.where(first_half, swapped, 1.0 if ones else x)
            out_refs[out][head0] = a.astype(BF16)
            out_refs[out][head0 + 1] = b.astype(BF16)


def head_prep(proj, gains, tables, plan, n_flat_tiles, head_outs, B, T, tm=512):
    BT, N = proj.shape
    tm = min(tm, T)
    tpb = T // tm
    bd = jnp.asarray(np.kron(np.eye(2), np.ones((HEAD_DIM, HEAD_DIM))), BF16)
    gains2 = jnp.concatenate([gains, gains], axis=1).astype(F32)
    tab = pl.BlockSpec((tm, LANES), lambda i: (i % tpb, 0))
    out_specs = [pl.BlockSpec((tm, n_flat_tiles * LANES), lambda i: (i, 0))]
    out_shape = [jax.ShapeDtypeStruct((BT, n_flat_tiles * LANES), BF16)]
    for nh in head_outs:
        out_specs.append(pl.BlockSpec((None, nh, tm, LANES), lambda i: (i // tpb, 0, i % tpb, 0)))
        out_shape.append(jax.ShapeDtypeStruct((B, nh, T, LANES), BF16))
    return pl.pallas_call(
        functools.partial(_head_prep_kernel, plan=tuple(plan)),
        grid=(BT // tm,),
        in_specs=[pl.BlockSpec((tm, N), lambda i: (i, 0)),
                  _resident(gains2.shape),
                  tab, tab, tab,
                  _resident((LANES, LANES))],
        out_specs=out_specs,
        out_shape=out_shape,
        compiler_params=_params("parallel"),
        name="head_prep",
    )(proj, gains2, *tables, bd)


def rope_tables(T):
    half = HEAD_DIM // 4 // 2
    inv = ROPE_THETA ** (-jnp.arange(half, dtype=F32) / half)
    lane = np.arange(LANES) % HEAD_DIM
    lo, hi = lane < half, (lane >= half) & (lane < 2 * half)
    inv_lane = jnp.where(lo | hi, inv[lane % half], 0.0)
    ang = jnp.arange(T).astype(F32)[:, None] * inv_lane[None, :]
    sin = jnp.sin(ang)
    return jnp.cos(ang), jnp.where(lo, -sin, 0.0), jnp.where(hi, sin, 0.0)


def retention_tables(T, dk):
    inv = 1.0 / (RET_ROT_BASE ** jnp.linspace(0.0, 1.0, dk // 2, dtype=F32))
    ang = jnp.arange(T).astype(F32)[:, None] * inv[None, :]
    return jnp.cos(ang), jnp.sin(ang)


def _stack_heads(q, extra=None):
    first_half = lax.broadcasted_iota(jnp.int32, (q.shape[0], LANES), 1) < HEAD_DIM
    zero = jnp.zeros((q.shape[0], LANES), q.dtype)
    rows = []
    for p in range(q.shape[1] // LANES):
        qp = q[:, p * LANES:(p + 1) * LANES]
        for part in (jnp.where(first_half, qp, zero), jnp.where(first_half, zero, qp)):
            rows.append(part if extra is None else jnp.concatenate([part, extra], axis=1))
    return jnp.concatenate(rows, axis=0)


def _unstack_heads(o, cq):
    first_half = lax.broadcasted_iota(jnp.int32, (cq, LANES), 1) < HEAD_DIM
    G = o.shape[0] // cq
    pairs = [jnp.where(first_half, o[(2 * p) * cq:(2 * p + 1) * cq], o[(2 * p + 1) * cq:(2 * p + 2) * cq])
             for p in range(G // 2)]
    return jnp.concatenate(pairs, axis=1)


def _unstack_heads_lo(o, cq):
    first_half = lax.broadcasted_iota(jnp.int32, (cq, LANES), 1) < HEAD_DIM
    G = o.shape[0] // cq
    pairs = [jnp.where(first_half, o[(2 * p) * cq:(2 * p + 1) * cq],
                       pltpu.roll(o[(2 * p + 1) * cq:(2 * p + 2) * cq], HEAD_DIM, 1))
             for p in range(G // 2)]
    return jnp.concatenate(pairs, axis=1)


def _banded_kernel(*refs, G, nprev, window, use_sink, qb, unroll):
    if use_sink:
        sink_ref, q_ref, k_ref, v_ref, o_ref = refs
    else:
        q_ref, k_ref, v_ref, o_ref = refs
    Cq = QBLK
    nk = (nprev + 1) * Cq
    hk = pl.program_id(1)

    def block(i, carry):
        n = pl.program_id(2) * qb + i
        r0 = pl.multiple_of(i * Cq, Cq)
        start = pl.multiple_of(jnp.maximum(n - nprev, 0) * Cq, Cq)
        kwin = k_ref[pl.ds(start, nk), :]
        vwin = v_ref[pl.ds(start, nk), :]
        lhs = _stack_heads(q_ref[pl.ds(r0, Cq), :])
        s_all = _dot_nt(lhs, kwin)
        t = n * Cq + lax.broadcasted_iota(jnp.int32, (Cq, nk), 0)
        rel = t - (start + lax.broadcasted_iota(jnp.int32, (Cq, nk), 1))
        valid = jnp.logical_and(rel >= 0, rel < window)
        outs = []
        for g in range(G):
            s = jnp.where(valid, s_all[g * Cq:(g + 1) * Cq], NEG_INF)
            m = jnp.max(s, axis=-1, keepdims=True)
            if use_sink:
                sink = sink_ref[hk * G + g]
                m = jnp.maximum(m, sink)
            p = jnp.exp((s - m).astype(BF16))
            acc = _dot(p, vwin)
            denom = pltpu.roll(acc, HEAD_DIM, 1)
            if use_sink:
                denom = denom + jnp.exp(sink - m)
            else:
                denom = jnp.maximum(denom, SOFTMAX_FLOOR)
            outs.append(acc / denom)
        o_ref[pl.ds(r0, Cq), :] = _unstack_heads_lo(jnp.concatenate(outs, axis=0), Cq).astype(BF16)
        return carry

    lax.fori_loop(0, qb, block, 0, unroll=unroll)


def banded_attention(q, k2, v2, sinks, B, T, *, G, nprev, window, unroll, qb=4):
    Hk = k2.shape[1]
    qb = min(qb, T // QBLK)
    use_sink = sinks is not None
    kern = functools.partial(_banded_kernel, G=G, nprev=nprev, window=window, use_sink=use_sink, qb=qb, unroll=unroll)
    in_specs = [pl.BlockSpec((None, qb * QBLK, G * HEAD_DIM), lambda b, h, n: (b, n, h)),
                pl.BlockSpec((None, None, T, LANES), lambda b, h, n: (b, h, 0, 0)),
                pl.BlockSpec((None, None, T, LANES), lambda b, h, n: (b, h, 0, 0))]
    args = [q, k2, v2]
    if use_sink:
        in_specs = [pl.BlockSpec(memory_space=pltpu.SMEM)] + in_specs
        args = [sinks.astype(F32)] + args
    return pl.pallas_call(
        kern,
        grid=(B, Hk, T // (qb * QBLK)),
        in_specs=in_specs,
        out_specs=pl.BlockSpec((None, qb * QBLK, G * HEAD_DIM), lambda b, h, n: (b, n, h)),
        out_shape=jax.ShapeDtypeStruct((B, T, Hk * G * HEAD_DIM), BF16),
        compiler_params=_params("parallel", "parallel", "arbitrary"),
        name="banded_attention",
    )(*args)


def _nsa_compress_kernel(x_ref, pe_ref, w1_ref, w2_ref, gain_ref, bd_ref, o_ref, *, norm):
    x = x_ref[...]
    gw = x.shape[1]
    u0 = _dot(x, w1_ref[:gw, :])
    u1 = _dot(x, w1_ref[gw:, :])
    n = u1.shape[0]
    pe8 = jnp.broadcast_to(pe_ref[...], (8, pe_ref.shape[1]))
    const = _dot(pe8, w1_ref[...])[:1]
    pre = u0 + pltpu.roll(u1, n - 1, 0) + const
    hid = (pre * jax.nn.sigmoid(pre)).astype(BF16)
    y = _dot(hid, w2_ref[...])
    if norm:
        sq = y * y
        hi = sq.astype(BF16)
        lo = (sq - hi.astype(F32)).astype(BF16)
        bd = bd_ref[...]
        ss = _dot(hi, bd) + _dot(lo, bd)
        y = y * lax.rsqrt(ss * (1.0 / HEAD_DIM) + NORM_EPS) * gain_ref[...]
    o_ref[...] = y.astype(BF16)


def nsa_compress(xg, pe, w1, w2, gain, *, norm):
    B, Hk, NG, GW = xg.shape
    bd = jnp.asarray(np.kron(np.eye(2), np.ones((HEAD_DIM, HEAD_DIM))), BF16)
    gain2 = jnp.concatenate([gain, gain]).reshape(1, LANES).astype(F32)
    w2d = jnp.concatenate([w2, w2], axis=1).astype(BF16)
    hid = w1.shape[1]
    return pl.pallas_call(
        functools.partial(_nsa_compress_kernel, norm=norm),
        grid=(B, Hk),
        in_specs=[pl.BlockSpec((None, None, NG, GW), lambda b, h: (b, h, 0, 0)),
                  pl.BlockSpec((1, 2 * GW), lambda b, h: (0, 0)),
                  pl.BlockSpec((2 * GW, hid), lambda b, h: (0, 0)),
                  pl.BlockSpec((hid, LANES), lambda b, h: (0, 0)),
                  pl.BlockSpec((1, LANES), lambda b, h: (0, 0)),
                  pl.BlockSpec((LANES, LANES), lambda b, h: (0, 0))],
        out_specs=pl.BlockSpec((None, None, NG, LANES), lambda b, h: (b, h, 0, 0)),
        out_shape=jax.ShapeDtypeStruct((B, Hk, NG, LANES), BF16),
        compiler_params=_params("parallel", "parallel"),
        name="nsa_compress",
    )(xg, pe.reshape(1, 2 * GW).astype(BF16), w1.astype(BF16), w2d, gain2, bd)


def _topk_rows_bias(v, n_sel):
    nslab = v.shape[0] // 8
    sub = lax.broadcasted_iota(jnp.int32, (8, v.shape[1]), 0).astype(F32)
    vals = [v[8 * k:8 * k + 8] for k in range(nslab)]
    ids = [sub + 8.0 * k for k in range(nslab)]
    bias = [jnp.full(sub.shape, MASK_BIAS, F32)] * nslab
    for _ in range(n_sel):
        cand = list(zip(vals, ids))
        while len(cand) > 1:
            nxt = []
            for (va, ia), (vb, ib) in zip(cand[0::2], cand[1::2]):
                take_b = vb > va
                nxt.append((jnp.where(take_b, vb, va), jnp.where(take_b, ib, ia)))
            cand = nxt + ([cand[-1]] if len(cand) % 2 else [])
        bv, bi = cand[0]
        for shift in (4, 2, 1):
            rv, ri = pltpu.roll(bv, shift, 0), pltpu.roll(bi, shift, 0)
            take_r = jnp.logical_or(rv > bv, jnp.logical_and(rv == bv, ri < bi))
            bv, bi = jnp.where(take_r, rv, bv), jnp.where(take_r, ri, bi)
        first = [ids[k] == bi for k in range(nslab)]
        vals = [jnp.where(first[k], REMOVED, vals[k]) for k in range(nslab)]
        bias = [jnp.where(first[k], 0.0, bias[k]) for k in range(nslab)]
    return jnp.concatenate(bias, axis=0)


def _nsa_cmp_kernel(q_ref, kc_ref, vc_ref, ov_ref, oc_ref, mneg_ref, imp_ref, *, n_sel, qb):
    Cq = QBLK
    G = q_ref.shape[1] // HEAD_DIM
    ncp = kc_ref.shape[0]
    step = pl.program_id(2)

    def attend(width):
        for i in range(qb):
            n = step * qb + i
            rows = slice(i * Cq, (i + 1) * Cq)
            s_all = _dot_nt(_stack_heads(q_ref[rows, :]), kc_ref[:width, :])
            t = n * Cq + lax.broadcasted_iota(jnp.int32, (Cq, width), 0)
            cmp_end = lax.broadcasted_iota(jnp.int32, (Cq, width), 1) * NSA_CMP_STRIDE + (NSA_CMP_LEN - 1)
            valid = cmp_end <= t
            outs = []
            psum = None
            for g in range(G):
                s = jnp.where(valid, s_all[g * Cq:(g + 1) * Cq], NEG_INF)
                p = jnp.exp(s - jnp.maximum(jnp.max(s, axis=-1, keepdims=True), 0.1 * NEG_INF))
                p = p / jnp.maximum(jnp.sum(p, axis=-1, keepdims=True), SOFTMAX_FLOOR)
                outs.append(_dot(p.astype(BF16), vc_ref[:width, :]))
                psum = p if psum is None else psum + p
            oc_ref[rows, :] = _unstack_heads(jnp.concatenate(outs, axis=0), Cq).astype(BF16)
            hi = psum.astype(BF16)
            lo = (psum - hi.astype(F32)).astype(BF16)
            ov = ov_ref[:width, :]
            imp_ref[i] = _dot(hi, ov) + _dot(lo, ov)

    n_valid = ((step + 1) * qb * Cq - NSA_CMP_LEN) // NSA_CMP_STRIDE + 1
    groups = (n_valid + LANES - 1) // LANES
    widths = list(range(LANES, ncp, LANES)) + [ncp]
    for gi, width in enumerate(widths):
        last = gi == len(widths) - 1
        pl.when(groups >= gi + 1 if last else groups == gi + 1)(functools.partial(attend, width))

    for i in range(qb):
        n = step * qb + i
        rows = slice(i * Cq, (i + 1) * Cq)
        imp = imp_ref[i].T
        nsb = imp.shape[0]
        j = lax.broadcasted_iota(jnp.int32, (nsb, Cq), 0)
        tt = n * Cq + lax.broadcasted_iota(jnp.int32, (nsb, Cq), 1)
        cur = tt // NSA_SEL_LEN
        forced = jnp.logical_or(j == 0, jnp.logical_or(j == cur, j == cur - 1))
        v = jnp.where(forced, NSA_FORCE_SCORE, imp)
        v = jnp.where(j * NSA_SEL_LEN <= tt, v, NEG_INF)
        mneg_ref[rows, :] = _topk_rows_bias(v, n_sel).T.astype(BF16)


def nsa_cmp_topk(q, kcmp, vcmp, ov, B, T, n_sel, qb=2):
    Hk = kcmp.shape[1]
    GW = q.shape[-1] // Hk
    ncp = kcmp.shape[2]
    cmp_spec = pl.BlockSpec((None, None, ncp, LANES), lambda b, h, n: (b, h, 0, 0))
    tq = qb * QBLK
    return pl.pallas_call(
        functools.partial(_nsa_cmp_kernel, n_sel=n_sel, qb=qb),
        grid=(B, Hk, T // tq),
        in_specs=[pl.BlockSpec((None, tq, GW), lambda b, h, n: (b, n, h)),
                  cmp_spec, cmp_spec,
                  pl.BlockSpec((ncp, LANES), lambda b, h, n: (0, 0))],
        out_specs=[pl.BlockSpec((None, tq, GW), lambda b, h, n: (b, n, h)),
                   pl.BlockSpec((None, None, tq, LANES), lambda b, h, n: (b, h, n, 0))],
        out_shape=[jax.ShapeDtypeStruct((B, T, q.shape[-1]), BF16),
                   jax.ShapeDtypeStruct((B, Hk, T, LANES), BF16)],
        scratch_shapes=[pltpu.VMEM((qb, QBLK, LANES), F32)],
        compiler_params=_params("parallel", "parallel", "arbitrary"),
        name="nsa_cmp_topk",
    )(q, kcmp, vcmp, ov)


def _nsa_sel_kernel(q_ref, mneg_ref, k_ref, e_ref, v_ref, o_ref, lhs_ref, m_ref, acc_ref):
    Cq = q_ref.shape[0]
    n = pl.program_id(2)
    lhs_ref[...] = _stack_heads(q_ref[...], extra=mneg_ref[...])
    rows = lhs_ref.shape[0]
    m_ref[...] = jnp.full(m_ref.shape, NEG_INF, F32)
    acc_ref[...] = jnp.zeros_like(acc_ref)

    def chunk(off, kc, causal):
        rhs = jnp.concatenate([k_ref[pl.ds(off, kc), :], e_ref[pl.ds(off, kc), :]], axis=1)
        s = _dot_nt(lhs_ref[...], rhs)
        if causal:
            t = n * Cq + (lax.broadcasted_iota(jnp.int32, (rows, kc), 0) & (Cq - 1))
            s = jnp.where(off + lax.broadcasted_iota(jnp.int32, (rows, kc), 1) <= t, s, NEG_INF)
        m_prev = m_ref[...]
        m_new = jnp.maximum(m_prev, jnp.max(s, axis=-1, keepdims=True))
        p = jnp.exp((s - jnp.concatenate([m_new] * (kc // LANES), axis=1)).astype(BF16))
        acc_ref[...] = jnp.exp(m_prev - m_new) * acc_ref[...] + _dot(p, v_ref[pl.ds(off, kc), :])
        m_ref[...] = m_new

    per_chunk = SEL_CHUNK // Cq
    nmain = n // per_chunk

    def body(c, carry):
        chunk(pl.multiple_of(2 * c * SEL_CHUNK, SEL_CHUNK), SEL_CHUNK, False)
        chunk(pl.multiple_of((2 * c + 1) * SEL_CHUNK, SEL_CHUNK), SEL_CHUNK, False)
        return carry

    lax.fori_loop(0, nmain // 2, body, 0)

    @pl.when(nmain % 2 == 1)
    def _():
        chunk(pl.multiple_of((nmain - 1) * SEL_CHUNK, SEL_CHUNK), SEL_CHUNK, False)

    tail0 = pl.multiple_of(nmain * SEL_CHUNK, SEL_CHUNK)
    chunk(tail0, SEL_TAIL, True)
    for i in range(1, SEL_CHUNK // SEL_TAIL):
        @pl.when(n - nmain * per_chunk >= i * (SEL_TAIL // Cq))
        def _():
            chunk(tail0 + i * SEL_TAIL, SEL_TAIL, True)

    acc = acc_ref[...]
    o = acc / jnp.maximum(pltpu.roll(acc, HEAD_DIM, 1), SOFTMAX_FLOOR)
    o_ref[...] = _unstack_heads_lo(o, Cq).astype(BF16)


def nsa_selected(q, mneg, k2, e, v2, B, T):
    Hk = k2.shape[1]
    GW = q.shape[-1] // Hk
    G = GW // HEAD_DIM
    kv_spec = pl.BlockSpec((None, None, T, LANES), lambda b, h, n: (b, h, 0, 0))
    tq = SEL_Q
    return pl.pallas_call(
        _nsa_sel_kernel,
        grid=(B, Hk, T // tq),
        in_specs=[pl.BlockSpec((None, tq, GW), lambda b, h, n: (b, n, h)),
                  pl.BlockSpec((None, None, tq, LANES), lambda b, h, n: (b, h, n, 0)),
                  kv_spec,
                  _resident((T, LANES)),
                  kv_spec],
        out_specs=pl.BlockSpec((None, tq, GW), lambda b, h, n: (b, n, h)),
        out_shape=jax.ShapeDtypeStruct((B, T, q.shape[-1]), BF16),
        scratch_shapes=[pltpu.VMEM((G * tq, 2 * LANES), BF16),
                        pltpu.VMEM((G * tq, LANES), F32),
                        pltpu.VMEM((G * tq, LANES), F32)],
        compiler_params=_params("parallel", "parallel", "arbitrary"),
        name="nsa_selected",
    )(q, mneg, k2, e, v2)


def _nsa_out_kernel(oc_ref, os_ref, ow_ref, gates_ref, e_ref, w_ref, x_ref, gate_ref, o_ref):
    D = oc_ref.shape[1]
    sg = jax.nn.sigmoid(gates_ref[...].astype(F32)).astype(BF16)
    gx = _dot(sg, e_ref[...])
    y = (gx[:, :D] * oc_ref[...].astype(F32) + gx[:, D:2 * D] * os_ref[...].astype(F32)
         + gx[:, 2 * D:] * ow_ref[...].astype(F32))
    o_ref[...] = x_ref[...] + gate_ref[...] * _dot(y.astype(BF16), w_ref[...])


def nsa_out(oc, os_, ow, proj, gate_col_tile, expand, w, x, mod, gate_idx, T, tm=512):
    BT, D = x.shape
    tm = min(tm, T)
    tpb = T // tm
    row = pl.BlockSpec((tm, D), lambda i: (i, 0))
    return pl.pallas_call(
        _nsa_out_kernel,
        grid=(BT // tm,),
        in_specs=[row, row, row,
                  pl.BlockSpec((tm, LANES), lambda i: (i, gate_col_tile)),
                  _resident((LANES, 3 * D)),
                  _resident((D, D)),
                  row,
                  pl.BlockSpec((None, None, 1, D), lambda i: (i // tpb, gate_idx, 0, 0))],
        out_specs=row,
        out_shape=jax.ShapeDtypeStruct((BT, D), F32),
        compiler_params=_params("parallel"),
        name="nsa_out",
    )(oc, os_, ow, proj, expand, w, x, mod)


def retention_mixer(x, mod, norm_g, w_in, w_out, B, T):
    H = RET_HEADS
    D = x.shape[1]
    dk = D // H
    nqk = 2 * H * dk
    w_qk = w_in[:, :nqk].reshape(D, 2 * H, dk // 2, 2).swapaxes(2, 3).reshape(D, nqk)
    w_perm = jnp.concatenate([w_qk, w_in[:, nqk:]], axis=1).astype(BF16)
    cos, sin = retention_tables(T, dk)
    scales = (1.0,) * H + (dk ** -0.5,) * H
    proj = normmod_matmul(x, norm_g, mod, 0, w_perm, T, rot=(cos, sin, scales))
    y = retention_core(proj.reshape(B, T, -1), B, T)
    return outproj_residual(y.reshape(B * T, -1), w_out.astype(BF16), x, mod, 2, T)


def swa_mixer(x, mod, norm_g, w_in, q_norm_g, k_norm_g, sinks, w_out, B, T):
    D = x.shape[1]
    Hk = SWA_KV_HEADS
    G = D // HEAD_DIM // Hk
    proj = normmod_matmul(x, norm_g, mod, 0, w_in.astype(BF16), T, tn=256)
    tables = rope_tables(T)
    nq = D // LANES
    nkv = Hk * HEAD_DIM // LANES
    plan = [(j, 0, True, Q_SCALE, ("flat", 0, j)) for j in range(nq)]
    plan += [(nq + j, 1, True, 1.0, ("heads", 1, 2 * j, False)) for j in range(nkv)]
    plan += [(nq + nkv + j, None, False, 1.0, ("heads", 2, 2 * j, True)) for j in range(nkv)]
    q, k2, v2 = head_prep(proj, jnp.stack([q_norm_g, k_norm_g]), tables, plan, nq, [Hk, Hk], B, T)
    o = banded_attention(q.reshape(B, T, D), k2, v2, sinks, B, T, G=G, nprev=SWA_WINDOW // QBLK, window=SWA_WINDOW,
                         unroll=2)
    return outproj_residual(o.reshape(B * T, D), w_out.astype(BF16), x, mod, 2, T)


def nsa_mixer(x, mod, norm_g, w_in, q_norm_g, k_norm_g, cmp_pe, cmp_w1, cmp_w2, w_out, B, T):
    D = x.shape[1]
    Hk = NSA_KV_HEADS
    H = D // HEAD_DIM
    G = H // Hk
    kvw = Hk * HEAD_DIM
    n_main = D + 6 * kvw
    n_pad = -(-w_in.shape[1] // LANES) * LANES
    w_pad = jnp.pad(w_in, ((0, 0), (0, n_pad - w_in.shape[1]))).astype(BF16)
    proj = normmod_matmul(x, norm_g, mod, 0, w_pad, T, tn=LANES * 3)
    tables = rope_tables(T)
    nq = D // LANES
    nkv = kvw // LANES
    col = lambda i: nq + i * nkv
    plan = [(j, 0, True, Q_SCALE, ("flat", 0, j)) for j in range(nq)]
    for j in range(nkv):
        plan += [(col(2) + j, 1, True, 1.0, ("heads", 1, 2 * j, False)),
                 (col(3) + j, None, False, 1.0, ("heads", 2, 2 * j, True)),
                 (col(4) + j, 2, True, 1.0, ("heads", 3, 2 * j, False)),
                 (col(5) + j, None, False, 1.0, ("heads", 4, 2 * j, True))]
    gains = jnp.stack([q_norm_g, k_norm_g[1], k_norm_g[2]])
    q, ks2, vs2, kw2, vw2 = head_prep(proj, gains, tables, plan, nq, [Hk] * 4, B, T)
    q = q.reshape(B, T, D)

    S = NSA_CMP_STRIDE
    p3 = proj.reshape(B, T, n_pad)

    def groups(c0):
        a = p3[:, :, c0:c0 + kvw].reshape(B, T // S, S, Hk, HEAD_DIM)
        return a.transpose(0, 3, 1, 2, 4).reshape(B, Hk, T // S, S * HEAD_DIM)

    kcmp = nsa_compress(groups(D), cmp_pe[0], cmp_w1[0], cmp_w2[0], k_norm_g[0], norm=True)
    vcmp = nsa_compress(groups(D + kvw), cmp_pe[1], cmp_w1[1], cmp_w2[1], k_norm_g[0], norm=False)

    ncp = T // S
    ns = T // NSA_SEL_LEN
    cs = np.arange(ncp)[:, None] * S
    js = np.arange(LANES)[None, :]
    overlap = ((cs < js * NSA_SEL_LEN + NSA_SEL_LEN) & (cs + NSA_CMP_LEN > js * NSA_SEL_LEN)
               & (js < ns) & (np.arange(ncp)[:, None] < ncp - 1))
    ov = jnp.asarray(overlap, BF16)
    oc, mneg = nsa_cmp_topk(q, kcmp, vcmp, ov, B, T, min(NSA_N_SEL, ns))

    blk_onehot = jnp.asarray(np.arange(T)[:, None] // NSA_SEL_LEN == js, BF16)
    os_ = nsa_selected(q, mneg, ks2, blk_onehot, vs2, B, T)
    ow = banded_attention(q, kw2, vw2, None, B, T, G=G, nprev=NSA_WINDOW // QBLK, window=NSA_WINDOW, unroll=True)

    rows = np.arange(LANES)[:, None]
    cols = np.arange(3 * D)[None, :]
    expand = jnp.asarray(rows == 3 * ((cols % D) // HEAD_DIM) + cols // D, BF16)
    flat = lambda a: a.reshape(B * T, D)
    return nsa_out(flat(oc), flat(os_), flat(ow), proj, n_main // LANES, expand, w_out.astype(BF16), x, mod, 2, T)


def kernel(x, c, l0_mod_w, l0_mod_b, l0_norm1_g, l0_ret_w_in, l0_ret_w_out, l0_norm2_g, l0_ffn_w_in, l0_ffn_w_out, l1_mod_w, l1_mod_b, l1_norm1_g, l1_swa_w_in, l1_swa_q_norm_g, l1_swa_k_norm_g, l1_swa_sinks, l1_swa_w_out, l1_norm2_g, l1_ffn_w_in, l1_ffn_w_out, l2_mod_w, l2_mod_b, l2_norm1_g, l2_nsa_w_in, l2_nsa_q_norm_g, l2_nsa_k_norm_g, l2_nsa_cmp_pe, l2_nsa_cmp_w1, l2_nsa_cmp_w2, l2_nsa_w_out, l2_norm2_g, l2_ffn_w_in, l2_ffn_w_out, l3_mod_w, l3_mod_b, l3_norm1_g, l3_ret_w_in, l3_ret_w_out, l3_norm2_g, l3_ffn_w_in, l3_ffn_w_out):
    B, T, D = x.shape
    layers = [
        (l0_mod_w, l0_mod_b, l0_norm1_g, retention_mixer, (l0_ret_w_in, l0_ret_w_out), l0_norm2_g, l0_ffn_w_in, l0_ffn_w_out),
        (l1_mod_w, l1_mod_b, l1_norm1_g, swa_mixer,
         (l1_swa_w_in, l1_swa_q_norm_g, l1_swa_k_norm_g, l1_swa_sinks, l1_swa_w_out), l1_norm2_g, l1_ffn_w_in, l1_ffn_w_out),
        (l2_mod_w, l2_mod_b, l2_norm1_g, nsa_mixer,
         (l2_nsa_w_in, l2_nsa_q_norm_g, l2_nsa_k_norm_g, l2_nsa_cmp_pe, l2_nsa_cmp_w1, l2_nsa_cmp_w2, l2_nsa_w_out),
         l2_norm2_g, l2_ffn_w_in, l2_ffn_w_out),
        (l3_mod_w, l3_mod_b, l3_norm1_g, retention_mixer, (l3_ret_w_in, l3_ret_w_out), l3_norm2_g, l3_ffn_w_in, l3_ffn_w_out),
    ]
    h = x.reshape(B * T, D)
    for mod_w, mod_b, norm1_g, mixer, mixer_params, norm2_g, ffn_w_in, ffn_w_out in layers:
        mod = adaln_mod(c, mod_w, mod_b)
        h = mixer(h, mod, norm1_g, *mixer_params, B, T)
        h = ffn_block(h, norm2_g, mod, ffn_w_in.astype(BF16), ffn_w_out.astype(BF16), T)
    return h.reshape(B, T, D)
```

```python
import functools

import jax
import jax.numpy as jnp
import numpy as np
from jax import lax
from jax.experimental import pallas as pl
from jax.experimental.pallas import tpu as pltpu

F32 = jnp.float32
BF16 = jnp.bfloat16

NORM_EPS = 1e-6
NEG_INF = -1e30
SOFTMAX_FLOOR = 1e-30
REMOVED = -3e38
MASK_BIAS = -(2.0 ** 30)

V7X_VMEM_LIMIT_BYTES = 56 * 2 ** 20
LANES = 128
QBLK = 128

ROPE_THETA = 500000.0
RET_ROT_BASE = 10000.0
RET_HEADS = 4
RET_CHUNK = 256
SWA_KV_HEADS = 2
SWA_WINDOW = 128
NSA_KV_HEADS = 4
NSA_CMP_LEN = 32
NSA_CMP_STRIDE = 16
NSA_SEL_LEN = 64
NSA_N_SEL = 16
NSA_WINDOW = 512
NSA_FORCE_SCORE = 1e4
HEAD_DIM = 64
SEL_CHUNK = 1024
SEL_TAIL = 512
SEL_Q = 512
Q_SCALE = HEAD_DIM ** -0.5

NT_DIMS = (((1,), (1,)), ((), ()))


def _params(*sem):
    return pltpu.CompilerParams(dimension_semantics=sem, vmem_limit_bytes=V7X_VMEM_LIMIT_BYTES)


def _dot(a, b):
    return jnp.dot(a, b, preferred_element_type=F32)


def _dot_nt(a, b):
    return lax.dot_general(a, b, NT_DIMS, preferred_element_type=F32)


def _mod_kernel(c_ref, w_ref, b_ref, o_ref):
    c = c_ref[...]
    s = c * jax.nn.sigmoid(c)
    nb = s.shape[0]
    s8 = jnp.concatenate([s, jnp.zeros((8 - nb, s.shape[1]), F32)], axis=0)
    r = jnp.dot(s8, w_ref[...], preferred_element_type=F32, precision=lax.Precision.HIGHEST)
    o_ref[...] = r[:nb] + b_ref[...]


def adaln_mod(c, mod_w, mod_b):
    B, D = c.shape
    N = mod_w.shape[1]
    tn = N // 4
    out = pl.pallas_call(
        _mod_kernel,
        grid=(N // tn,),
        in_specs=[pl.BlockSpec((B, D), lambda j: (0, 0)),
                  pl.BlockSpec((D, tn), lambda j: (0, j)),
                  pl.BlockSpec((1, tn), lambda j: (0, j))],
        out_specs=pl.BlockSpec((B, tn), lambda j: (0, j)),
        out_shape=jax.ShapeDtypeStruct((B, N), F32),
        compiler_params=_params("arbitrary"),
        name="adaln_mod",
    )(c, mod_w, mod_b.reshape(1, N))
    return out.reshape(B, 6, 1, D)


def _normmod(x, g, scale, shift):
    ms = jnp.mean(x * x, axis=-1, keepdims=True)
    y = x * lax.rsqrt(ms + NORM_EPS) * g
    return y * (1.0 + scale) + shift


def _resident(shape):
    return pl.BlockSpec(shape, lambda *_: (0,) * len(shape), pipeline_mode=pl.Buffered(1))


def _normmod_matmul_kernel(x_ref, g_ref, scale_ref, shift_ref, w_ref, *rest, tn, rot_scales):
    h = _normmod(x_ref[...], g_ref[...], scale_ref[...], shift_ref[...]).astype(BF16)
    o_ref = rest[-1]
    if rot_scales:
        cos, sin = rest[0][...], rest[1][...]
        hw = 2 * cos.shape[1]
    for c in range(w_ref.shape[1] // tn):
        cols = slice(c * tn, (c + 1) * tn)
        acc = _dot(h, w_ref[:, cols])
        if c * tn < len(rot_scales) * (hw if rot_scales else 0):
            parts = []
            for hd in range(tn // hw):
                x1 = acc[:, hd * hw:hd * hw + hw // 2]
                x2 = acc[:, hd * hw + hw // 2:(hd + 1) * hw]
                sc = rot_scales[c * tn // hw + hd]
                parts += [(x1 * cos - x2 * sin) * sc, (x2 * cos + x1 * sin) * sc]
            acc = jnp.concatenate(parts, axis=1)
        o_ref[:, cols] = acc.astype(o_ref.dtype)


def normmod_matmul(x, g, mod, shift_idx, w, T, tm=512, tn=512, rot=None):
    BT, D = x.shape
    N = w.shape[1]
    tm = min(tm, T)
    tn = tn if N % tn == 0 else LANES
    tpb = T // tm
    in_specs = [pl.BlockSpec((tm, D), lambda i: (i, 0)),
                _resident((1, D)),
                pl.BlockSpec((None, None, 1, D), lambda i: (i // tpb, shift_idx + 1, 0, 0)),
                pl.BlockSpec((None, None, 1, D), lambda i: (i // tpb, shift_idx, 0, 0)),
                _resident((D, N))]
    args = [x, g.reshape(1, D), mod, mod, w]
    rot_scales = ()
    if rot is not None:
        cos, sin, rot_scales = rot
        in_specs += [pl.BlockSpec((tm, cos.shape[1]), lambda i: (i % tpb, 0))] * 2
        args += [cos, sin]
    return pl.pallas_call(
        functools.partial(_normmod_matmul_kernel, tn=tn, rot_scales=tuple(rot_scales)),
        grid=(BT // tm,),
        in_specs=in_specs,
        out_specs=pl.BlockSpec((tm, N), lambda i: (i, 0)),
        out_shape=jax.ShapeDtypeStruct((BT, N), BF16),
        compiler_params=_params("parallel"),
        name="normmod_matmul",
    )(*args)


def _outproj_kernel(y_ref, w_ref, x_ref, gate_ref, o_ref):
    o_ref[...] = x_ref[...] + gate_ref[...] * _dot(y_ref[...], w_ref[...])


def outproj_residual(y, w, x, mod, gate_idx, T, tm=512):
    BT, K = y.shape
    D = w.shape[1]
    tm = min(tm, T)
    tpb = T // tm
    return pl.pallas_call(
        _outproj_kernel,
        grid=(BT // tm,),
        in_specs=[pl.BlockSpec((tm, K), lambda i: (i, 0)),
                  _resident((K, D)),
                  pl.BlockSpec((tm, D), lambda i: (i, 0)),
                  pl.BlockSpec((None, None, 1, D), lambda i: (i // tpb, gate_idx, 0, 0))],
        out_specs=pl.BlockSpec((tm, D), lambda i: (i, 0)),
        out_shape=jax.ShapeDtypeStruct((BT, D), F32),
        compiler_params=_params("parallel"),
        name="outproj_residual",
    )(y, w, x, mod)


def _ffn_kernel(x_ref, g_ref, scale_ref, shift_ref, gate_ref, wi_ref, wo_ref, o_ref, *, tc):
    x = x_ref[...]
    h = _normmod(x, g_ref[...], scale_ref[...], shift_ref[...]).astype(BF16)
    F = wo_ref.shape[0]
    acc = None
    for c in range(F // tc):
        a = _dot(h, wi_ref[:, c * tc:(c + 1) * tc])
        b = _dot(h, wi_ref[:, F + c * tc:F + (c + 1) * tc])
        u = (a * jax.nn.sigmoid(a) * b).astype(BF16)
        part = _dot(u, wo_ref[c * tc:(c + 1) * tc, :])
        acc = part if acc is None else acc + part
    o_ref[...] = x + gate_ref[...] * acc


def ffn_block(x, g, mod, w_in, w_out, T, tm=512, tc=256):
    BT, D = x.shape
    F = w_out.shape[0]
    tm = min(tm, T)
    tpb = T // tm
    modspec = lambda idx: pl.BlockSpec((None, None, 1, D), lambda i: (i // tpb, idx, 0, 0))
    return pl.pallas_call(
        functools.partial(_ffn_kernel, tc=tc),
        grid=(BT // tm,),
        in_specs=[pl.BlockSpec((tm, D), lambda i: (i, 0)),
                  _resident((1, D)),
                  modspec(4), modspec(3), modspec(5),
                  _resident((D, 2 * F)),
                  _resident((F, D))],
        out_specs=pl.BlockSpec((tm, D), lambda i: (i, 0)),
        out_shape=jax.ShapeDtypeStruct((BT, D), F32),
        compiler_params=_params("parallel"),
        name="ffn_block",
    )(x, g.reshape(1, D), mod, mod, mod, w_in, w_out)


def _ret_core_kernel(q_ref, k_ref, v_ref, g_ref, o_ref, state_ref, *, chunks):
    C = RET_CHUNK

    @pl.when(pl.program_id(2) == 0)
    def _():
        state_ref[...] = jnp.zeros_like(state_ref)

    head = pl.program_id(1)
    log_gammas = [float(np.log(1.0 - 2.0 ** (-5.0 - i))) for i in range(RET_HEADS)]
    lg = jnp.float32(log_gammas[-1])
    for i in range(RET_HEADS - 1):
        lg = jnp.where(head == i, log_gammas[i], lg)
    rel = (lax.broadcasted_iota(jnp.int32, (C, C), 0) - lax.broadcasted_iota(jnp.int32, (C, C), 1)).astype(F32)
    intra = jnp.where(rel >= 0, jnp.exp(lg * jnp.maximum(rel, 0.0)), 0.0)
    idx = lax.broadcasted_iota(jnp.int32, (C, 1), 0).astype(F32)
    q_decay = jnp.exp(lg * (idx + 1.0))
    k_decay = jnp.exp(lg * (C - 1.0 - idx))
    state_decay = jnp.exp(lg * float(C))

    for c in range(chunks):
        sl = slice(c * C, (c + 1) * C)
        qb = q_ref[sl, :]
        kb = k_ref[sl, :]
        v = v_ref[sl, :]
        scores = _dot_nt(qb, kb) * intra
        inner = _dot(scores.astype(BF16), v)
        state = state_ref[...]
        cross = _dot(qb, state.astype(BF16)) * q_decay
        kd_t = (kb.astype(F32) * k_decay).T.astype(BF16)
        state_ref[...] = state * state_decay + _dot(kd_t, v)
        out = inner + cross
        ms = jnp.mean(out * out, axis=-1, keepdims=True)
        y = out * lax.rsqrt(ms + NORM_EPS)
        g = g_ref[sl, :]
        o_ref[sl, :] = y.astype(BF16) * (g * jax.nn.sigmoid(g))


def retention_core(proj, B, T, tb=1024):
    H = RET_HEADS
    DK = proj.shape[-1] // (6 * H)
    DV = 2 * DK
    tb = min(tb, T)
    kern = functools.partial(_ret_core_kernel, chunks=tb // RET_CHUNK)
    return pl.pallas_call(
        kern,
        grid=(B, H, T // tb),
        in_specs=[pl.BlockSpec((None, tb, DK), lambda b, h, t: (b, t, h)),
                  pl.BlockSpec((None, tb, DK), lambda b, h, t: (b, t, H + h)),
                  pl.BlockSpec((None, tb, DV), lambda b, h, t: (b, t, H + h)),
                  pl.BlockSpec((None, tb, DV), lambda b, h, t: (b, t, 2 * H + h))],
        out_specs=pl.BlockSpec((None, tb, DV), lambda b, h, t: (b, t, h)),
        out_shape=jax.ShapeDtypeStruct((B, T, H * DV), BF16),
        scratch_shapes=[pltpu.VMEM((DK, DV), F32)],
        compiler_params=_params("parallel", "parallel", "arbitrary"),
        name="retention_core",
    )(proj, proj, proj, proj)


def _head_prep_kernel(x_ref, gains_ref, c_ref, s1_ref, s2_ref, bd_ref, *out_refs, plan):
    c, s1, s2 = c_ref[...], s1_ref[...], s2_ref[...]
    bd = bd_ref[...]
    first_half = lax.broadcasted_iota(jnp.int32, c.shape, 1) < HEAD_DIM
    for src, gain_row, rope, scale, dst in plan:
        x = x_ref[:, src * LANES:(src + 1) * LANES].astype(F32)
        if gain_row is not None:
            ss = _dot((x * x).astype(BF16), bd)
            x = x * lax.rsqrt(ss * (1.0 / HEAD_DIM) + NORM_EPS) * gains_ref[gain_row:gain_row + 1, :]
        if rope:
            x = x * c + pltpu.roll(x, LANES - 8, 1) * s1 + pltpu.roll(x, 8, 1) * s2
        if scale != 1.0:
            x = x * scale
        if dst[0] == "flat":
            out = out_refs[dst[1]]
            out[:, dst[2] * LANES:(dst[2] + 1) * LANES] = x.astype(out.dtype)
        else:
            _, out, head0, ones = dst
            swapped = pltpu.roll(x, HEAD_DIM, 1)
            a = jnp.where(first_half, x, 1.0 if ones else swapped)
            b = jnp.where(first_half, swapped, 1.0 if ones else x)
            out_refs[out][head0] = a.astype(BF16)
            out_refs[out][head0 + 1] = b.astype(BF16)


def head_prep(proj, gains, tables, plan, flat_outs, head_outs, B, T, tm=512):
    BT, N = proj.shape
    tm = min(tm, T)
    tpb = T // tm
    bd = jnp.asarray(np.kron(np.eye(2), np.ones((HEAD_DIM, HEAD_DIM))), BF16)
    gains2 = jnp.concatenate([gains, gains], axis=1).astype(F32)
    tab = pl.BlockSpec((tm, LANES), lambda i: (i % tpb, 0))
    out_specs, out_shape = [], []
    for tiles, dtype in flat_outs:
        out_specs.append(pl.BlockSpec((tm, tiles * LANES), lambda i: (i, 0)))
        out_shape.append(jax.ShapeDtypeStruct((BT, tiles * LANES), dtype))
    for nh in head_outs:
        out_specs.append(pl.BlockSpec((None, nh, tm, LANES), lambda i: (i // tpb, 0, i % tpb, 0)))
        out_shape.append(jax.ShapeDtypeStruct((B, nh, T, LANES), BF16))
    return pl.pallas_call(
        functools.partial(_head_prep_kernel, plan=tuple(plan)),
        grid=(BT // tm,),
        in_specs=[pl.BlockSpec((tm, N), lambda i: (i, 0)),
                  _resident(gains2.shape),
                  tab, tab, tab,
                  _resident((LANES, LANES))],
        out_specs=out_specs,
        out_shape=out_shape,
        compiler_params=_params("parallel"),
        name="head_prep",
    )(proj, gains2, *tables, bd)


def rope_tables(T):
    half = HEAD_DIM // 4 // 2
    inv = ROPE_THETA ** (-jnp.arange(half, dtype=F32) / half)
    lane = np.arange(LANES) % HEAD_DIM
    lo, hi = lane < half, (lane >= half) & (lane < 2 * half)
    inv_lane = jnp.where(lo | hi, inv[lane % half], 0.0)
    ang = jnp.arange(T).astype(F32)[:, None] * inv_lane[None, :]
    sin = jnp.sin(ang)
    return jnp.cos(ang), jnp.where(lo, -sin, 0.0), jnp.where(hi, sin, 0.0)


def retention_tables(T, dk):
    inv = 1.0 / (RET_ROT_BASE ** jnp.linspace(0.0, 1.0, dk // 2, dtype=F32))
    ang = jnp.arange(T).astype(F32)[:, None] * inv[None, :]
    return jnp.cos(ang), jnp.sin(ang)


def _stack_heads(q, extra=None):
    first_half = lax.broadcasted_iota(jnp.int32, (q.shape[0], LANES), 1) < HEAD_DIM
    zero = jnp.zeros((q.shape[0], LANES), q.dtype)
    rows = []
    for p in range(q.shape[1] // LANES):
        qp = q[:, p * LANES:(p + 1) * LANES]
        for part in (jnp.where(first_half, qp, zero), jnp.where(first_half, zero, qp)):
            rows.append(part if extra is None else jnp.concatenate([part, extra], axis=1))
    return jnp.concatenate(rows, axis=0)


def _unstack_heads(o, cq):
    first_half = lax.broadcasted_iota(jnp.int32, (cq, LANES), 1) < HEAD_DIM
    G = o.shape[0] // cq
    pairs = [jnp.where(first_half, o[(2 * p) * cq:(2 * p + 1) * cq], o[(2 * p + 1) * cq:(2 * p + 2) * cq])
             for p in range(G // 2)]
    return jnp.concatenate(pairs, axis=1)


def _unstack_heads_lo(o, cq):
    first_half = lax.broadcasted_iota(jnp.int32, (cq, LANES), 1) < HEAD_DIM
    G = o.shape[0] // cq
    pairs = [jnp.where(first_half, o[(2 * p) * cq:(2 * p + 1) * cq],
                       pltpu.roll(o[(2 * p + 1) * cq:(2 * p + 2) * cq], HEAD_DIM, 1))
             for p in range(G // 2)]
    return jnp.concatenate(pairs, axis=1)


def _banded_kernel(*refs, G, nprev, window, use_sink, qb, unroll):
    if use_sink:
        sink_ref, q_ref, k_ref, v_ref, o_ref = refs
    else:
        q_ref, k_ref, v_ref, o_ref = refs
    Cq = QBLK
    nk = (nprev + 1) * Cq
    hk = pl.program_id(1)

    def block(i, carry):
        n = pl.program_id(2) * qb + i
        r0 = pl.multiple_of(i * Cq, Cq)
        start = pl.multiple_of(jnp.maximum(n - nprev, 0) * Cq, Cq)
        kwin = k_ref[pl.ds(start, nk), :]
        vwin = v_ref[pl.ds(start, nk), :]
        lhs = _stack_heads(q_ref[pl.ds(r0, Cq), :])
        s_all = _dot_nt(lhs, kwin)
        t = n * Cq + lax.broadcasted_iota(jnp.int32, (Cq, nk), 0)
        rel = t - (start + lax.broadcasted_iota(jnp.int32, (Cq, nk), 1))
        valid = jnp.logical_and(rel >= 0, rel < window)
        outs = []
        for g in range(G):
            s = jnp.where(valid, s_all[g * Cq:(g + 1) * Cq], NEG_INF)
            m = jnp.max(s, axis=-1, keepdims=True)
            if use_sink:
                sink = sink_ref[hk * G + g]
                m = jnp.maximum(m, sink)
            p = jnp.exp((s - m).astype(BF16))
            acc = _dot(p, vwin)
            denom = pltpu.roll(acc, HEAD_DIM, 1)
            if use_sink:
                denom = denom + jnp.exp(sink - m)
            else:
                denom = jnp.maximum(denom, SOFTMAX_FLOOR)
            outs.append(acc / denom)
        o_ref[pl.ds(r0, Cq), :] = _unstack_heads_lo(jnp.concatenate(outs, axis=0), Cq).astype(BF16)
        return carry

    lax.fori_loop(0, qb, block, 0, unroll=unroll)


def banded_attention(q, k2, v2, sinks, B, T, *, G, nprev, window, unroll, qb=4):
    Hk = k2.shape[1]
    qb = min(qb, T // QBLK)
    use_sink = sinks is not None
    kern = functools.partial(_banded_kernel, G=G, nprev=nprev, window=window, use_sink=use_sink, qb=qb, unroll=unroll)
    in_specs = [pl.BlockSpec((None, qb * QBLK, G * HEAD_DIM), lambda b, h, n: (b, n, h)),
                pl.BlockSpec((None, None, T, LANES), lambda b, h, n: (b, h, 0, 0)),
                pl.BlockSpec((None, None, T, LANES), lambda b, h, n: (b, h, 0, 0))]
    args = [q, k2, v2]
    if use_sink:
        in_specs = [pl.BlockSpec(memory_space=pltpu.SMEM)] + in_specs
        args = [sinks.astype(F32)] + args
    return pl.pallas_call(
        kern,
        grid=(B, Hk, T // (qb * QBLK)),
        in_specs=in_specs,
        out_specs=pl.BlockSpec((None, qb * QBLK, G * HEAD_DIM), lambda b, h, n: (b, n, h)),
        out_shape=jax.ShapeDtypeStruct((B, T, Hk * G * HEAD_DIM), BF16),
        compiler_params=_params("parallel", "parallel", "arbitrary"),
        name="banded_attention",
    )(*args)


def _nsa_compress_kernel(*refs, norm, npairs):
    x_refs = refs[:npairs]
    pe_ref, w1_ref, wbig_ref, w2_ref, gain_ref, bd_ref, o_ref = refs[npairs:]
    S = NSA_CMP_STRIDE
    ng = x_refs[0].shape[0] // S
    nh = o_ref.shape[0]
    u = [None, None]
    for r in range(S):
        xr = jnp.concatenate([x[pl.ds(r, ng, stride=S), :] for x in x_refs], axis=1).astype(BF16)
        for a in range(2):
            part = _dot(xr, wbig_ref[a, r])
            u[a] = part if u[a] is None else u[a] + part
    pe8 = jnp.broadcast_to(pe_ref[...], (8, pe_ref.shape[1]))
    const = _dot(pe8, w1_ref[...])[:1]
    pre = u[0] + pltpu.roll(u[1], ng - 1, 0) + jnp.concatenate([const] * nh, axis=1)
    hid = (pre * jax.nn.sigmoid(pre)).astype(BF16)
    hw = hid.shape[1] // nh
    for h in range(nh):
        y = _dot(hid[:, h * hw:(h + 1) * hw], w2_ref[...])
        if norm:
            ss = _dot((y * y).astype(BF16), bd_ref[...])
            y = y * lax.rsqrt(ss * (1.0 / HEAD_DIM) + NORM_EPS) * gain_ref[...]
        o_ref[h] = y.astype(BF16)


def nsa_compress(xs, pe, w1, w2, gain, Hk, *, norm):
    B, T, _ = xs[0].shape
    W = len(xs) * LANES
    S = NSA_CMP_STRIDE
    hid = w1.shape[1]
    bd = jnp.asarray(np.kron(np.eye(2), np.ones((HEAD_DIM, HEAD_DIM))), BF16)
    gain2 = jnp.concatenate([gain, gain]).reshape(1, LANES).astype(F32)
    w2d = jnp.concatenate([w2, w2], axis=1).astype(BF16)
    w1b = w1.astype(BF16)
    wbig = jnp.einsum("ardc,kh->arkdhc", w1b.reshape(2, S, HEAD_DIM, hid), jnp.eye(Hk, dtype=BF16))
    wbig = wbig.reshape(2, S, W, Hk * hid)
    return pl.pallas_call(
        functools.partial(_nsa_compress_kernel, norm=norm, npairs=len(xs)),
        grid=(B,),
        in_specs=[pl.BlockSpec((None, T, LANES), lambda b: (b, 0, 0))] * len(xs) + [
                  _resident((1, w1.shape[0])),
                  _resident(w1.shape),
                  _resident(wbig.shape),
                  _resident((hid, LANES)),
                  _resident((1, LANES)),
                  _resident((LANES, LANES))],
        out_specs=pl.BlockSpec((None, Hk, T // S, LANES), lambda b: (b, 0, 0, 0)),
        out_shape=jax.ShapeDtypeStruct((B, Hk, T // S, LANES), BF16),
        compiler_params=_params("parallel"),
        name="nsa_compress",
    )(*xs, pe.reshape(1, w1.shape[0]).astype(BF16), w1b, wbig, w2d, gain2, bd)


def _topk_rows_bias(v, n_sel):
    nslab = v.shape[0] // 8
    sub = lax.broadcasted_iota(jnp.int32, (8, v.shape[1]), 0).astype(F32)
    vals = [v[8 * k:8 * k + 8] for k in range(nslab)]
    ids = [sub + 8.0 * k for k in range(nslab)]
    bias = [jnp.full(sub.shape, MASK_BIAS, F32)] * nslab
    for _ in range(n_sel):
        cand = list(zip(vals, ids))
        while len(cand) > 1:
            nxt = []
            for (va, ia), (vb, ib) in zip(cand[0::2], cand[1::2]):
                take_b = vb > va
                nxt.append((jnp.where(take_b, vb, va), jnp.where(take_b, ib, ia)))
            cand = nxt + ([cand[-1]] if len(cand) % 2 else [])
        bv, bi = cand[0]
        for shift in (4, 2, 1):
            rv, ri = pltpu.roll(bv, shift, 0), pltpu.roll(bi, shift, 0)
            take_r = jnp.logical_or(rv > bv, jnp.logical_and(rv == bv, ri < bi))
            bv, bi = jnp.where(take_r, rv, bv), jnp.where(take_r, ri, bi)
        first = [ids[k] == bi for k in range(nslab)]
        vals = [jnp.where(first[k], REMOVED, vals[k]) for k in range(nslab)]
        bias = [jnp.where(first[k], 0.0, bias[k]) for k in range(nslab)]
    return jnp.concatenate(bias, axis=0)


def _nsa_cmp_kernel(q_ref, kc_ref, vc_ref, ov_ref, oc_ref, mneg_ref, imp_ref, *, n_sel, qb):
    Cq = QBLK
    G = q_ref.shape[1] // HEAD_DIM
    ncp = kc_ref.shape[0]
    step = pl.program_id(2)

    def attend(width):
        for i in range(qb):
            n = step * qb + i
            rows = slice(i * Cq, (i + 1) * Cq)
            s_all = _dot_nt(_stack_heads(q_ref[rows, :]), kc_ref[:width, :])
            t = n * Cq + lax.broadcasted_iota(jnp.int32, (Cq, width), 0)
            cmp_end = lax.broadcasted_iota(jnp.int32, (Cq, width), 1) * NSA_CMP_STRIDE + (NSA_CMP_LEN - 1)
            valid = cmp_end <= t
            outs = []
            psum = None
            for g in range(G):
                s = jnp.where(valid, s_all[g * Cq:(g + 1) * Cq], NEG_INF)
                p = jnp.exp(s - jnp.maximum(jnp.max(s, axis=-1, keepdims=True), 0.1 * NEG_INF))
                p = p / jnp.maximum(jnp.sum(p, axis=-1, keepdims=True), SOFTMAX_FLOOR)
                outs.append(_dot(p.astype(BF16), vc_ref[:width, :]))
                psum = p if psum is None else psum + p
            oc_ref[rows, :] = _unstack_heads(jnp.concatenate(outs, axis=0), Cq).astype(BF16)
            hi = psum.astype(BF16)
            lo = (psum - hi.astype(F32)).astype(BF16)
            ov = ov_ref[:width, :]
            imp_ref[i] = _dot(hi, ov) + _dot(lo, ov)

    n_valid = ((step + 1) * qb * Cq - NSA_CMP_LEN) // NSA_CMP_STRIDE + 1
    groups = (n_valid + LANES - 1) // LANES
    widths = list(range(LANES, ncp, LANES)) + [ncp]
    for gi, width in enumerate(widths):
        last = gi == len(widths) - 1
        pl.when(groups >= gi + 1 if last else groups == gi + 1)(functools.partial(attend, width))

    for i in range(qb):
        n = step * qb + i
        rows = slice(i * Cq, (i + 1) * Cq)
        imp = imp_ref[i].T
        nsb = imp.shape[0]
        j = lax.broadcasted_iota(jnp.int32, (nsb, Cq), 0)
        tt = n * Cq + lax.broadcasted_iota(jnp.int32, (nsb, Cq), 1)
        cur = tt // NSA_SEL_LEN
        forced = jnp.logical_or(j == 0, jnp.logical_or(j == cur, j == cur - 1))
        v = jnp.where(forced, NSA_FORCE_SCORE, imp)
        v = jnp.where(j * NSA_SEL_LEN <= tt, v, NEG_INF)
        mneg_ref[rows, :] = _topk_rows_bias(v, n_sel).T.astype(BF16)


def nsa_cmp_topk(q, kcmp, vcmp, ov, B, T, n_sel, qb=2):
    Hk = kcmp.shape[1]
    GW = q.shape[-1] // Hk
    ncp = kcmp.shape[2]
    cmp_spec = pl.BlockSpec((None, None, ncp, LANES), lambda b, h, n: (b, h, 0, 0))
    tq = qb * QBLK
    return pl.pallas_call(
        functools.partial(_nsa_cmp_kernel, n_sel=n_sel, qb=qb),
        grid=(B, Hk, T // tq),
        in_specs=[pl.BlockSpec((None, tq, GW), lambda b, h, n: (b, n, h)),
                  cmp_spec, cmp_spec,
                  pl.BlockSpec((ncp, LANES), lambda b, h, n: (0, 0))],
        out_specs=[pl.BlockSpec((None, tq, GW), lambda b, h, n: (b, n, h)),
                   pl.BlockSpec((None, None, tq, LANES), lambda b, h, n: (b, h, n, 0))],
        out_shape=[jax.ShapeDtypeStruct((B, T, q.shape[-1]), BF16),
                   jax.ShapeDtypeStruct((B, Hk, T, LANES), BF16)],
        scratch_shapes=[pltpu.VMEM((qb, QBLK, LANES), F32)],
        compiler_params=_params("parallel", "parallel", "arbitrary"),
        name="nsa_cmp_topk",
    )(q, kcmp, vcmp, ov)


def _nsa_sel_kernel(q_ref, mneg_ref, k_ref, e_ref, v_ref, o_ref, lhs_ref, m_ref, acc_ref):
    Cq = q_ref.shape[0]
    n = pl.program_id(2)
    lhs_ref[...] = _stack_heads(q_ref[...], extra=mneg_ref[...])
    rows = lhs_ref.shape[0]
    m_ref[...] = jnp.full(m_ref.shape, NEG_INF, F32)
    acc_ref[...] = jnp.zeros_like(acc_ref)

    def chunk(off, kc, causal):
        rhs = jnp.concatenate([k_ref[pl.ds(off, kc), :], e_ref[pl.ds(off, kc), :]], axis=1)
        s = _dot_nt(lhs_ref[...], rhs)
        if causal:
            t = n * Cq + (lax.broadcasted_iota(jnp.int32, (rows, kc), 0) & (Cq - 1))
            s = jnp.where(off + lax.broadcasted_iota(jnp.int32, (rows, kc), 1) <= t, s, NEG_INF)
        m_prev = m_ref[...]
        m_new = jnp.maximum(m_prev, jnp.max(s, axis=-1, keepdims=True))
        p = jnp.exp((s - jnp.concatenate([m_new] * (kc // LANES), axis=1)).astype(BF16))
        acc_ref[...] = jnp.exp(m_prev - m_new) * acc_ref[...] + _dot(p, v_ref[pl.ds(off, kc), :])
        m_ref[...] = m_new

    per_chunk = SEL_CHUNK // Cq
    nmain = n // per_chunk

    def body(c, carry):
        chunk(pl.multiple_of(2 * c * SEL_CHUNK, SEL_CHUNK), SEL_CHUNK, False)
        chunk(pl.multiple_of((2 * c + 1) * SEL_CHUNK, SEL_CHUNK), SEL_CHUNK, False)
        return carry

    lax.fori_loop(0, nmain // 2, body, 0)

    @pl.when(nmain % 2 == 1)
    def _():
        chunk(pl.multiple_of((nmain - 1) * SEL_CHUNK, SEL_CHUNK), SEL_CHUNK, False)

    tail0 = pl.multiple_of(nmain * SEL_CHUNK, SEL_CHUNK)
    chunk(tail0, SEL_TAIL, True)
    for i in range(1, SEL_CHUNK // SEL_TAIL):
        @pl.when(n - nmain * per_chunk >= i * (SEL_TAIL // Cq))
        def _():
            chunk(tail0 + i * SEL_TAIL, SEL_TAIL, True)

    acc = acc_ref[...]
    o = acc / jnp.maximum(pltpu.roll(acc, HEAD_DIM, 1), SOFTMAX_FLOOR)
    o_ref[...] = _unstack_heads_lo(o, Cq).astype(BF16)


def nsa_selected(q, mneg, k2, e, v2, B, T):
    Hk = k2.shape[1]
    GW = q.shape[-1] // Hk
    G = GW // HEAD_DIM
    kv_spec = pl.BlockSpec((None, None, T, LANES), lambda b, h, n: (b, h, 0, 0))
    tq = SEL_Q
    return pl.pallas_call(
        _nsa_sel_kernel,
        grid=(B, Hk, T // tq),
        in_specs=[pl.BlockSpec((None, tq, GW), lambda b, h, n: (b, n, h)),
                  pl.BlockSpec((None, None, tq, LANES), lambda b, h, n: (b, h, n, 0)),
                  kv_spec,
                  _resident((T, LANES)),
                  kv_spec],
        out_specs=pl.BlockSpec((None, tq, GW), lambda b, h, n: (b, n, h)),
        out_shape=jax.ShapeDtypeStruct((B, T, q.shape[-1]), BF16),
        scratch_shapes=[pltpu.VMEM((G * tq, 2 * LANES), BF16),
                        pltpu.VMEM((G * tq, LANES), F32),
                        pltpu.VMEM((G * tq, LANES), F32)],
        compiler_params=_params("parallel", "parallel", "arbitrary"),
        name="nsa_selected",
    )(q, mneg, k2, e, v2)


def _nsa_out_kernel(oc_ref, os_ref, ow_ref, gates_ref, e_ref, w_ref, x_ref, gate_ref, o_ref):
    D = oc_ref.shape[1]
    sg = jax.nn.sigmoid(gates_ref[...].astype(F32)).astype(BF16)
    gx = _dot(sg, e_ref[...])
    y = (gx[:, :D] * oc_ref[...].astype(F32) + gx[:, D:2 * D] * os_ref[...].astype(F32)
         + gx[:, 2 * D:] * ow_ref[...].astype(F32))
    o_ref[...] = x_ref[...] + gate_ref[...] * _dot(y.astype(BF16), w_ref[...])


def nsa_out(oc, os_, ow, proj, gate_col_tile, expand, w, x, mod, gate_idx, T, tm=512):
    BT, D = x.shape
    tm = min(tm, T)
    tpb = T // tm
    row = pl.BlockSpec((tm, D), lambda i: (i, 0))
    return pl.pallas_call(
        _nsa_out_kernel,
        grid=(BT // tm,),
        in_specs=[row, row, row,
                  pl.BlockSpec((tm, LANES), lambda i: (i, gate_col_tile)),
                  _resident((LANES, 3 * D)),
                  _resident((D, D)),
                  row,
                  pl.BlockSpec((None, None, 1, D), lambda i: (i // tpb, gate_idx, 0, 0))],
        out_specs=row,
        out_shape=jax.ShapeDtypeStruct((BT, D), F32),
        compiler_params=_params("parallel"),
        name="nsa_out",
    )(oc, os_, ow, proj, expand, w, x, mod)


def retention_mixer(x, mod, norm_g, w_in, w_out, B, T):
    H = RET_HEADS
    D = x.shape[1]
    dk = D // H
    nqk = 2 * H * dk
    w_qk = w_in[:, :nqk].reshape(D, 2 * H, dk // 2, 2).swapaxes(2, 3).reshape(D, nqk)
    w_perm = jnp.concatenate([w_qk, w_in[:, nqk:]], axis=1).astype(BF16)
    cos, sin = retention_tables(T, dk)
    scales = (1.0,) * H + (dk ** -0.5,) * H
    proj = normmod_matmul(x, norm_g, mod, 0, w_perm, T, rot=(cos, sin, scales))
    y = retention_core(proj.reshape(B, T, -1), B, T)
    return outproj_residual(y.reshape(B * T, -1), w_out.astype(BF16), x, mod, 2, T)


def swa_mixer(x, mod, norm_g, w_in, q_norm_g, k_norm_g, sinks, w_out, B, T):
    D = x.shape[1]
    Hk = SWA_KV_HEADS
    G = D // HEAD_DIM // Hk
    proj = normmod_matmul(x, norm_g, mod, 0, w_in.astype(BF16), T, tn=256)
    tables = rope_tables(T)
    nq = D // LANES
    nkv = Hk * HEAD_DIM // LANES
    plan = [(j, 0, True, Q_SCALE, ("flat", 0, j)) for j in range(nq)]
    plan += [(nq + j, 1, True, 1.0, ("heads", 1, 2 * j, False)) for j in range(nkv)]
    plan += [(nq + nkv + j, None, False, 1.0, ("heads", 2, 2 * j, True)) for j in range(nkv)]
    q, k2, v2 = head_prep(proj, jnp.stack([q_norm_g, k_norm_g]), tables, plan, [(nq, BF16)], [Hk, Hk], B, T)
    o = banded_attention(q.reshape(B, T, D), k2, v2, sinks, B, T, G=G, nprev=SWA_WINDOW // QBLK, window=SWA_WINDOW,
                         unroll=2)
    return outproj_residual(o.reshape(B * T, D), w_out.astype(BF16), x, mod, 2, T)


def nsa_mixer(x, mod, norm_g, w_in, q_norm_g, k_norm_g, cmp_pe, cmp_w1, cmp_w2, w_out, B, T):
    D = x.shape[1]
    Hk = NSA_KV_HEADS
    H = D // HEAD_DIM
    G = H // Hk
    kvw = Hk * HEAD_DIM
    n_main = D + 6 * kvw
    n_pad = -(-w_in.shape[1] // LANES) * LANES
    w_pad = jnp.pad(w_in, ((0, 0), (0, n_pad - w_in.shape[1]))).astype(BF16)
    proj = normmod_matmul(x, norm_g, mod, 0, w_pad, T, tn=LANES * 3)
    tables = rope_tables(T)
    nq = D // LANES
    nkv = kvw // LANES
    col = lambda i: nq + i * nkv
    plan = [(j, 0, True, Q_SCALE, ("flat", 0, j)) for j in range(nq)]
    for j in range(nkv):
        plan += [(col(0) + j, None, False, 1.0, ("flat", 1 + j, 0)),
                 (col(1) + j, None, False, 1.0, ("flat", 1 + nkv + j, 0)),
                 (col(2) + j, 1, True, 1.0, ("heads", 1 + 2 * nkv, 2 * j, False)),
                 (col(3) + j, None, False, 1.0, ("heads", 2 + 2 * nkv, 2 * j, True)),
                 (col(4) + j, 2, True, 1.0, ("heads", 3 + 2 * nkv, 2 * j, False)),
                 (col(5) + j, None, False, 1.0, ("heads", 4 + 2 * nkv, 2 * j, True))]
    gains = jnp.stack([q_norm_g, k_norm_g[1], k_norm_g[2]])
    outs = head_prep(proj, gains, tables, plan, [(nq, BF16)] + [(1, F32)] * (2 * nkv), [Hk] * 4, B, T)
    q = outs[0].reshape(B, T, D)
    kc = [a.reshape(B, T, LANES) for a in outs[1:1 + nkv]]
    vc = [a.reshape(B, T, LANES) for a in outs[1 + nkv:1 + 2 * nkv]]
    ks2, vs2, kw2, vw2 = outs[1 + 2 * nkv:]

    S = NSA_CMP_STRIDE
    kcmp = nsa_compress(kc, cmp_pe[0], cmp_w1[0], cmp_w2[0], k_norm_g[0], Hk, norm=True)
    vcmp = nsa_compress(vc, cmp_pe[1], cmp_w1[1], cmp_w2[1], k_norm_g[0], Hk, norm=False)

    ncp = T // S
    ns = T // NSA_SEL_LEN
    cs = np.arange(ncp)[:, None] * S
    js = np.arange(LANES)[None, :]
    overlap = ((cs < js * NSA_SEL_LEN + NSA_SEL_LEN) & (cs + NSA_CMP_LEN > js * NSA_SEL_LEN)
               & (js < ns) & (np.arange(ncp)[:, None] < ncp - 1))
    ov = jnp.asarray(overlap, BF16)
    oc, mneg = nsa_cmp_topk(q, kcmp, vcmp, ov, B, T, min(NSA_N_SEL, ns))

    blk_onehot = jnp.asarray(np.arange(T)[:, None] // NSA_SEL_LEN == js, BF16)
    os_ = nsa_selected(q, mneg, ks2, blk_onehot, vs2, B, T)
    ow = banded_attention(q, kw2, vw2, None, B, T, G=G, nprev=NSA_WINDOW // QBLK, window=NSA_WINDOW, unroll=True)

    rows = np.arange(LANES)[:, None]
    cols = np.arange(3 * D)[None, :]
    expand = jnp.asarray(rows == 3 * ((cols % D) // HEAD_DIM) + cols // D, BF16)
    flat = lambda a: a.reshape(B * T, D)
    return nsa_out(flat(oc), flat(os_), flat(ow), proj, n_main // LANES, expand, w_out.astype(BF16), x, mod, 2, T)


def kernel(x, c, l0_mod_w, l0_mod_b, l0_norm1_g, l0_ret_w_in, l0_ret_w_out, l0_norm2_g, l0_ffn_w_in, l0_ffn_w_out, l1_mod_w, l1_mod_b, l1_norm1_g, l1_swa_w_in, l1_swa_q_norm_g, l1_swa_k_norm_g, l1_swa_sinks, l1_swa_w_out, l1_norm2_g, l1_ffn_w_in, l1_ffn_w_out, l2_mod_w, l2_mod_b, l2_norm1_g, l2_nsa_w_in, l2_nsa_q_norm_g, l2_nsa_k_norm_g, l2_nsa_cmp_pe, l2_nsa_cmp_w1, l2_nsa_cmp_w2, l2_nsa_w_out, l2_norm2_g, l2_ffn_w_in, l2_ffn_w_out, l3_mod_w, l3_mod_b, l3_norm1_g, l3_ret_w_in, l3_ret_w_out, l3_norm2_g, l3_ffn_w_in, l3_ffn_w_out):
    B, T, D = x.shape
    layers = [
        (l0_mod_w, l0_mod_b, l0_norm1_g, retention_mixer, (l0_ret_w_in, l0_ret_w_out), l0_norm2_g, l0_ffn_w_in, l0_ffn_w_out),
        (l1_mod_w, l1_mod_b, l1_norm1_g, swa_mixer,
         (l1_swa_w_in, l1_swa_q_norm_g, l1_swa_k_norm_g, l1_swa_sinks, l1_swa_w_out), l1_norm2_g, l1_ffn_w_in, l1_ffn_w_out),
        (l2_mod_w, l2_mod_b, l2_norm1_g, nsa_mixer,
         (l2_nsa_w_in, l2_nsa_q_norm_g, l2_nsa_k_norm_g, l2_nsa_cmp_pe, l2_nsa_cmp_w1, l2_nsa_cmp_w2, l2_nsa_w_out),
         l2_norm2_g, l2_ffn_w_in, l2_ffn_w_out),
        (l3_mod_w, l3_mod_b, l3_norm1_g, retention_mixer, (l3_ret_w_in, l3_ret_w_out), l3_norm2_g, l3_ffn_w_in, l3_ffn_w_out),
    ]
    h = x.reshape(B * T, D)
    for mod_w, mod_b, norm1_g, mixer, mixer_params, norm2_g, ffn_w_in, ffn_w_out in layers:
        mod = adaln_mod(c, mod_w, mod_b)
        h = mixer(h, mod, norm1_g, *mixer_params, B, T)
        h = ffn_block(h, norm2_g, mod, ffn_w_in.astype(BF16), ffn_w_out.astype(BF16), T)
    return h.reshape(B, T, D)
```

```python
import functools

import jax
import jax.numpy as jnp
import numpy as np
from jax import lax
from jax.experimental import pallas as pl
from jax.experimental.pallas import tpu as pltpu

F32 = jnp.float32
BF16 = jnp.bfloat16

NORM_EPS = 1e-6
NEG_INF = -1e30
SOFTMAX_FLOOR = 1e-30
REMOVED = -3e38
MASK_BIAS = -(2.0 ** 30)

V7X_VMEM_LIMIT_BYTES = 56 * 2 ** 20
LANES = 128
QBLK = 128

ROPE_THETA = 500000.0
RET_ROT_BASE = 10000.0
RET_HEADS = 4
RET_CHUNK = 256
SWA_KV_HEADS = 2
SWA_WINDOW = 128
NSA_KV_HEADS = 4
NSA_CMP_LEN = 32
NSA_CMP_STRIDE = 16
NSA_SEL_LEN = 64
NSA_N_SEL = 16
NSA_WINDOW = 512
NSA_FORCE_SCORE = 1e4
HEAD_DIM = 64
SEL_CHUNK = 1024
SEL_TAIL = 512
SEL_Q = 512
Q_SCALE = HEAD_DIM ** -0.5

NT_DIMS = (((1,), (1,)), ((), ()))


def _params(*sem):
    return pltpu.CompilerParams(dimension_semantics=sem, vmem_limit_bytes=V7X_VMEM_LIMIT_BYTES)


def _dot(a, b):
    return jnp.dot(a, b, preferred_element_type=F32)


def _dot_nt(a, b):
    return lax.dot_general(a, b, NT_DIMS, preferred_element_type=F32)


def _mod_kernel(c_ref, w_ref, b_ref, o_ref):
    c = c_ref[...]
    s = c * jax.nn.sigmoid(c)
    nb = s.shape[0]
    s8 = jnp.concatenate([s, jnp.zeros((8 - nb, s.shape[1]), F32)], axis=0)
    r = jnp.dot(s8, w_ref[...], preferred_element_type=F32, precision=lax.Precision.HIGHEST)
    o_ref[...] = r[:nb] + b_ref[...]


def adaln_mod(c, mod_w, mod_b):
    B, D = c.shape
    N = mod_w.shape[1]
    tn = N // 4
    out = pl.pallas_call(
        _mod_kernel,
        grid=(N // tn,),
        in_specs=[pl.BlockSpec((B, D), lambda j: (0, 0)),
                  pl.BlockSpec((D, tn), lambda j: (0, j)),
                  pl.BlockSpec((1, tn), lambda j: (0, j))],
        out_specs=pl.BlockSpec((B, tn), lambda j: (0, j)),
        out_shape=jax.ShapeDtypeStruct((B, N), F32),
        compiler_params=_params("arbitrary"),
        name="adaln_mod",
    )(c, mod_w, mod_b.reshape(1, N))
    return out.reshape(B, 6, 1, D)


def _normmod(x, g, scale, shift):
    ms = jnp.mean(x * x, axis=-1, keepdims=True)
    y = x * lax.rsqrt(ms + NORM_EPS) * g
    return y * (1.0 + scale) + shift


def _resident(shape):
    return pl.BlockSpec(shape, lambda *_: (0,) * len(shape), pipeline_mode=pl.Buffered(1))


def _normmod_matmul_kernel(x_ref, g_ref, scale_ref, shift_ref, w_ref, *rest, tn, rot_scales):
    h = _normmod(x_ref[...], g_ref[...], scale_ref[...], shift_ref[...]).astype(BF16)
    o_ref = rest[-1]
    if rot_scales:
        cos, sin = rest[0][...], rest[1][...]
        hw = 2 * cos.shape[1]
    for c in range(w_ref.shape[1] // tn):
        cols = slice(c * tn, (c + 1) * tn)
        acc = _dot(h, w_ref[:, cols])
        if c * tn < len(rot_scales) * (hw if rot_scales else 0):
            parts = []
            for hd in range(tn // hw):
                x1 = acc[:, hd * hw:hd * hw + hw // 2]
                x2 = acc[:, hd * hw + hw // 2:(hd + 1) * hw]
                sc = rot_scales[c * tn // hw + hd]
                parts += [(x1 * cos - x2 * sin) * sc, (x2 * cos + x1 * sin) * sc]
            acc = jnp.concatenate(parts, axis=1)
        o_ref[:, cols] = acc.astype(o_ref.dtype)


def normmod_matmul(x, g, mod, shift_idx, w, T, tm=512, tn=512, rot=None):
    BT, D = x.shape
    N = w.shape[1]
    tm = min(tm, T)
    tn = tn if N % tn == 0 else LANES
    tpb = T // tm
    in_specs = [pl.BlockSpec((tm, D), lambda i: (i, 0)),
                _resident((1, D)),
                pl.BlockSpec((None, None, 1, D), lambda i: (i // tpb, shift_idx + 1, 0, 0)),
                pl.BlockSpec((None, None, 1, D), lambda i: (i // tpb, shift_idx, 0, 0)),
                _resident((D, N))]
    args = [x, g.reshape(1, D), mod, mod, w]
    rot_scales = ()
    if rot is not None:
        cos, sin, rot_scales = rot
        in_specs += [pl.BlockSpec((tm, cos.shape[1]), lambda i: (i % tpb, 0))] * 2
        args += [cos, sin]
    return pl.pallas_call(
        functools.partial(_normmod_matmul_kernel, tn=tn, rot_scales=tuple(rot_scales)),
        grid=(BT // tm,),
        in_specs=in_specs,
        out_specs=pl.BlockSpec((tm, N), lambda i: (i, 0)),
        out_shape=jax.ShapeDtypeStruct((BT, N), BF16),
        compiler_params=_params("parallel"),
        name="normmod_matmul",
    )(*args)


def _outproj_kernel(y_ref, w_ref, x_ref, gate_ref, o_ref):
    o_ref[...] = x_ref[...] + gate_ref[...] * _dot(y_ref[...], w_ref[...])


def outproj_residual(y, w, x, mod, gate_idx, T, tm=512):
    BT, K = y.shape
    D = w.shape[1]
    tm = min(tm, T)
    tpb = T // tm
    return pl.pallas_call(
        _outproj_kernel,
        grid=(BT // tm,),
        in_specs=[pl.BlockSpec((tm, K), lambda i: (i, 0)),
                  _resident((K, D)),
                  pl.BlockSpec((tm, D), lambda i: (i, 0)),
                  pl.BlockSpec((None, None, 1, D), lambda i: (i // tpb, gate_idx, 0, 0))],
        out_specs=pl.BlockSpec((tm, D), lambda i: (i, 0)),
        out_shape=jax.ShapeDtypeStruct((BT, D), F32),
        compiler_params=_params("parallel"),
        name="outproj_residual",
    )(y, w, x, mod)


def _ffn_kernel(x_ref, g_ref, scale_ref, shift_ref, gate_ref, wi_ref, wo_ref, o_ref, *, tc):
    x = x_ref[...]
    h = _normmod(x, g_ref[...], scale_ref[...], shift_ref[...]).astype(BF16)
    F = wo_ref.shape[0]
    acc = None
    for c in range(F // tc):
        a = _dot(h, wi_ref[:, c * tc:(c + 1) * tc])
        b = _dot(h, wi_ref[:, F + c * tc:F + (c + 1) * tc])
        u = (a * jax.nn.sigmoid(a) * b).astype(BF16)
        part = _dot(u, wo_ref[c * tc:(c + 1) * tc, :])
        acc = part if acc is None else acc + part
    o_ref[...] = x + gate_ref[...] * acc


def ffn_block(x, g, mod, w_in, w_out, T, tm=512, tc=256):
    BT, D = x.shape
    F = w_out.shape[0]
    tm = min(tm, T)
    tpb = T // tm
    modspec = lambda idx: pl.BlockSpec((None, None, 1, D), lambda i: (i // tpb, idx, 0, 0))
    return pl.pallas_call(
        functools.partial(_ffn_kernel, tc=tc),
        grid=(BT // tm,),
        in_specs=[pl.BlockSpec((tm, D), lambda i: (i, 0)),
                  _resident((1, D)),
                  modspec(4), modspec(3), modspec(5),
                  _resident((D, 2 * F)),
                  _resident((F, D))],
        out_specs=pl.BlockSpec((tm, D), lambda i: (i, 0)),
        out_shape=jax.ShapeDtypeStruct((BT, D), F32),
        compiler_params=_params("parallel"),
        name="ffn_block",
    )(x, g.reshape(1, D), mod, mod, mod, w_in, w_out)


def _ret_core_kernel(q_ref, k_ref, v_ref, g_ref, o_ref, state_ref, *, chunks):
    C = RET_CHUNK

    @pl.when(pl.program_id(2) == 0)
    def _():
        state_ref[...] = jnp.zeros_like(state_ref)

    head = pl.program_id(1)
    log_gammas = [float(np.log(1.0 - 2.0 ** (-5.0 - i))) for i in range(RET_HEADS)]
    lg = jnp.float32(log_gammas[-1])
    for i in range(RET_HEADS - 1):
        lg = jnp.where(head == i, log_gammas[i], lg)
    rel = (lax.broadcasted_iota(jnp.int32, (C, C), 0) - lax.broadcasted_iota(jnp.int32, (C, C), 1)).astype(F32)
    intra = jnp.where(rel >= 0, jnp.exp(lg * jnp.maximum(rel, 0.0)), 0.0)
    idx = lax.broadcasted_iota(jnp.int32, (C, 1), 0).astype(F32)
    q_decay = jnp.exp(lg * (idx + 1.0))
    k_decay = jnp.exp(lg * (C - 1.0 - idx))
    state_decay = jnp.exp(lg * float(C))

    for c in range(chunks):
        sl = slice(c * C, (c + 1) * C)
        qb = q_ref[sl, :]
        kb = k_ref[sl, :]
        v = v_ref[sl, :]
        scores = _dot_nt(qb, kb) * intra
        inner = _dot(scores.astype(BF16), v)
        state = state_ref[...]
        cross = _dot(qb, state.astype(BF16)) * q_decay
        kd_t = (kb.astype(F32) * k_decay).T.astype(BF16)
        state_ref[...] = state * state_decay + _dot(kd_t, v)
        out = inner + cross
        ms = jnp.mean(out * out, axis=-1, keepdims=True)
        y = out * lax.rsqrt(ms + NORM_EPS)
        g = g_ref[sl, :]
        o_ref[sl, :] = y.astype(BF16) * (g * jax.nn.sigmoid(g))


def retention_core(proj, B, T, tb=1024):
    H = RET_HEADS
    DK = proj.shape[-1] // (6 * H)
    DV = 2 * DK
    tb = min(tb, T)
    kern = functools.partial(_ret_core_kernel, chunks=tb // RET_CHUNK)
    return pl.pallas_call(
        kern,
        grid=(B, H, T // tb),
        in_specs=[pl.BlockSpec((None, tb, DK), lambda b, h, t: (b, t, h)),
                  pl.BlockSpec((None, tb, DK), lambda b, h, t: (b, t, H + h)),
                  pl.BlockSpec((None, tb, DV), lambda b, h, t: (b, t, H + h)),
                  pl.BlockSpec((None, tb, DV), lambda b, h, t: (b, t, 2 * H + h))],
        out_specs=pl.BlockSpec((None, tb, DV), lambda b, h, t: (b, t, h)),
        out_shape=jax.ShapeDtypeStruct((B, T, H * DV), BF16),
        scratch_shapes=[pltpu.VMEM((DK, DV), F32)],
        compiler_params=_params("parallel", "parallel", "arbitrary"),
        name="retention_core",
    )(proj, proj, proj, proj)


def _head_prep_kernel(x_ref, gains_ref, c_ref, s1_ref, s2_ref, bd_ref, *out_refs, plan):
    c, s1, s2 = c_ref[...], s1_ref[...], s2_ref[...]
    bd = bd_ref[...]
    first_half = lax.broadcasted_iota(jnp.int32, c.shape, 1) < HEAD_DIM
    for src, gain_row, rope, scale, dst in plan:
        x = x_ref[:, src * LANES:(src + 1) * LANES].astype(F32)
        if gain_row is not None:
            ss = _dot((x * x).astype(BF16), bd)
            x = x * lax.rsqrt(ss * (1.0 / HEAD_DIM) + NORM_EPS) * gains_ref[gain_row:gain_row + 1, :]
        if rope:
            x = x * c + pltpu.roll(x, LANES - 8, 1) * s1 + pltpu.roll(x, 8, 1) * s2
        if scale != 1.0:
            x = x * scale
        if dst[0] == "flat":
            out = out_refs[dst[1]]
            out[:, dst[2] * LANES:(dst[2] + 1) * LANES] = x.astype(out.dtype)
        else:
            _, out, head0, ones = dst
            swapped = pltpu.roll(x, HEAD_DIM, 1)
            a = jnp.where(first_half, x, 1.0 if ones else swapped)
            b = jnp.where(first_half, swapped, 1.0 if ones else x)
            out_refs[out][head0] = a.astype(BF16)
            out_refs[out][head0 + 1] = b.astype(BF16)


def head_prep(proj, gains, tables, plan, flat_outs, head_outs, B, T, tm=512):
    BT, N = proj.shape
    tm = min(tm, T)
    tpb = T // tm
    bd = jnp.asarray(np.kron(np.eye(2), np.ones((HEAD_DIM, HEAD_DIM))), BF16)
    gains2 = jnp.concatenate([gains, gains], axis=1).astype(F32)
    tab = pl.BlockSpec((tm, LANES), lambda i: (i % tpb, 0))
    out_specs, out_shape = [], []
    for tiles, dtype in flat_outs:
        out_specs.append(pl.BlockSpec((tm, tiles * LANES), lambda i: (i, 0)))
        out_shape.append(jax.ShapeDtypeStruct((BT, tiles * LANES), dtype))
    for nh in head_outs:
        out_specs.append(pl.BlockSpec((None, nh, tm, LANES), lambda i: (i // tpb, 0, i % tpb, 0)))
        out_shape.append(jax.ShapeDtypeStruct((B, nh, T, LANES), BF16))
    return pl.pallas_call(
        functools.partial(_head_prep_kernel, plan=tuple(plan)),
        grid=(BT // tm,),
        in_specs=[pl.BlockSpec((tm, N), lambda i: (i, 0)),
                  _resident(gains2.shape),
                  tab, tab, tab,
                  _resident((LANES, LANES))],
        out_specs=out_specs,
        out_shape=out_shape,
        compiler_params=_params("parallel"),
        name="head_prep",
    )(proj, gains2, *tables, bd)


def rope_tables(T):
    half = HEAD_DIM // 4 // 2
    inv = ROPE_THETA ** (-jnp.arange(half, dtype=F32) / half)
    lane = np.arange(LANES) % HEAD_DIM
    lo, hi = lane < half, (lane >= half) & (lane < 2 * half)
    inv_lane = jnp.where(lo | hi, inv[lane % half], 0.0)
    ang = jnp.arange(T).astype(F32)[:, None] * inv_lane[None, :]
    sin = jnp.sin(ang)
    return jnp.cos(ang), jnp.where(lo, -sin, 0.0), jnp.where(hi, sin, 0.0)


def retention_tables(T, dk):
    inv = 1.0 / (RET_ROT_BASE ** jnp.linspace(0.0, 1.0, dk // 2, dtype=F32))
    ang = jnp.arange(T).astype(F32)[:, None] * inv[None, :]
    return jnp.cos(ang), jnp.sin(ang)


def _stack_heads(q, extra=None):
    first_half = lax.broadcasted_iota(jnp.int32, (q.shape[0], LANES), 1) < HEAD_DIM
    zero = jnp.zeros((q.shape[0], LANES), q.dtype)
    rows = []
    for p in range(q.shape[1] // LANES):
        qp = q[:, p * LANES:(p + 1) * LANES]
        for part in (jnp.where(first_half, qp, zero), jnp.where(first_half, zero, qp)):
            rows.append(part if extra is None else jnp.concatenate([part, extra], axis=1))
    return jnp.concatenate(rows, axis=0)


def _unstack_heads(o, cq):
    first_half = lax.broadcasted_iota(jnp.int32, (cq, LANES), 1) < HEAD_DIM
    G = o.shape[0] // cq
    pairs = [jnp.where(first_half, o[(2 * p) * cq:(2 * p + 1) * cq], o[(2 * p + 1) * cq:(2 * p + 2) * cq])
             for p in range(G // 2)]
    return jnp.concatenate(pairs, axis=1)


def _unstack_heads_lo(o, cq):
    first_half = lax.broadcasted_iota(jnp.int32, (cq, LANES), 1) < HEAD_DIM
    G = o.shape[0] // cq
    pairs = [jnp.where(first_half, o[(2 * p) * cq:(2 * p + 1) * cq],
                       pltpu.roll(o[(2 * p + 1) * cq:(2 * p + 2) * cq], HEAD_DIM, 1))
             for p in range(G // 2)]
    return jnp.concatenate(pairs, axis=1)


def _banded_kernel(*refs, G, nprev, window, use_sink, qb, unroll):
    if use_sink:
        sink_ref, q_ref, k_ref, v_ref, o_ref = refs
    else:
        q_ref, k_ref, v_ref, o_ref = refs
    Cq = QBLK
    nk = (nprev + 1) * Cq
    hk = pl.program_id(1)

    def block(i, carry):
        n = pl.program_id(2) * qb + i
        r0 = pl.multiple_of(i * Cq, Cq)
        start = pl.multiple_of(jnp.maximum(n - nprev, 0) * Cq, Cq)
        kwin = k_ref[pl.ds(start, nk), :]
        vwin = v_ref[pl.ds(start, nk), :]
        lhs = _stack_heads(q_ref[pl.ds(r0, Cq), :])
        s_all = _dot_nt(lhs, kwin)
        t = n * Cq + lax.broadcasted_iota(jnp.int32, (Cq, nk), 0)
        rel = t - (start + lax.broadcasted_iota(jnp.int32, (Cq, nk), 1))
        valid = jnp.logical_and(rel >= 0, rel < window)
        outs = []
        for g in range(G):
            s = jnp.where(valid, s_all[g * Cq:(g + 1) * Cq], NEG_INF)
            m = jnp.max(s, axis=-1, keepdims=True)
            if use_sink:
                sink = sink_ref[hk * G + g]
                m = jnp.maximum(m, sink)
            p = jnp.exp((s - m).astype(BF16))
            acc = _dot(p, vwin)
            denom = pltpu.roll(acc, HEAD_DIM, 1)
            if use_sink:
                denom = denom + jnp.exp(sink - m)
            else:
                denom = jnp.maximum(denom, SOFTMAX_FLOOR)
            outs.append(acc / denom)
        o_ref[pl.ds(r0, Cq), :] = _unstack_heads_lo(jnp.concatenate(outs, axis=0), Cq).astype(BF16)
        return carry

    lax.fori_loop(0, qb, block, 0, unroll=unroll)


def banded_attention(q, k2, v2, sinks, B, T, *, G, nprev, window, unroll, qb=4):
    Hk = k2.shape[1]
    qb = min(qb, T // QBLK)
    use_sink = sinks is not None
    kern = functools.partial(_banded_kernel, G=G, nprev=nprev, window=window, use_sink=use_sink, qb=qb, unroll=unroll)
    in_specs = [pl.BlockSpec((None, qb * QBLK, G * HEAD_DIM), lambda b, h, n: (b, n, h)),
                pl.BlockSpec((None, None, T, LANES), lambda b, h, n: (b, h, 0, 0)),
                pl.BlockSpec((None, None, T, LANES), lambda b, h, n: (b, h, 0, 0))]
    args = [q, k2, v2]
    if use_sink:
        in_specs = [pl.BlockSpec(memory_space=pltpu.SMEM)] + in_specs
        args = [sinks.astype(F32)] + args
    return pl.pallas_call(
        kern,
        grid=(B, Hk, T // (qb * QBLK)),
        in_specs=in_specs,
        out_specs=pl.BlockSpec((None, qb * QBLK, G * HEAD_DIM), lambda b, h, n: (b, n, h)),
        out_shape=jax.ShapeDtypeStruct((B, T, Hk * G * HEAD_DIM), BF16),
        compiler_params=_params("parallel", "parallel", "arbitrary"),
        name="banded_attention",
    )(*args)


def _nsa_compress_kernel(*refs, norm, npairs):
    x_refs = refs[:npairs]
    pe_ref, w1_ref, wbig_ref, w2_ref, gain_ref, bd_ref, o_ref = refs[npairs:]
    S = NSA_CMP_STRIDE
    ng = x_refs[0].shape[0] // S
    nh = o_ref.shape[0]
    u = [None, None]
    for r in range(S):
        xr = jnp.concatenate([x[pl.ds(r, ng, stride=S), :] for x in x_refs], axis=1).astype(BF16)
        for a in range(2):
            part = _dot(xr, wbig_ref[a, r])
            u[a] = part if u[a] is None else u[a] + part
    pe8 = jnp.broadcast_to(pe_ref[...], (8, pe_ref.shape[1]))
    const = _dot(pe8, w1_ref[...])[:1]
    pre = u[0] + pltpu.roll(u[1], ng - 1, 0) + jnp.concatenate([const] * nh, axis=1)
    hid = (pre * jax.nn.sigmoid(pre)).astype(BF16)
    hw = hid.shape[1] // nh
    for h in range(nh):
        y = _dot(hid[:, h * hw:(h + 1) * hw], w2_ref[...])
        if norm:
            ss = _dot((y * y).astype(BF16), bd_ref[...])
            y = y * lax.rsqrt(ss * (1.0 / HEAD_DIM) + NORM_EPS) * gain_ref[...]
        o_ref[h] = y.astype(BF16)


def nsa_compress(xs, pe, w1, w2, gain, Hk, *, norm):
    B, T, _ = xs[0].shape
    W = len(xs) * LANES
    S = NSA_CMP_STRIDE
    hid = w1.shape[1]
    bd = jnp.asarray(np.kron(np.eye(2), np.ones((HEAD_DIM, HEAD_DIM))), BF16)
    gain2 = jnp.concatenate([gain, gain]).reshape(1, LANES).astype(F32)
    w2d = jnp.concatenate([w2, w2], axis=1).astype(BF16)
    w1b = w1.astype(BF16)
    wbig = jnp.einsum("ardc,kh->arkdhc", w1b.reshape(2, S, HEAD_DIM, hid), jnp.eye(Hk, dtype=BF16))
    wbig = wbig.reshape(2, S, W, Hk * hid)
    return pl.pallas_call(
        functools.partial(_nsa_compress_kernel, norm=norm, npairs=len(xs)),
        grid=(B,),
        in_specs=[pl.BlockSpec((None, T, LANES), lambda b: (b, 0, 0))] * len(xs) + [
                  _resident((1, w1.shape[0])),
                  _resident(w1.shape),
                  _resident(wbig.shape),
                  _resident((hid, LANES)),
                  _resident((1, LANES)),
                  _resident((LANES, LANES))],
        out_specs=pl.BlockSpec((None, Hk, T // S, LANES), lambda b: (b, 0, 0, 0)),
        out_shape=jax.ShapeDtypeStruct((B, Hk, T // S, LANES), BF16),
        compiler_params=_params("parallel"),
        name="nsa_compress",
    )(*xs, pe.reshape(1, w1.shape[0]).astype(BF16), w1b, wbig, w2d, gain2, bd)


def _topk_rows_bias(v, n_sel):
    nslab = v.shape[0] // 8
    sub = lax.broadcasted_iota(jnp.int32, (8, v.shape[1]), 0).astype(F32)
    vals = [v[8 * k:8 * k + 8] for k in range(nslab)]
    ids = [sub + 8.0 * k for k in range(nslab)]
    bias = [jnp.full(sub.shape, MASK_BIAS, F32)] * nslab
    for _ in range(n_sel):
        cand = list(zip(vals, ids))
        while len(cand) > 1:
            nxt = []
            for (va, ia), (vb, ib) in zip(cand[0::2], cand[1::2]):
                take_b = vb > va
                nxt.append((jnp.where(take_b, vb, va), jnp.where(take_b, ib, ia)))
            cand = nxt + ([cand[-1]] if len(cand) % 2 else [])
        bv, bi = cand[0]
        for shift in (4, 2, 1):
            rv, ri = pltpu.roll(bv, shift, 0), pltpu.roll(bi, shift, 0)
            take_r = jnp.logical_or(rv > bv, jnp.logical_and(rv == bv, ri < bi))
            bv, bi = jnp.where(take_r, rv, bv), jnp.where(take_r, ri, bi)
        first = [ids[k] == bi for k in range(nslab)]
        vals = [jnp.where(first[k], REMOVED, vals[k]) for k in range(nslab)]
        bias = [jnp.where(first[k], 0.0, bias[k]) for k in range(nslab)]
    return jnp.concatenate(bias, axis=0)


def _nsa_cmp_kernel(q_ref, kc_ref, vc_ref, ov_ref, oc_ref, mneg_ref, imp_ref, *, n_sel, qb):
    Cq = QBLK
    G = q_ref.shape[1] // HEAD_DIM
    ncp = kc_ref.shape[0]
    step = pl.program_id(2)

    def attend(width):
        for i in range(qb):
            n = step * qb + i
            rows = slice(i * Cq, (i + 1) * Cq)
            s_all = _dot_nt(_stack_heads(q_ref[rows, :]), kc_ref[:width, :])
            t = n * Cq + lax.broadcasted_iota(jnp.int32, (Cq, width), 0)
            cmp_end = lax.broadcasted_iota(jnp.int32, (Cq, width), 1) * NSA_CMP_STRIDE + (NSA_CMP_LEN - 1)
            valid = cmp_end <= t
            outs = []
            psum = None
            for g in range(G):
                s = jnp.where(valid, s_all[g * Cq:(g + 1) * Cq], NEG_INF)
                p = jnp.exp(s - jnp.maximum(jnp.max(s, axis=-1, keepdims=True), 0.1 * NEG_INF))
                p = p / jnp.maximum(jnp.sum(p, axis=-1, keepdims=True), SOFTMAX_FLOOR)
                outs.append(_dot(p.astype(BF16), vc_ref[:width, :]))
                psum = p if psum is None else psum + p
            oc_ref[rows, :] = _unstack_heads(jnp.concatenate(outs, axis=0), Cq).astype(BF16)
            hi = psum.astype(BF16)
            lo = (psum - hi.astype(F32)).astype(BF16)
            ov = ov_ref[:width, :]
            imp_ref[i] = _dot(hi, ov) + _dot(lo, ov)

    n_valid = ((step + 1) * qb * Cq - NSA_CMP_LEN) // NSA_CMP_STRIDE + 1
    groups = (n_valid + LANES - 1) // LANES
    widths = list(range(LANES, ncp, LANES)) + [ncp]
    for gi, width in enumerate(widths):
        last = gi == len(widths) - 1
        pl.when(groups >= gi + 1 if last else groups == gi + 1)(functools.partial(attend, width))

    for i in range(qb):
        n = step * qb + i
        rows = slice(i * Cq, (i + 1) * Cq)
        imp = imp_ref[i].T
        nsb = imp.shape[0]
        j = lax.broadcasted_iota(jnp.int32, (nsb, Cq), 0)
        tt = n * Cq + lax.broadcasted_iota(jnp.int32, (nsb, Cq), 1)
        cur = tt // NSA_SEL_LEN
        forced = jnp.logical_or(j == 0, jnp.logical_or(j == cur, j == cur - 1))
        v = jnp.where(forced, NSA_FORCE_SCORE, imp)
        v = jnp.where(j * NSA_SEL_LEN <= tt, v, NEG_INF)
        mneg_ref[rows, :] = _topk_rows_bias(v, n_sel).T.astype(BF16)


def nsa_cmp_topk(q, kcmp, vcmp, ov, B, T, n_sel, qb=2):
    Hk = kcmp.shape[1]
    GW = q.shape[-1] // Hk
    ncp = kcmp.shape[2]
    cmp_spec = pl.BlockSpec((None, None, ncp, LANES), lambda b, h, n: (b, h, 0, 0))
    tq = qb * QBLK
    return pl.pallas_call(
        functools.partial(_nsa_cmp_kernel, n_sel=n_sel, qb=qb),
        grid=(B, Hk, T // tq),
        in_specs=[pl.BlockSpec((None, tq, GW), lambda b, h, n: (b, n, h)),
                  cmp_spec, cmp_spec,
                  pl.BlockSpec((ncp, LANES), lambda b, h, n: (0, 0))],
        out_specs=[pl.BlockSpec((None, tq, GW), lambda b, h, n: (b, n, h)),
                   pl.BlockSpec((None, None, tq, LANES), lambda b, h, n: (b, h, n, 0))],
        out_shape=[jax.ShapeDtypeStruct((B, T, q.shape[-1]), BF16),
                   jax.ShapeDtypeStruct((B, Hk, T, LANES), BF16)],
        scratch_shapes=[pltpu.VMEM((qb, QBLK, LANES), F32)],
        compiler_params=_params("parallel", "parallel", "arbitrary"),
        name="nsa_cmp_topk",
    )(q, kcmp, vcmp, ov)


def _nsa_sel_kernel(q_ref, mneg_ref, k_ref, e_ref, v_ref, o_ref, lhs_ref, m_ref, acc_ref):
    Cq = q_ref.shape[0]
    HP = k_ref.shape[0]
    GW = q_ref.shape[1] // HP
    n = pl.program_id(2)
    rows = lhs_ref.shape[1]
    for h in range(HP):
        lhs_ref[h] = _stack_heads(q_ref[:, h * GW:(h + 1) * GW], extra=mneg_ref[h])
    m_ref[...] = jnp.full(m_ref.shape, NEG_INF, F32)
    acc_ref[...] = jnp.zeros_like(acc_ref)

    def chunk(off, kc, causal):
        for h in range(HP):
            rhs = jnp.concatenate([k_ref[h, pl.ds(off, kc), :], e_ref[pl.ds(off, kc), :]], axis=1)
            s = _dot_nt(lhs_ref[h], rhs)
            if causal:
                t = n * Cq + (lax.broadcasted_iota(jnp.int32, (rows, kc), 0) & (Cq - 1))
                s = jnp.where(off + lax.broadcasted_iota(jnp.int32, (rows, kc), 1) <= t, s, NEG_INF)
            m_prev = m_ref[h]
            m_new = jnp.maximum(m_prev, jnp.max(s, axis=-1, keepdims=True))
            p = jnp.exp((s - jnp.concatenate([m_new] * (kc // LANES), axis=1)).astype(BF16))
            acc_ref[h] = jnp.exp(m_prev - m_new) * acc_ref[h] + _dot(p, v_ref[h, pl.ds(off, kc), :])
            m_ref[h] = m_new

    per_chunk = SEL_CHUNK // Cq
    nmain = n // per_chunk

    def body(c, carry):
        chunk(pl.multiple_of(c * SEL_CHUNK, SEL_CHUNK), SEL_CHUNK, False)
        return carry

    lax.fori_loop(0, nmain, body, 0)
    tail0 = pl.multiple_of(nmain * SEL_CHUNK, SEL_CHUNK)
    chunk(tail0, SEL_TAIL, True)
    for i in range(1, SEL_CHUNK // SEL_TAIL):
        @pl.when(n - nmain * per_chunk >= i * (SEL_TAIL // Cq))
        def _():
            chunk(tail0 + i * SEL_TAIL, SEL_TAIL, True)

    for h in range(HP):
        acc = acc_ref[h]
        o = acc / jnp.maximum(pltpu.roll(acc, HEAD_DIM, 1), SOFTMAX_FLOOR)
        o_ref[:, h * GW:(h + 1) * GW] = _unstack_heads_lo(o, Cq).astype(BF16)


def nsa_selected(q, mneg, k2, e, v2, B, T, heads_per_step=2):
    Hk = k2.shape[1]
    HP = heads_per_step
    GW = q.shape[-1] // Hk
    G = GW // HEAD_DIM
    kv_spec = pl.BlockSpec((None, HP, T, LANES), lambda b, h, n: (b, h, 0, 0))
    tq = SEL_Q
    return pl.pallas_call(
        _nsa_sel_kernel,
        grid=(B, Hk // HP, T // tq),
        in_specs=[pl.BlockSpec((None, tq, HP * GW), lambda b, h, n: (b, n, h)),
                  pl.BlockSpec((None, HP, tq, LANES), lambda b, h, n: (b, h, n, 0)),
                  kv_spec,
                  _resident((T, LANES)),
                  kv_spec],
        out_specs=pl.BlockSpec((None, tq, HP * GW), lambda b, h, n: (b, n, h)),
        out_shape=jax.ShapeDtypeStruct((B, T, q.shape[-1]), BF16),
        scratch_shapes=[pltpu.VMEM((HP, G * tq, 2 * LANES), BF16),
                        pltpu.VMEM((HP, G * tq, LANES), F32),
                        pltpu.VMEM((HP, G * tq, LANES), F32)],
        compiler_params=_params("parallel", "parallel", "arbitrary"),
        name="nsa_selected",
    )(q, mneg, k2, e, v2)


def _nsa_out_kernel(oc_ref, os_ref, ow_ref, gates_ref, e_ref, w_ref, x_ref, gate_ref, o_ref):
    D = oc_ref.shape[1]
    sg = jax.nn.sigmoid(gates_ref[...].astype(F32)).astype(BF16)
    gx = _dot(sg, e_ref[...])
    y = (gx[:, :D] * oc_ref[...].astype(F32) + gx[:, D:2 * D] * os_ref[...].astype(F32)
         + gx[:, 2 * D:] * ow_ref[...].astype(F32))
    o_ref[...] = x_ref[...] + gate_ref[...] * _dot(y.astype(BF16), w_ref[...])


def nsa_out(oc, os_, ow, proj, gate_col_tile, expand, w, x, mod, gate_idx, T, tm=512):
    BT, D = x.shape
    tm = min(tm, T)
    tpb = T // tm
    row = pl.BlockSpec((tm, D), lambda i: (i, 0))
    return pl.pallas_call(
        _nsa_out_kernel,
        grid=(BT // tm,),
        in_specs=[row, row, row,
                  pl.BlockSpec((tm, LANES), lambda i: (i, gate_col_tile)),
                  _resident((LANES, 3 * D)),
                  _resident((D, D)),
                  row,
                  pl.BlockSpec((None, None, 1, D), lambda i: (i // tpb, gate_idx, 0, 0))],
        out_specs=row,
        out_shape=jax.ShapeDtypeStruct((BT, D), F32),
        compiler_params=_params("parallel"),
        name="nsa_out",
    )(oc, os_, ow, proj, expand, w, x, mod)


def retention_mixer(x, mod, norm_g, w_in, w_out, B, T):
    H = RET_HEADS
    D = x.shape[1]
    dk = D // H
    nqk = 2 * H * dk
    w_bf = w_in.astype(BF16)
    w_qk = w_bf[:, :nqk].reshape(D, 2 * H, dk // 2, 2).swapaxes(2, 3).reshape(D, nqk)
    w_perm = jnp.concatenate([w_qk, w_bf[:, nqk:]], axis=1)
    cos, sin = retention_tables(T, dk)
    scales = (1.0,) * H + (dk ** -0.5,) * H
    proj = normmod_matmul(x, norm_g, mod, 0, w_perm, T, rot=(cos, sin, scales))
    y = retention_core(proj.reshape(B, T, -1), B, T)
    return outproj_residual(y.reshape(B * T, -1), w_out.astype(BF16), x, mod, 2, T)


def swa_mixer(x, mod, norm_g, w_in, q_norm_g, k_norm_g, sinks, w_out, B, T):
    D = x.shape[1]
    Hk = SWA_KV_HEADS
    G = D // HEAD_DIM // Hk
    proj = normmod_matmul(x, norm_g, mod, 0, w_in.astype(BF16), T, tn=256)
    tables = rope_tables(T)
    nq = D // LANES
    nkv = Hk * HEAD_DIM // LANES
    plan = [(j, 0, True, Q_SCALE, ("flat", 0, j)) for j in range(nq)]
    plan += [(nq + j, 1, True, 1.0, ("heads", 1, 2 * j, False)) for j in range(nkv)]
    plan += [(nq + nkv + j, None, False, 1.0, ("heads", 2, 2 * j, True)) for j in range(nkv)]
    q, k2, v2 = head_prep(proj, jnp.stack([q_norm_g, k_norm_g]), tables, plan, [(nq, BF16)], [Hk, Hk], B, T)
    o = banded_attention(q.reshape(B, T, D), k2, v2, sinks, B, T, G=G, nprev=SWA_WINDOW // QBLK, window=SWA_WINDOW,
                         unroll=2)
    return outproj_residual(o.reshape(B * T, D), w_out.astype(BF16), x, mod, 2, T)


def nsa_mixer(x, mod, norm_g, w_in, q_norm_g, k_norm_g, cmp_pe, cmp_w1, cmp_w2, w_out, B, T):
    D = x.shape[1]
    Hk = NSA_KV_HEADS
    H = D // HEAD_DIM
    G = H // Hk
    kvw = Hk * HEAD_DIM
    n_main = D + 6 * kvw
    n_pad = -(-w_in.shape[1] // LANES) * LANES
    w_pad = jnp.pad(w_in, ((0, 0), (0, n_pad - w_in.shape[1]))).astype(BF16)
    proj = normmod_matmul(x, norm_g, mod, 0, w_pad, T, tn=LANES * 3)
    tables = rope_tables(T)
    nq = D // LANES
    nkv = kvw // LANES
    col = lambda i: nq + i * nkv
    plan = [(j, 0, True, Q_SCALE, ("flat", 0, j)) for j in range(nq)]
    for j in range(nkv):
        plan += [(col(0) + j, None, False, 1.0, ("flat", 1 + j, 0)),
                 (col(1) + j, None, False, 1.0, ("flat", 1 + nkv + j, 0)),
                 (col(2) + j, 1, True, 1.0, ("heads", 1 + 2 * nkv, 2 * j, False)),
                 (col(3) + j, None, False, 1.0, ("heads", 2 + 2 * nkv, 2 * j, True)),
                 (col(4) + j, 2, True, 1.0, ("heads", 3 + 2 * nkv, 2 * j, False)),
                 (col(5) + j, None, False, 1.0, ("heads", 4 + 2 * nkv, 2 * j, True))]
    gains = jnp.stack([q_norm_g, k_norm_g[1], k_norm_g[2]])
    outs = head_prep(proj, gains, tables, plan, [(nq, BF16)] + [(1, F32)] * (2 * nkv), [Hk] * 4, B, T)
    q = outs[0].reshape(B, T, D)
    kc = [a.reshape(B, T, LANES) for a in outs[1:1 + nkv]]
    vc = [a.reshape(B, T, LANES) for a in outs[1 + nkv:1 + 2 * nkv]]
    ks2, vs2, kw2, vw2 = outs[1 + 2 * nkv:]

    S = NSA_CMP_STRIDE
    kcmp = nsa_compress(kc, cmp_pe[0], cmp_w1[0], cmp_w2[0], k_norm_g[0], Hk, norm=True)
    vcmp = nsa_compress(vc, cmp_pe[1], cmp_w1[1], cmp_w2[1], k_norm_g[0], Hk, norm=False)

    ncp = T // S
    ns = T // NSA_SEL_LEN
    cs = np.arange(ncp)[:, None] * S
    js = np.arange(LANES)[None, :]
    overlap = ((cs < js * NSA_SEL_LEN + NSA_SEL_LEN) & (cs + NSA_CMP_LEN > js * NSA_SEL_LEN)
               & (js < ns) & (np.arange(ncp)[:, None] < ncp - 1))
    ov = jnp.asarray(overlap, BF16)
    oc, mneg = nsa_cmp_topk(q, kcmp, vcmp, ov, B, T, min(NSA_N_SEL, ns))

    blk_onehot = jnp.asarray(np.arange(T)[:, None] // NSA_SEL_LEN == js, BF16)
    os_ = nsa_selected(q, mneg, ks2, blk_onehot, vs2, B, T)
    ow = banded_attention(q, kw2, vw2, None, B, T, G=G, nprev=NSA_WINDOW // QBLK, window=NSA_WINDOW, unroll=True)

    rows = np.arange(LANES)[:, None]
    cols = np.arange(3 * D)[None, :]
    expand = jnp.asarray(rows == 3 * ((cols % D) // HEAD_DIM) + cols // D, BF16)
    flat = lambda a: a.reshape(B * T, D)
    return nsa_out(flat(oc), flat(os_), flat(ow), proj, n_main // LANES, expand, w_out.astype(BF16), x, mod, 2, T)


def kernel(x, c, l0_mod_w, l0_mod_b, l0_norm1_g, l0_ret_w_in, l0_ret_w_out, l0_norm2_g, l0_ffn_w_in, l0_ffn_w_out, l1_mod_w, l1_mod_b, l1_norm1_g, l1_swa_w_in, l1_swa_q_norm_g, l1_swa_k_norm_g, l1_swa_sinks, l1_swa_w_out, l1_norm2_g, l1_ffn_w_in, l1_ffn_w_out, l2_mod_w, l2_mod_b, l2_norm1_g, l2_nsa_w_in, l2_nsa_q_norm_g, l2_nsa_k_norm_g, l2_nsa_cmp_pe, l2_nsa_cmp_w1, l2_nsa_cmp_w2, l2_nsa_w_out, l2_norm2_g, l2_ffn_w_in, l2_ffn_w_out, l3_mod_w, l3_mod_b, l3_norm1_g, l3_ret_w_in, l3_ret_w_out, l3_norm2_g, l3_ffn_w_in, l3_ffn_w_out):
    B, T, D = x.shape
    layers = [
        (l0_mod_w, l0_mod_b, l0_norm1_g, retention_mixer, (l0_ret_w_in, l0_ret_w_out), l0_norm2_g, l0_ffn_w_in, l0_ffn_w_out),
        (l1_mod_w, l1_mod_b, l1_norm1_g, swa_mixer,
         (l1_swa_w_in, l1_swa_q_norm_g, l1_swa_k_norm_g, l1_swa_sinks, l1_swa_w_out), l1_norm2_g, l1_ffn_w_in, l1_ffn_w_out),
        (l2_mod_w, l2_mod_b, l2_norm1_g, nsa_mixer,
         (l2_nsa_w_in, l2_nsa_q_norm_g, l2_nsa_k_norm_g, l2_nsa_cmp_pe, l2_nsa_cmp_w1, l2_nsa_cmp_w2, l2_nsa_w_out),
         l2_norm2_g, l2_ffn_w_in, l2_ffn_w_out),
        (l3_mod_w, l3_mod_b, l3_norm1_g, retention_mixer, (l3_ret_w_in, l3_ret_w_out), l3_norm2_g, l3_ffn_w_in, l3_ffn_w_out),
    ]
    h = x.reshape(B * T, D)
    for mod_w, mod_b, norm1_g, mixer, mixer_params, norm2_g, ffn_w_in, ffn_w_out in layers:
        mod = adaln_mod(c, mod_w, mod_b)
        h = mixer(h, mod, norm1_g, *mixer_params, B, T)
        h = ffn_block(h, norm2_g, mod, ffn_w_in.astype(BF16), ffn_w_out.astype(BF16), T)
    return h.reshape(B, T, D)
```

```python
import functools

import jax
import jax.numpy as jnp
import numpy as np
from jax import lax
from jax.experimental import pallas as pl
from jax.experimental.pallas import tpu as pltpu

F32 = jnp.float32
BF16 = jnp.bfloat16

NORM_EPS = 1e-6
NEG_INF = -1e30
SOFTMAX_FLOOR = 1e-30
REMOVED = -3e38
MASK_BIAS = -(2.0 ** 30)

V7X_VMEM_LIMIT_BYTES = 56 * 2 ** 20
LANES = 128
QBLK = 128

ROPE_THETA = 500000.0
RET_ROT_BASE = 10000.0
RET_HEADS = 4
RET_CHUNK = 256
SWA_KV_HEADS = 2
SWA_WINDOW = 128
NSA_KV_HEADS = 4
NSA_CMP_LEN = 32
NSA_CMP_STRIDE = 16
NSA_SEL_LEN = 64
NSA_N_SEL = 16
NSA_WINDOW = 512
NSA_FORCE_SCORE = 1e4
HEAD_DIM = 64
SEL_CHUNK = 1024
SEL_TAIL = 512
SEL_Q = 512
Q_SCALE = HEAD_DIM ** -0.5

NT_DIMS = (((1,), (1,)), ((), ()))


def _params(*sem):
    return pltpu.CompilerParams(dimension_semantics=sem, vmem_limit_bytes=V7X_VMEM_LIMIT_BYTES)


def _dot(a, b):
    return jnp.dot(a, b, preferred_element_type=F32)


def _dot_nt(a, b):
    return lax.dot_general(a, b, NT_DIMS, preferred_element_type=F32)


def _mod_kernel(c_ref, w_ref, b_ref, o_ref):
    c = c_ref[...]
    s = c * jax.nn.sigmoid(c)
    nb = s.shape[0]
    s8 = jnp.concatenate([s, jnp.zeros((8 - nb, s.shape[1]), F32)], axis=0)
    r = jnp.dot(s8, w_ref[...], preferred_element_type=F32, precision=lax.Precision.HIGHEST)
    o_ref[...] = r[:nb] + b_ref[...]


def adaln_mod(c, mod_w, mod_b):
    B, D = c.shape
    N = mod_w.shape[1]
    tn = N // 4
    out = pl.pallas_call(
        _mod_kernel,
        grid=(N // tn,),
        in_specs=[pl.BlockSpec((B, D), lambda j: (0, 0)),
                  pl.BlockSpec((D, tn), lambda j: (0, j)),
                  pl.BlockSpec((1, tn), lambda j: (0, j))],
        out_specs=pl.BlockSpec((B, tn), lambda j: (0, j)),
        out_shape=jax.ShapeDtypeStruct((B, N), F32),
        compiler_params=_params("arbitrary"),
        name="adaln_mod",
    )(c, mod_w, mod_b.reshape(1, N))
    return out.reshape(B, 6, 1, D)


def _normmod(x, g, scale, shift):
    ms = jnp.mean(x * x, axis=-1, keepdims=True)
    y = x * lax.rsqrt(ms + NORM_EPS) * g
    return y * (1.0 + scale) + shift


def _resident(shape):
    return pl.BlockSpec(shape, lambda *_: (0,) * len(shape), pipeline_mode=pl.Buffered(1))


def _normmod_matmul_kernel(x_ref, g_ref, scale_ref, shift_ref, w_ref, *rest, tn, rot_scales):
    h = _normmod(x_ref[...], g_ref[...], scale_ref[...], shift_ref[...]).astype(BF16)
    o_ref = rest[-1]
    if rot_scales:
        cos, sin = rest[0][...], rest[1][...]
        hw = 2 * cos.shape[1]
    for c in range(w_ref.shape[1] // tn):
        cols = slice(c * tn, (c + 1) * tn)
        acc = _dot(h, w_ref[:, cols])
        if c * tn < len(rot_scales) * (hw if rot_scales else 0):
            parts = []
            for hd in range(tn // hw):
                x1 = acc[:, hd * hw:hd * hw + hw // 2]
                x2 = acc[:, hd * hw + hw // 2:(hd + 1) * hw]
                sc = rot_scales[c * tn // hw + hd]
                parts += [(x1 * cos - x2 * sin) * sc, (x2 * cos + x1 * sin) * sc]
            acc = jnp.concatenate(parts, axis=1)
        o_ref[:, cols] = acc.astype(o_ref.dtype)


def normmod_matmul(x, g, mod, shift_idx, w, T, tm=512, tn=512, rot=None):
    BT, D = x.shape
    N = w.shape[1]
    tm = min(tm, T)
    tn = tn if N % tn == 0 else LANES
    tpb = T // tm
    in_specs = [pl.BlockSpec((tm, D), lambda i: (i, 0)),
                _resident((1, D)),
                pl.BlockSpec((None, None, 1, D), lambda i: (i // tpb, shift_idx + 1, 0, 0)),
                pl.BlockSpec((None, None, 1, D), lambda i: (i // tpb, shift_idx, 0, 0)),
                _resident((D, N))]
    args = [x, g.reshape(1, D), mod, mod, w]
    rot_scales = ()
    if rot is not None:
        cos, sin, rot_scales = rot
        in_specs += [pl.BlockSpec((tm, cos.shape[1]), lambda i: (i % tpb, 0))] * 2
        args += [cos, sin]
    return pl.pallas_call(
        functools.partial(_normmod_matmul_kernel, tn=tn, rot_scales=tuple(rot_scales)),
        grid=(BT // tm,),
        in_specs=in_specs,
        out_specs=pl.BlockSpec((tm, N), lambda i: (i, 0)),
        out_shape=jax.ShapeDtypeStruct((BT, N), BF16),
        compiler_params=_params("parallel"),
        name="normmod_matmul",
    )(*args)


def _ffn_kernel(x_ref, g_ref, scale_ref, shift_ref, gate_ref, wi_ref, wo_ref, *rest, tc):
    o_ref = rest[-1]
    x = x_ref[...]
    if len(rest) > 1:
        y_ref, wm_ref, gate1_ref = rest[:3]
        x = x + gate1_ref[...] * _dot(y_ref[...], wm_ref[...])
    h = _normmod(x, g_ref[...], scale_ref[...], shift_ref[...]).astype(BF16)
    F = wo_ref.shape[0]
    acc = None
    for c in range(F // tc):
        a = _dot(h, wi_ref[:, c * tc:(c + 1) * tc])
        b = _dot(h, wi_ref[:, F + c * tc:F + (c + 1) * tc])
        u = (a * jax.nn.sigmoid(a) * b).astype(BF16)
        part = _dot(u, wo_ref[c * tc:(c + 1) * tc, :])
        acc = part if acc is None else acc + part
    o_ref[...] = x + gate_ref[...] * acc


def ffn_block(x, g, mod, w_in, w_out, T, mix=None, tm=512, tc=256):
    BT, D = x.shape
    F = w_out.shape[0]
    tm = min(tm, T)
    tpb = T // tm
    modspec = lambda idx: pl.BlockSpec((None, None, 1, D), lambda i: (i // tpb, idx, 0, 0))
    in_specs = [pl.BlockSpec((tm, D), lambda i: (i, 0)),
                _resident((1, D)),
                modspec(4), modspec(3), modspec(5),
                _resident((D, 2 * F)),
                _resident((F, D))]
    args = [x, g.reshape(1, D), mod, mod, mod, w_in, w_out]
    if mix is not None:
        y, w_mix = mix
        in_specs += [pl.BlockSpec((tm, y.shape[1]), lambda i: (i, 0)), _resident(w_mix.shape), modspec(2)]
        args += [y, w_mix, mod]
    return pl.pallas_call(
        functools.partial(_ffn_kernel, tc=tc),
        grid=(BT // tm,),
        in_specs=in_specs,
        out_specs=pl.BlockSpec((tm, D), lambda i: (i, 0)),
        out_shape=jax.ShapeDtypeStruct((BT, D), F32),
        compiler_params=_params("parallel"),
        name="ffn_block",
    )(*args)


def _ret_core_kernel(q_ref, k_ref, v_ref, g_ref, o_ref, state_ref, *, chunks):
    C = RET_CHUNK

    @pl.when(pl.program_id(2) == 0)
    def _():
        state_ref[...] = jnp.zeros_like(state_ref)

    head = pl.program_id(1)
    log_gammas = [float(np.log(1.0 - 2.0 ** (-5.0 - i))) for i in range(RET_HEADS)]
    lg = jnp.float32(log_gammas[-1])
    for i in range(RET_HEADS - 1):
        lg = jnp.where(head == i, log_gammas[i], lg)
    rel = (lax.broadcasted_iota(jnp.int32, (C, C), 0) - lax.broadcasted_iota(jnp.int32, (C, C), 1)).astype(F32)
    intra = jnp.where(rel >= 0, jnp.exp(lg * jnp.maximum(rel, 0.0)), 0.0)
    idx = lax.broadcasted_iota(jnp.int32, (C, 1), 0).astype(F32)
    q_decay = jnp.exp(lg * (idx + 1.0))
    k_decay = jnp.exp(lg * (C - 1.0 - idx))
    state_decay = jnp.exp(lg * float(C))

    for c in range(chunks):
        sl = slice(c * C, (c + 1) * C)
        qb = q_ref[sl, :]
        kb = k_ref[sl, :]
        v = v_ref[sl, :]
        scores = _dot_nt(qb, kb) * intra
        inner = _dot(scores.astype(BF16), v)
        state = state_ref[...]
        cross = _dot(qb, state.astype(BF16)) * q_decay
        kd_t = (kb.astype(F32) * k_decay).T.astype(BF16)
        state_ref[...] = state * state_decay + _dot(kd_t, v)
        out = inner + cross
        ms = jnp.mean(out * out, axis=-1, keepdims=True)
        y = out * lax.rsqrt(ms + NORM_EPS)
        g = g_ref[sl, :]
        o_ref[sl, :] = y.astype(BF16) * (g * jax.nn.sigmoid(g))


def retention_core(proj, B, T, tb=1024):
    H = RET_HEADS
    DK = proj.shape[-1] // (6 * H)
    DV = 2 * DK
    tb = min(tb, T)
    kern = functools.partial(_ret_core_kernel, chunks=tb // RET_CHUNK)
    return pl.pallas_call(
        kern,
        grid=(B, H, T // tb),
        in_specs=[pl.BlockSpec((None, tb, DK), lambda b, h, t: (b, t, h)),
                  pl.BlockSpec((None, tb, DK), lambda b, h, t: (b, t, H + h)),
                  pl.BlockSpec((None, tb, DV), lambda b, h, t: (b, t, H + h)),
                  pl.BlockSpec((None, tb, DV), lambda b, h, t: (b, t, 2 * H + h))],
        out_specs=pl.BlockSpec((None, tb, DV), lambda b, h, t: (b, t, h)),
        out_shape=jax.ShapeDtypeStruct((B, T, H * DV), BF16),
        scratch_shapes=[pltpu.VMEM((DK, DV), F32)],
        compiler_params=_params("parallel", "parallel", "arbitrary"),
        name="retention_core",
    )(proj, proj, proj, proj)


def _head_prep_kernel(x_ref, gains_ref, c_ref, s1_ref, s2_ref, bd_ref, *out_refs, plan):
    c, s1, s2 = c_ref[...], s1_ref[...], s2_ref[...]
    bd = bd_ref[...]
    first_half = lax.broadcasted_iota(jnp.int32, c.shape, 1) < HEAD_DIM
    for src, gain_row, rope, scale, dst in plan:
        x = x_ref[:, src * LANES:(src + 1) * LANES].astype(F32)
        if gain_row is not None:
            ss = _dot((x * x).astype(BF16), bd)
            x = x * lax.rsqrt(ss * (1.0 / HEAD_DIM) + NORM_EPS) * gains_ref[gain_row:gain_row + 1, :]
        if rope:
            x = x * c + pltpu.roll(x, LANES - 8, 1) * s1 + pltpu.roll(x, 8, 1) * s2
        if scale != 1.0:
            x = x * scale
        if dst[0] == "flat":
            out = out_refs[dst[1]]
            out[:, dst[2] * LANES:(dst[2] + 1) * LANES] = x.astype(out.dtype)
        else:
            _, out, head0, ones = dst
            swapped = pltpu.roll(x, HEAD_DIM, 1)
            a = jnp.where(first_half, x, 1.0 if ones else swapped)
            b = jnp.where(first_half, swapped, 1.0 if ones else x)
            out_refs[out][head0] = a.astype(BF16)
            out_refs[out][head0 + 1] = b.astype(BF16)


def head_prep(proj, gains, tables, plan, flat_outs, head_outs, B, T, tm=512):
    BT, N = proj.shape
    tm = min(tm, T)
    tpb = T // tm
    bd = jnp.asarray(np.kron(np.eye(2), np.ones((HEAD_DIM, HEAD_DIM))), BF16)
    gains2 = jnp.concatenate([gains, gains], axis=1).astype(F32)
    tab = pl.BlockSpec((tm, LANES), lambda i: (i % tpb, 0))
    out_specs, out_shape = [], []
    for tiles, dtype in flat_outs:
        out_specs.append(pl.BlockSpec((tm, tiles * LANES), lambda i: (i, 0)))
        out_shape.append(jax.ShapeDtypeStruct((BT, tiles * LANES), dtype))
    for nh in head_outs:
        out_specs.append(pl.BlockSpec((None, nh, tm, LANES), lambda i: (i // tpb, 0, i % tpb, 0)))
        out_shape.append(jax.ShapeDtypeStruct((B, nh, T, LANES), BF16))
    return pl.pallas_call(
        functools.partial(_head_prep_kernel, plan=tuple(plan)),
        grid=(BT // tm,),
        in_specs=[pl.BlockSpec((tm, N), lambda i: (i, 0)),
                  _resident(gains2.shape),
                  tab, tab, tab,
                  _resident((LANES, LANES))],
        out_specs=out_specs,
        out_shape=out_shape,
        compiler_params=_params("parallel"),
        name="head_prep",
    )(proj, gains2, *tables, bd)


def rope_tables(T):
    half = HEAD_DIM // 4 // 2
    inv = ROPE_THETA ** (-jnp.arange(half, dtype=F32) / half)
    lane = np.arange(LANES) % HEAD_DIM
    lo, hi = lane < half, (lane >= half) & (lane < 2 * half)
    inv_lane = jnp.where(lo | hi, inv[lane % half], 0.0)
    ang = jnp.arange(T).astype(F32)[:, None] * inv_lane[None, :]
    sin = jnp.sin(ang)
    return jnp.cos(ang), jnp.where(lo, -sin, 0.0), jnp.where(hi, sin, 0.0)


def retention_tables(T, dk):
    inv = 1.0 / (RET_ROT_BASE ** jnp.linspace(0.0, 1.0, dk // 2, dtype=F32))
    ang = jnp.arange(T).astype(F32)[:, None] * inv[None, :]
    return jnp.cos(ang), jnp.sin(ang)


def _stack_heads(q, extra=None):
    first_half = lax.broadcasted_iota(jnp.int32, (q.shape[0], LANES), 1) < HEAD_DIM
    zero = jnp.zeros((q.shape[0], LANES), q.dtype)
    rows = []
    for p in range(q.shape[1] // LANES):
        qp = q[:, p * LANES:(p + 1) * LANES]
        for part in (jnp.where(first_half, qp, zero), jnp.where(first_half, zero, qp)):
            rows.append(part if extra is None else jnp.concatenate([part, extra], axis=1))
    return jnp.concatenate(rows, axis=0)


def _unstack_heads(o, cq):
    first_half = lax.broadcasted_iota(jnp.int32, (cq, LANES), 1) < HEAD_DIM
    G = o.shape[0] // cq
    pairs = [jnp.where(first_half, o[(2 * p) * cq:(2 * p + 1) * cq], o[(2 * p + 1) * cq:(2 * p + 2) * cq])
             for p in range(G // 2)]
    return jnp.concatenate(pairs, axis=1)


def _unstack_heads_lo(o, cq):
    first_half = lax.broadcasted_iota(jnp.int32, (cq, LANES), 1) < HEAD_DIM
    G = o.shape[0] // cq
    pairs = [jnp.where(first_half, o[(2 * p) * cq:(2 * p + 1) * cq],
                       pltpu.roll(o[(2 * p + 1) * cq:(2 * p + 2) * cq], HEAD_DIM, 1))
             for p in range(G // 2)]
    return jnp.concatenate(pairs, axis=1)


def _banded_kernel(*refs, G, nprev, window, use_sink, qb, unroll):
    if use_sink:
        sink_ref, q_ref, k_ref, v_ref, o_ref = refs
    else:
        q_ref, k_ref, v_ref, o_ref = refs
    Cq = QBLK
    nk = (nprev + 1) * Cq
    hk = pl.program_id(1)

    def block(i, carry):
        n = pl.program_id(2) * qb + i
        r0 = pl.multiple_of(i * Cq, Cq)
        start = pl.multiple_of(jnp.maximum(n - nprev, 0) * Cq, Cq)
        kwin = k_ref[pl.ds(start, nk), :]
        vwin = v_ref[pl.ds(start, nk), :]
        lhs = _stack_heads(q_ref[pl.ds(r0, Cq), :])
        s_all = _dot_nt(lhs, kwin)
        t = n * Cq + lax.broadcasted_iota(jnp.int32, (Cq, nk), 0)
        rel = t - (start + lax.broadcasted_iota(jnp.int32, (Cq, nk), 1))
        valid = jnp.logical_and(rel >= 0, rel < window)
        outs = []
        for g in range(G):
            s = jnp.where(valid, s_all[g * Cq:(g + 1) * Cq], NEG_INF)
            m = jnp.max(s, axis=-1, keepdims=True)
            if use_sink:
                sink = sink_ref[hk * G + g]
                m = jnp.maximum(m, sink)
            p = jnp.exp((s - m).astype(BF16))
            acc = _dot(p, vwin)
            denom = pltpu.roll(acc, HEAD_DIM, 1)
            if use_sink:
                denom = denom + jnp.exp(sink - m)
            else:
                denom = jnp.maximum(denom, SOFTMAX_FLOOR)
            outs.append(acc / denom)
        o_ref[pl.ds(r0, Cq), :] = _unstack_heads_lo(jnp.concatenate(outs, axis=0), Cq).astype(BF16)
        return carry

    lax.fori_loop(0, qb, block, 0, unroll=unroll)


def banded_attention(q, k2, v2, sinks, B, T, *, G, nprev, window, unroll, qb=4):
    Hk = k2.shape[1]
    qb = min(qb, T // QBLK)
    use_sink = sinks is not None
    kern = functools.partial(_banded_kernel, G=G, nprev=nprev, window=window, use_sink=use_sink, qb=qb, unroll=unroll)
    in_specs = [pl.BlockSpec((None, qb * QBLK, G * HEAD_DIM), lambda b, h, n: (b, n, h)),
                pl.BlockSpec((None, None, T, LANES), lambda b, h, n: (b, h, 0, 0)),
                pl.BlockSpec((None, None, T, LANES), lambda b, h, n: (b, h, 0, 0))]
    args = [q, k2, v2]
    if use_sink:
        in_specs = [pl.BlockSpec(memory_space=pltpu.SMEM)] + in_specs
        args = [sinks.astype(F32)] + args
    return pl.pallas_call(
        kern,
        grid=(B, Hk, T // (qb * QBLK)),
        in_specs=in_specs,
        out_specs=pl.BlockSpec((None, qb * QBLK, G * HEAD_DIM), lambda b, h, n: (b, n, h)),
        out_shape=jax.ShapeDtypeStruct((B, T, Hk * G * HEAD_DIM), BF16),
        compiler_params=_params("parallel", "parallel", "arbitrary"),
        name="banded_attention",
    )(*args)


def _nsa_compress_kernel(*refs, norm, npairs):
    x_refs = refs[:npairs]
    pe_ref, w1_ref, wbig_ref, w2_ref, gain_ref, bd_ref, o_ref = refs[npairs:]
    S = NSA_CMP_STRIDE
    ng = x_refs[0].shape[0] // S
    nh = o_ref.shape[0]
    u = [None, None]
    for r in range(S):
        xr = jnp.concatenate([x[pl.ds(r, ng, stride=S), :] for x in x_refs], axis=1).astype(BF16)
        for a in range(2):
            part = _dot(xr, wbig_ref[a, r])
            u[a] = part if u[a] is None else u[a] + part
    pe8 = jnp.broadcast_to(pe_ref[...], (8, pe_ref.shape[1]))
    const = _dot(pe8, w1_ref[...])[:1]
    pre = u[0] + pltpu.roll(u[1], ng - 1, 0) + jnp.concatenate([const] * nh, axis=1)
    hid = (pre * jax.nn.sigmoid(pre)).astype(BF16)
    hw = hid.shape[1] // nh
    for h in range(nh):
        y = _dot(hid[:, h * hw:(h + 1) * hw], w2_ref[...])
        if norm:
            ss = _dot((y * y).astype(BF16), bd_ref[...])
            y = y * lax.rsqrt(ss * (1.0 / HEAD_DIM) + NORM_EPS) * gain_ref[...]
        o_ref[h] = y.astype(BF16)


def nsa_compress(xs, pe, w1, w2, gain, Hk, *, norm):
    B, T, _ = xs[0].shape
    W = len(xs) * LANES
    S = NSA_CMP_STRIDE
    hid = w1.shape[1]
    bd = jnp.asarray(np.kron(np.eye(2), np.ones((HEAD_DIM, HEAD_DIM))), BF16)
    gain2 = jnp.concatenate([gain, gain]).reshape(1, LANES).astype(F32)
    w2d = jnp.concatenate([w2, w2], axis=1).astype(BF16)
    w1b = w1.astype(BF16)
    wbig = jnp.einsum("ardc,kh->arkdhc", w1b.reshape(2, S, HEAD_DIM, hid), jnp.eye(Hk, dtype=BF16))
    wbig = wbig.reshape(2, S, W, Hk * hid)
    return pl.pallas_call(
        functools.partial(_nsa_compress_kernel, norm=norm, npairs=len(xs)),
        grid=(B,),
        in_specs=[pl.BlockSpec((None, T, LANES), lambda b: (b, 0, 0))] * len(xs) + [
                  _resident((1, w1.shape[0])),
                  _resident(w1.shape),
                  _resident(wbig.shape),
                  _resident((hid, LANES)),
                  _resident((1, LANES)),
                  _resident((LANES, LANES))],
        out_specs=pl.BlockSpec((None, Hk, T // S, LANES), lambda b: (b, 0, 0, 0)),
        out_shape=jax.ShapeDtypeStruct((B, Hk, T // S, LANES), BF16),
        compiler_params=_params("parallel"),
        name="nsa_compress",
    )(*xs, pe.reshape(1, w1.shape[0]).astype(BF16), w1b, wbig, w2d, gain2, bd)


def _topk_rows_bias(v, n_sel):
    nslab = v.shape[0] // 8
    sub = lax.broadcasted_iota(jnp.int32, (8, v.shape[1]), 0).astype(F32)
    vals = [v[8 * k:8 * k + 8] for k in range(nslab)]
    ids = [sub + 8.0 * k for k in range(nslab)]
    bias = [jnp.full(sub.shape, MASK_BIAS, F32)] * nslab
    for _ in range(n_sel):
        cand = list(zip(vals, ids))
        while len(cand) > 1:
            nxt = []
            for (va, ia), (vb, ib) in zip(cand[0::2], cand[1::2]):
                take_b = vb > va
                nxt.append((jnp.where(take_b, vb, va), jnp.where(take_b, ib, ia)))
            cand = nxt + ([cand[-1]] if len(cand) % 2 else [])
        bv, bi = cand[0]
        for shift in (4, 2, 1):
            rv, ri = pltpu.roll(bv, shift, 0), pltpu.roll(bi, shift, 0)
            take_r = jnp.logical_or(rv > bv, jnp.logical_and(rv == bv, ri < bi))
            bv, bi = jnp.where(take_r, rv, bv), jnp.where(take_r, ri, bi)
        first = [ids[k] == bi for k in range(nslab)]
        vals = [jnp.where(first[k], REMOVED, vals[k]) for k in range(nslab)]
        bias = [jnp.where(first[k], 0.0, bias[k]) for k in range(nslab)]
    return jnp.concatenate(bias, axis=0)


def _nsa_cmp_kernel(q_ref, kc_ref, vc_ref, ov_ref, oc_ref, mneg_ref, imp_ref, *, n_sel, qb):
    Cq = QBLK
    G = q_ref.shape[1] // HEAD_DIM
    ncp = kc_ref.shape[0]
    step = pl.program_id(2)

    def attend(width):
        for i in range(qb):
            n = step * qb + i
            rows = slice(i * Cq, (i + 1) * Cq)
            s_all = _dot_nt(_stack_heads(q_ref[rows, :]), kc_ref[:width, :])
            t = n * Cq + lax.broadcasted_iota(jnp.int32, (Cq, width), 0)
            cmp_end = lax.broadcasted_iota(jnp.int32, (Cq, width), 1) * NSA_CMP_STRIDE + (NSA_CMP_LEN - 1)
            valid = cmp_end <= t
            outs = []
            psum = None
            for g in range(G):
                s = jnp.where(valid, s_all[g * Cq:(g + 1) * Cq], NEG_INF)
                p = jnp.exp(s - jnp.maximum(jnp.max(s, axis=-1, keepdims=True), 0.1 * NEG_INF))
                p = p / jnp.maximum(jnp.sum(p, axis=-1, keepdims=True), SOFTMAX_FLOOR)
                outs.append(_dot(p.astype(BF16), vc_ref[:width, :]))
                psum = p if psum is None else psum + p
            oc_ref[rows, :] = _unstack_heads(jnp.concatenate(outs, axis=0), Cq).astype(BF16)
            hi = psum.astype(BF16)
            lo = (psum - hi.astype(F32)).astype(BF16)
            ov = ov_ref[:width, :]
            imp_ref[i] = _dot(hi, ov) + _dot(lo, ov)

    n_valid = ((step + 1) * qb * Cq - NSA_CMP_LEN) // NSA_CMP_STRIDE + 1
    groups = (n_valid + LANES - 1) // LANES
    widths = list(range(LANES, ncp, LANES)) + [ncp]
    for gi, width in enumerate(widths):
        last = gi == len(widths) - 1
        pl.when(groups >= gi + 1 if last else groups == gi + 1)(functools.partial(attend, width))

    for i in range(qb):
        n = step * qb + i
        rows = slice(i * Cq, (i + 1) * Cq)
        imp = imp_ref[i].T
        nsb = imp.shape[0]
        j = lax.broadcasted_iota(jnp.int32, (nsb, Cq), 0)
        tt = n * Cq + lax.broadcasted_iota(jnp.int32, (nsb, Cq), 1)
        cur = tt // NSA_SEL_LEN
        forced = jnp.logical_or(j == 0, jnp.logical_or(j == cur, j == cur - 1))
        v = jnp.where(forced, NSA_FORCE_SCORE, imp)
        v = jnp.where(j * NSA_SEL_LEN <= tt, v, NEG_INF)
        mneg_ref[rows, :] = _topk_rows_bias(v, n_sel).T.astype(BF16)


def nsa_cmp_topk(q, kcmp, vcmp, ov, B, T, n_sel, qb=2):
    Hk = kcmp.shape[1]
    GW = q.shape[-1] // Hk
    ncp = kcmp.shape[2]
    cmp_spec = pl.BlockSpec((None, None, ncp, LANES), lambda b, h, n: (b, h, 0, 0))
    tq = qb * QBLK
    return pl.pallas_call(
        functools.partial(_nsa_cmp_kernel, n_sel=n_sel, qb=qb),
        grid=(B, Hk, T // tq),
        in_specs=[pl.BlockSpec((None, tq, GW), lambda b, h, n: (b, n, h)),
                  cmp_spec, cmp_spec,
                  pl.BlockSpec((ncp, LANES), lambda b, h, n: (0, 0))],
        out_specs=[pl.BlockSpec((None, tq, GW), lambda b, h, n: (b, n, h)),
                   pl.BlockSpec((None, None, tq, LANES), lambda b, h, n: (b, h, n, 0))],
        out_shape=[jax.ShapeDtypeStruct((B, T, q.shape[-1]), BF16),
                   jax.ShapeDtypeStruct((B, Hk, T, LANES), BF16)],
        scratch_shapes=[pltpu.VMEM((qb, QBLK, LANES), F32)],
        compiler_params=_params("parallel", "parallel", "arbitrary"),
        name="nsa_cmp_topk",
    )(q, kcmp, vcmp, ov)


def _nsa_sel_kernel(q_ref, mneg_ref, k_ref, e_ref, v_ref, o_ref, lhs_ref, m_ref, acc_ref):
    Cq = q_ref.shape[0]
    HP = k_ref.shape[0]
    GW = q_ref.shape[1] // HP
    n = pl.program_id(2)
    rows = lhs_ref.shape[1]
    for h in range(HP):
        lhs_ref[h] = _stack_heads(q_ref[:, h * GW:(h + 1) * GW], extra=mneg_ref[h])
    m_ref[...] = jnp.full(m_ref.shape, NEG_INF, F32)
    acc_ref[...] = jnp.zeros_like(acc_ref)

    def chunk(off, kc, causal):
        for h in range(HP):
            rhs = jnp.concatenate([k_ref[h, pl.ds(off, kc), :], e_ref[pl.ds(off, kc), :]], axis=1)
            s = _dot_nt(lhs_ref[h], rhs)
            if causal:
                t = n * Cq + (lax.broadcasted_iota(jnp.int32, (rows, kc), 0) & (Cq - 1))
                s = jnp.where(off + lax.broadcasted_iota(jnp.int32, (rows, kc), 1) <= t, s, NEG_INF)
            m_prev = m_ref[h]
            m_new = jnp.maximum(m_prev, jnp.max(s, axis=-1, keepdims=True))
            p = jnp.exp((s - jnp.concatenate([m_new] * (kc // LANES), axis=1)).astype(BF16))
            acc_ref[h] = jnp.exp(m_prev - m_new) * acc_ref[h] + _dot(p, v_ref[h, pl.ds(off, kc), :])
            m_ref[h] = m_new

    per_chunk = SEL_CHUNK // Cq
    nmain = n // per_chunk

    def body(c, carry):
        chunk(pl.multiple_of(c * SEL_CHUNK, SEL_CHUNK), SEL_CHUNK, False)
        return carry

    lax.fori_loop(0, nmain, body, 0)
    tail0 = pl.multiple_of(nmain * SEL_CHUNK, SEL_CHUNK)
    chunk(tail0, SEL_TAIL, True)
    for i in range(1, SEL_CHUNK // SEL_TAIL):
        @pl.when(n - nmain * per_chunk >= i * (SEL_TAIL // Cq))
        def _():
            chunk(tail0 + i * SEL_TAIL, SEL_TAIL, True)

    for h in range(HP):
        acc = acc_ref[h]
        o = acc / jnp.maximum(pltpu.roll(acc, HEAD_DIM, 1), SOFTMAX_FLOOR)
        o_ref[:, h * GW:(h + 1) * GW] = _unstack_heads_lo(o, Cq).astype(BF16)


def nsa_selected(q, mneg, k2, e, v2, B, T, heads_per_step=2):
    Hk = k2.shape[1]
    HP = heads_per_step
    GW = q.shape[-1] // Hk
    G = GW // HEAD_DIM
    kv_spec = pl.BlockSpec((None, HP, T, LANES), lambda b, h, n: (b, h, 0, 0))
    tq = SEL_Q
    return pl.pallas_call(
        _nsa_sel_kernel,
        grid=(B, Hk // HP, T // tq),
        in_specs=[pl.BlockSpec((None, tq, HP * GW), lambda b, h, n: (b, n, h)),
                  pl.BlockSpec((None, HP, tq, LANES), lambda b, h, n: (b, h, n, 0)),
                  kv_spec,
                  _resident((T, LANES)),
                  kv_spec],
        out_specs=pl.BlockSpec((None, tq, HP * GW), lambda b, h, n: (b, n, h)),
        out_shape=jax.ShapeDtypeStruct((B, T, q.shape[-1]), BF16),
        scratch_shapes=[pltpu.VMEM((HP, G * tq, 2 * LANES), BF16),
                        pltpu.VMEM((HP, G * tq, LANES), F32),
                        pltpu.VMEM((HP, G * tq, LANES), F32)],
        compiler_params=_params("parallel", "parallel", "arbitrary"),
        name="nsa_selected",
    )(q, mneg, k2, e, v2)


def _nsa_out_kernel(oc_ref, os_ref, ow_ref, gates_ref, e_ref, w_ref, x_ref, gate_ref, o_ref):
    D = oc_ref.shape[1]
    sg = jax.nn.sigmoid(gates_ref[...].astype(F32)).astype(BF16)
    gx = _dot(sg, e_ref[...])
    y = (gx[:, :D] * oc_ref[...].astype(F32) + gx[:, D:2 * D] * os_ref[...].astype(F32)
         + gx[:, 2 * D:] * ow_ref[...].astype(F32))
    o_ref[...] = x_ref[...] + gate_ref[...] * _dot(y.astype(BF16), w_ref[...])


def nsa_out(oc, os_, ow, proj, gate_col_tile, expand, w, x, mod, gate_idx, T, tm=512):
    BT, D = x.shape
    tm = min(tm, T)
    tpb = T // tm
    row = pl.BlockSpec((tm, D), lambda i: (i, 0))
    return pl.pallas_call(
        _nsa_out_kernel,
        grid=(BT // tm,),
        in_specs=[row, row, row,
                  pl.BlockSpec((tm, LANES), lambda i: (i, gate_col_tile)),
                  _resident((LANES, 3 * D)),
                  _resident((D, D)),
                  row,
                  pl.BlockSpec((None, None, 1, D), lambda i: (i // tpb, gate_idx, 0, 0))],
        out_specs=row,
        out_shape=jax.ShapeDtypeStruct((BT, D), F32),
        compiler_params=_params("parallel"),
        name="nsa_out",
    )(oc, os_, ow, proj, expand, w, x, mod)


def retention_mixer(x, mod, norm_g, w_in, w_out, B, T):
    H = RET_HEADS
    D = x.shape[1]
    dk = D // H
    nqk = 2 * H * dk
    w_bf = w_in.astype(BF16)
    w_qk = w_bf[:, :nqk].reshape(D, 2 * H, dk // 2, 2).swapaxes(2, 3).reshape(D, nqk)
    w_perm = jnp.concatenate([w_qk, w_bf[:, nqk:]], axis=1)
    cos, sin = retention_tables(T, dk)
    scales = (1.0,) * H + (dk ** -0.5,) * H
    proj = normmod_matmul(x, norm_g, mod, 0, w_perm, T, rot=(cos, sin, scales))
    y = retention_core(proj.reshape(B, T, -1), B, T)
    return x, (y.reshape(B * T, -1), w_out.astype(BF16))


def swa_mixer(x, mod, norm_g, w_in, q_norm_g, k_norm_g, sinks, w_out, B, T):
    D = x.shape[1]
    Hk = SWA_KV_HEADS
    G = D // HEAD_DIM // Hk
    proj = normmod_matmul(x, norm_g, mod, 0, w_in.astype(BF16), T, tn=256)
    tables = rope_tables(T)
    nq = D // LANES
    nkv = Hk * HEAD_DIM // LANES
    plan = [(j, 0, True, Q_SCALE, ("flat", 0, j)) for j in range(nq)]
    plan += [(nq + j, 1, True, 1.0, ("heads", 1, 2 * j, False)) for j in range(nkv)]
    plan += [(nq + nkv + j, None, False, 1.0, ("heads", 2, 2 * j, True)) for j in range(nkv)]
    q, k2, v2 = head_prep(proj, jnp.stack([q_norm_g, k_norm_g]), tables, plan, [(nq, BF16)], [Hk, Hk], B, T)
    o = banded_attention(q.reshape(B, T, D), k2, v2, sinks, B, T, G=G, nprev=SWA_WINDOW // QBLK, window=SWA_WINDOW,
                         unroll=2)
    return x, (o.reshape(B * T, D), w_out.astype(BF16))


def nsa_mixer(x, mod, norm_g, w_in, q_norm_g, k_norm_g, cmp_pe, cmp_w1, cmp_w2, w_out, B, T):
    D = x.shape[1]
    Hk = NSA_KV_HEADS
    H = D // HEAD_DIM
    G = H // Hk
    kvw = Hk * HEAD_DIM
    n_main = D + 6 * kvw
    n_pad = -(-w_in.shape[1] // LANES) * LANES
    w_pad = jnp.pad(w_in, ((0, 0), (0, n_pad - w_in.shape[1]))).astype(BF16)
    proj = normmod_matmul(x, norm_g, mod, 0, w_pad, T, tn=LANES * 3)
    tables = rope_tables(T)
    nq = D // LANES
    nkv = kvw // LANES
    col = lambda i: nq + i * nkv
    plan = [(j, 0, True, Q_SCALE, ("flat", 0, j)) for j in range(nq)]
    for j in range(nkv):
        plan += [(col(0) + j, None, False, 1.0, ("flat", 1 + j, 0)),
                 (col(1) + j, None, False, 1.0, ("flat", 1 + nkv + j, 0)),
                 (col(2) + j, 1, True, 1.0, ("heads", 1 + 2 * nkv, 2 * j, False)),
                 (col(3) + j, None, False, 1.0, ("heads", 2 + 2 * nkv, 2 * j, True)),
                 (col(4) + j, 2, True, 1.0, ("heads", 3 + 2 * nkv, 2 * j, False)),
                 (col(5) + j, None, False, 1.0, ("heads", 4 + 2 * nkv, 2 * j, True))]
    gains = jnp.stack([q_norm_g, k_norm_g[1], k_norm_g[2]])
    outs = head_prep(proj, gains, tables, plan, [(nq, BF16)] + [(1, F32)] * (2 * nkv), [Hk] * 4, B, T)
    q = outs[0].reshape(B, T, D)
    kc = [a.reshape(B, T, LANES) for a in outs[1:1 + nkv]]
    vc = [a.reshape(B, T, LANES) for a in outs[1 + nkv:1 + 2 * nkv]]
    ks2, vs2, kw2, vw2 = outs[1 + 2 * nkv:]

    S = NSA_CMP_STRIDE
    kcmp = nsa_compress(kc, cmp_pe[0], cmp_w1[0], cmp_w2[0], k_norm_g[0], Hk, norm=True)
    vcmp = nsa_compress(vc, cmp_pe[1], cmp_w1[1], cmp_w2[1], k_norm_g[0], Hk, norm=False)

    ncp = T // S
    ns = T // NSA_SEL_LEN
    cs = np.arange(ncp)[:, None] * S
    js = np.arange(LANES)[None, :]
    overlap = ((cs < js * NSA_SEL_LEN + NSA_SEL_LEN) & (cs + NSA_CMP_LEN > js * NSA_SEL_LEN)
               & (js < ns) & (np.arange(ncp)[:, None] < ncp - 1))
    ov = jnp.asarray(overlap, BF16)
    oc, mneg = nsa_cmp_topk(q, kcmp, vcmp, ov, B, T, min(NSA_N_SEL, ns))

    blk_onehot = jnp.asarray(np.arange(T)[:, None] // NSA_SEL_LEN == js, BF16)
    os_ = nsa_selected(q, mneg, ks2, blk_onehot, vs2, B, T)
    ow = banded_attention(q, kw2, vw2, None, B, T, G=G, nprev=NSA_WINDOW // QBLK, window=NSA_WINDOW, unroll=True)

    rows = np.arange(LANES)[:, None]
    cols = np.arange(3 * D)[None, :]
    expand = jnp.asarray(rows == 3 * ((cols % D) // HEAD_DIM) + cols // D, BF16)
    flat = lambda a: a.reshape(B * T, D)
    return nsa_out(flat(oc), flat(os_), flat(ow), proj, n_main // LANES, expand, w_out.astype(BF16), x, mod, 2, T), None


def kernel(x, c, l0_mod_w, l0_mod_b, l0_norm1_g, l0_ret_w_in, l0_ret_w_out, l0_norm2_g, l0_ffn_w_in, l0_ffn_w_out, l1_mod_w, l1_mod_b, l1_norm1_g, l1_swa_w_in, l1_swa_q_norm_g, l1_swa_k_norm_g, l1_swa_sinks, l1_swa_w_out, l1_norm2_g, l1_ffn_w_in, l1_ffn_w_out, l2_mod_w, l2_mod_b, l2_norm1_g, l2_nsa_w_in, l2_nsa_q_norm_g, l2_nsa_k_norm_g, l2_nsa_cmp_pe, l2_nsa_cmp_w1, l2_nsa_cmp_w2, l2_nsa_w_out, l2_norm2_g, l2_ffn_w_in, l2_ffn_w_out, l3_mod_w, l3_mod_b, l3_norm1_g, l3_ret_w_in, l3_ret_w_out, l3_norm2_g, l3_ffn_w_in, l3_ffn_w_out):
    B, T, D = x.shape
    layers = [
        (l0_mod_w, l0_mod_b, l0_norm1_g, retention_mixer, (l0_ret_w_in, l0_ret_w_out), l0_norm2_g, l0_ffn_w_in, l0_ffn_w_out),
        (l1_mod_w, l1_mod_b, l1_norm1_g, swa_mixer,
         (l1_swa_w_in, l1_swa_q_norm_g, l1_swa_k_norm_g, l1_swa_sinks, l1_swa_w_out), l1_norm2_g, l1_ffn_w_in, l1_ffn_w_out),
        (l2_mod_w, l2_mod_b, l2_norm1_g, nsa_mixer,
         (l2_nsa_w_in, l2_nsa_q_norm_g, l2_nsa_k_norm_g, l2_nsa_cmp_pe, l2_nsa_cmp_w1, l2_nsa_cmp_w2, l2_nsa_w_out),
         l2_norm2_g, l2_ffn_w_in, l2_ffn_w_out),
        (l3_mod_w, l3_mod_b, l3_norm1_g, retention_mixer, (l3_ret_w_in, l3_ret_w_out), l3_norm2_g, l3_ffn_w_in, l3_ffn_w_out),
    ]
    h = x.reshape(B * T, D)
    for mod_w, mod_b, norm1_g, mixer, mixer_params, norm2_g, ffn_w_in, ffn_w_out in layers:
        mod = adaln_mod(c, mod_w, mod_b)
        h, mix = mixer(h, mod, norm1_g, *mixer_params, B, T)
        h = ffn_block(h, norm2_g, mod, ffn_w_in.astype(BF16), ffn_w_out.astype(BF16), T, mix=mix)
    return h.reshape(B, T, D)
```

```python
import functools

import jax
import jax.numpy as jnp
import numpy as np
from jax import lax
from jax.experimental import pallas as pl
from jax.experimental.pallas import tpu as pltpu

F32 = jnp.float32
BF16 = jnp.bfloat16

NORM_EPS = 1e-6
NEG_INF = -1e30
SOFTMAX_FLOOR = 1e-30
REMOVED = -3e38
MASK_BIAS = -(2.0 ** 30)

V7X_VMEM_LIMIT_BYTES = 56 * 2 ** 20
LANES = 128
QBLK = 128

ROPE_THETA = 500000.0
RET_ROT_BASE = 10000.0
RET_HEADS = 4
RET_CHUNK = 256
SWA_KV_HEADS = 2
SWA_WINDOW = 128
NSA_KV_HEADS = 4
NSA_CMP_LEN = 32
NSA_CMP_STRIDE = 16
NSA_SEL_LEN = 64
NSA_N_SEL = 16
NSA_WINDOW = 512
NSA_FORCE_SCORE = 1e4
HEAD_DIM = 64
SEL_CHUNK = 1024
SEL_TAIL = 512
SEL_Q = 512
Q_SCALE = HEAD_DIM ** -0.5

NT_DIMS = (((1,), (1,)), ((), ()))


def _params(*sem):
    return pltpu.CompilerParams(dimension_semantics=sem, vmem_limit_bytes=V7X_VMEM_LIMIT_BYTES)


def _dot(a, b):
    return jnp.dot(a, b, preferred_element_type=F32)


def _dot_nt(a, b):
    return lax.dot_general(a, b, NT_DIMS, preferred_element_type=F32)


def _mod_kernel(c_ref, w_ref, b_ref, o_ref):
    c = c_ref[...]
    s = c * jax.nn.sigmoid(c)
    nb = s.shape[0]
    s8 = jnp.concatenate([s, jnp.zeros((8 - nb, s.shape[1]), F32)], axis=0)
    r = jnp.dot(s8, w_ref[...], preferred_element_type=F32, precision=lax.Precision.HIGHEST)
    o_ref[...] = r[:nb] + b_ref[...]


def adaln_mod(c, mod_w, mod_b):
    B, D = c.shape
    N = mod_w.shape[1]
    tn = N // 4
    out = pl.pallas_call(
        _mod_kernel,
        grid=(N // tn,),
        in_specs=[pl.BlockSpec((B, D), lambda j: (0, 0)),
                  pl.BlockSpec((D, tn), lambda j: (0, j)),
                  pl.BlockSpec((1, tn), lambda j: (0, j))],
        out_specs=pl.BlockSpec((B, tn), lambda j: (0, j)),
        out_shape=jax.ShapeDtypeStruct((B, N), F32),
        compiler_params=_params("arbitrary"),
        name="adaln_mod",
    )(c, mod_w, mod_b.reshape(1, N))
    return out.reshape(B, 6, 1, D)


def _normmod(x, g, scale, shift):
    ms = jnp.mean(x * x, axis=-1, keepdims=True)
    y = x * lax.rsqrt(ms + NORM_EPS) * g
    return y * (1.0 + scale) + shift


def _resident(shape):
    return pl.BlockSpec(shape, lambda *_: (0,) * len(shape), pipeline_mode=pl.Buffered(1))


def _normmod_matmul_kernel(x_ref, g_ref, scale_ref, shift_ref, w_ref, *rest, tn, rot_scales):
    h = _normmod(x_ref[...], g_ref[...], scale_ref[...], shift_ref[...]).astype(BF16)
    o_ref = rest[-1]
    if rot_scales:
        cos, sin = rest[0][...], rest[1][...]
        hw = 2 * cos.shape[1]
    for c in range(w_ref.shape[1] // tn):
        cols = slice(c * tn, (c + 1) * tn)
        acc = _dot(h, w_ref[:, cols])
        if c * tn < len(rot_scales) * (hw if rot_scales else 0):
            parts = []
            for hd in range(tn // hw):
                x1 = acc[:, hd * hw:hd * hw + hw // 2]
                x2 = acc[:, hd * hw + hw // 2:(hd + 1) * hw]
                sc = rot_scales[c * tn // hw + hd]
                parts += [(x1 * cos - x2 * sin) * sc, (x2 * cos + x1 * sin) * sc]
            acc = jnp.concatenate(parts, axis=1)
        o_ref[:, cols] = acc.astype(o_ref.dtype)


def normmod_matmul(x, g, mod, shift_idx, w, T, tm=512, tn=512, rot=None):
    BT, D = x.shape
    N = w.shape[1]
    tm = min(tm, T)
    tn = tn if N % tn == 0 else LANES
    tpb = T // tm
    in_specs = [pl.BlockSpec((tm, D), lambda i: (i, 0)),
                _resident((1, D)),
                pl.BlockSpec((None, None, 1, D), lambda i: (i // tpb, shift_idx + 1, 0, 0)),
                pl.BlockSpec((None, None, 1, D), lambda i: (i // tpb, shift_idx, 0, 0)),
                _resident((D, N))]
    args = [x, g.reshape(1, D), mod, mod, w]
    rot_scales = ()
    if rot is not None:
        cos, sin, rot_scales = rot
        in_specs += [pl.BlockSpec((tm, cos.shape[1]), lambda i: (i % tpb, 0))] * 2
        args += [cos, sin]
    return pl.pallas_call(
        functools.partial(_normmod_matmul_kernel, tn=tn, rot_scales=tuple(rot_scales)),
        grid=(BT // tm,),
        in_specs=in_specs,
        out_specs=pl.BlockSpec((tm, N), lambda i: (i, 0)),
        out_shape=jax.ShapeDtypeStruct((BT, N), BF16),
        compiler_params=_params("parallel"),
        name="normmod_matmul",
    )(*args)


def _ffn_kernel(x_ref, g_ref, scale_ref, shift_ref, gate_ref, wi_ref, wo_ref, *rest, tc):
    o_ref = rest[-1]
    x = x_ref[...]
    if len(rest) > 1:
        *ys, wm_ref, gate1_ref = rest[:-1]
        if len(ys) == 1:
            y = ys[0][...]
        else:
            oc_ref, os_ref, ow_ref, gates_ref, e_ref = ys
            D = oc_ref.shape[1]
            sg = jax.nn.sigmoid(gates_ref[...].astype(F32)).astype(BF16)
            gx = _dot(sg, e_ref[...])
            y = (gx[:, :D] * oc_ref[...].astype(F32) + gx[:, D:2 * D] * os_ref[...].astype(F32)
                 + gx[:, 2 * D:] * ow_ref[...].astype(F32)).astype(BF16)
        x = x + gate1_ref[...] * _dot(y, wm_ref[...])
    h = _normmod(x, g_ref[...], scale_ref[...], shift_ref[...]).astype(BF16)
    F = wo_ref.shape[0]
    acc = None
    for c in range(F // tc):
        a = _dot(h, wi_ref[:, c * tc:(c + 1) * tc])
        b = _dot(h, wi_ref[:, F + c * tc:F + (c + 1) * tc])
        u = (a * jax.nn.sigmoid(a) * b).astype(BF16)
        part = _dot(u, wo_ref[c * tc:(c + 1) * tc, :])
        acc = part if acc is None else acc + part
    o_ref[...] = x + gate_ref[...] * acc


def ffn_block(x, g, mod, w_in, w_out, T, mix=None, tm=512, tc=256):
    BT, D = x.shape
    F = w_out.shape[0]
    tm = min(tm, T)
    tpb = T // tm
    modspec = lambda idx: pl.BlockSpec((None, None, 1, D), lambda i: (i // tpb, idx, 0, 0))
    in_specs = [pl.BlockSpec((tm, D), lambda i: (i, 0)),
                _resident((1, D)),
                modspec(4), modspec(3), modspec(5),
                _resident((D, 2 * F)),
                _resident((F, D))]
    args = [x, g.reshape(1, D), mod, mod, mod, w_in, w_out]
    if mix is not None:
        ys, w_mix = mix
        for y in ys:
            if len(y) == 1:
                in_specs.append(_resident(y[0].shape))
            else:
                in_specs.append(pl.BlockSpec((tm, y[1]), functools.partial(lambda i, j: (i, j), j=y[2])))
            args.append(y[0])
        in_specs += [_resident(w_mix.shape), modspec(2)]
        args += [w_mix, mod]
    return pl.pallas_call(
        functools.partial(_ffn_kernel, tc=tc),
        grid=(BT // tm,),
        in_specs=in_specs,
        out_specs=pl.BlockSpec((tm, D), lambda i: (i, 0)),
        out_shape=jax.ShapeDtypeStruct((BT, D), F32),
        compiler_params=_params("parallel"),
        name="ffn_block",
    )(*args)


def _ret_core_kernel(q_ref, k_ref, v_ref, g_ref, o_ref, state_ref, *, chunks):
    C = RET_CHUNK

    @pl.when(pl.program_id(2) == 0)
    def _():
        state_ref[...] = jnp.zeros_like(state_ref)

    head = pl.program_id(1)
    log_gammas = [float(np.log(1.0 - 2.0 ** (-5.0 - i))) for i in range(RET_HEADS)]
    lg = jnp.float32(log_gammas[-1])
    for i in range(RET_HEADS - 1):
        lg = jnp.where(head == i, log_gammas[i], lg)
    rel = (lax.broadcasted_iota(jnp.int32, (C, C), 0) - lax.broadcasted_iota(jnp.int32, (C, C), 1)).astype(F32)
    intra = jnp.where(rel >= 0, jnp.exp(lg * jnp.maximum(rel, 0.0)), 0.0)
    idx = lax.broadcasted_iota(jnp.int32, (C, 1), 0).astype(F32)
    q_decay = jnp.exp(lg * (idx + 1.0))
    k_decay = jnp.exp(lg * (C - 1.0 - idx))
    state_decay = jnp.exp(lg * float(C))

    for c in range(chunks):
        sl = slice(c * C, (c + 1) * C)
        qb = q_ref[sl, :]
        kb = k_ref[sl, :]
        v = v_ref[sl, :]
        scores = _dot_nt(qb, kb) * intra
        inner = _dot(scores.astype(BF16), v)
        state = state_ref[...]
        cross = _dot(qb, state.astype(BF16)) * q_decay
        kd_t = (kb.astype(F32) * k_decay).T.astype(BF16)
        state_ref[...] = state * state_decay + _dot(kd_t, v)
        out = inner + cross
        ms = jnp.mean(out * out, axis=-1, keepdims=True)
        y = out * lax.rsqrt(ms + NORM_EPS)
        g = g_ref[sl, :]
        o_ref[sl, :] = y.astype(BF16) * (g * jax.nn.sigmoid(g))


def retention_core(proj, B, T, tb=1024):
    H = RET_HEADS
    DK = proj.shape[-1] // (6 * H)
    DV = 2 * DK
    tb = min(tb, T)
    kern = functools.partial(_ret_core_kernel, chunks=tb // RET_CHUNK)
    return pl.pallas_call(
        kern,
        grid=(B, H, T // tb),
        in_specs=[pl.BlockSpec((None, tb, DK), lambda b, h, t: (b, t, h)),
                  pl.BlockSpec((None, tb, DK), lambda b, h, t: (b, t, H + h)),
                  pl.BlockSpec((None, tb, DV), lambda b, h, t: (b, t, H + h)),
                  pl.BlockSpec((None, tb, DV), lambda b, h, t: (b, t, 2 * H + h))],
        out_specs=pl.BlockSpec((None, tb, DV), lambda b, h, t: (b, t, h)),
        out_shape=jax.ShapeDtypeStruct((B, T, H * DV), BF16),
        scratch_shapes=[pltpu.VMEM((DK, DV), F32)],
        compiler_params=_params("parallel", "parallel", "arbitrary"),
        name="retention_core",
    )(proj, proj, proj, proj)


def _head_prep_kernel(x_ref, gains_ref, c_ref, s1_ref, s2_ref, bd_ref, *out_refs, plan):
    c, s1, s2 = c_ref[...], s1_ref[...], s2_ref[...]
    bd = bd_ref[...]
    first_half = lax.broadcasted_iota(jnp.int32, c.shape, 1) < HEAD_DIM
    for src, gain_row, rope, scale, dst in plan:
        x = x_ref[:, src * LANES:(src + 1) * LANES].astype(F32)
        if gain_row is not None:
            ss = _dot((x * x).astype(BF16), bd)
            x = x * lax.rsqrt(ss * (1.0 / HEAD_DIM) + NORM_EPS) * gains_ref[gain_row:gain_row + 1, :]
        if rope:
            x = x * c + pltpu.roll(x, LANES - 8, 1) * s1 + pltpu.roll(x, 8, 1) * s2
        if scale != 1.0:
            x = x * scale
        if dst[0] == "flat":
            out = out_refs[dst[1]]
            out[:, dst[2] * LANES:(dst[2] + 1) * LANES] = x.astype(out.dtype)
        else:
            _, out, head0, ones = dst
            swapped = pltpu.roll(x, HEAD_DIM, 1)
            a = jnp.where(first_half, x, 1.0 if ones else swapped)
            b = jnp.where(first_half, swapped, 1.0 if ones else x)
            out_refs[out][head0] = a.astype(BF16)
            out_refs[out][head0 + 1] = b.astype(BF16)


def head_prep(proj, gains, tables, plan, flat_outs, head_outs, B, T, tm=512):
    BT, N = proj.shape
    tm = min(tm, T)
    tpb = T // tm
    bd = jnp.asarray(np.kron(np.eye(2), np.ones((HEAD_DIM, HEAD_DIM))), BF16)
    gains2 = jnp.concatenate([gains, gains], axis=1).astype(F32)
    tab = pl.BlockSpec((tm, LANES), lambda i: (i % tpb, 0))
    out_specs, out_shape = [], []
    for tiles, dtype in flat_outs:
        out_specs.append(pl.BlockSpec((tm, tiles * LANES), lambda i: (i, 0)))
        out_shape.append(jax.ShapeDtypeStruct((BT, tiles * LANES), dtype))
    for nh in head_outs:
        out_specs.append(pl.BlockSpec((None, nh, tm, LANES), lambda i: (i // tpb, 0, i % tpb, 0)))
        out_shape.append(jax.ShapeDtypeStruct((B, nh, T, LANES), BF16))
    return pl.pallas_call(
        functools.partial(_head_prep_kernel, plan=tuple(plan)),
        grid=(BT // tm,),
        in_specs=[pl.BlockSpec((tm, N), lambda i: (i, 0)),
                  _resident(gains2.shape),
                  tab, tab, tab,
                  _resident((LANES, LANES))],
        out_specs=out_specs,
        out_shape=out_shape,
        compiler_params=_params("parallel"),
        name="head_prep",
    )(proj, gains2, *tables, bd)


def rope_tables(T):
    half = HEAD_DIM // 4 // 2
    inv = ROPE_THETA ** (-jnp.arange(half, dtype=F32) / half)
    lane = np.arange(LANES) % HEAD_DIM
    lo, hi = lane < half, (lane >= half) & (lane < 2 * half)
    inv_lane = jnp.where(lo | hi, inv[lane % half], 0.0)
    ang = jnp.arange(T).astype(F32)[:, None] * inv_lane[None, :]
    sin = jnp.sin(ang)
    return jnp.cos(ang), jnp.where(lo, -sin, 0.0), jnp.where(hi, sin, 0.0)


def retention_tables(T, dk):
    inv = 1.0 / (RET_ROT_BASE ** jnp.linspace(0.0, 1.0, dk // 2, dtype=F32))
    ang = jnp.arange(T).astype(F32)[:, None] * inv[None, :]
    return jnp.cos(ang), jnp.sin(ang)


def _stack_heads(q, extra=None):
    first_half = lax.broadcasted_iota(jnp.int32, (q.shape[0], LANES), 1) < HEAD_DIM
    zero = jnp.zeros((q.shape[0], LANES), q.dtype)
    rows = []
    for p in range(q.shape[1] // LANES):
        qp = q[:, p * LANES:(p + 1) * LANES]
        for part in (jnp.where(first_half, qp, zero), jnp.where(first_half, zero, qp)):
            rows.append(part if extra is None else jnp.concatenate([part, extra], axis=1))
    return jnp.concatenate(rows, axis=0)


def _unstack_heads(o, cq):
    first_half = lax.broadcasted_iota(jnp.int32, (cq, LANES), 1) < HEAD_DIM
    G = o.shape[0] // cq
    pairs = [jnp.where(first_half, o[(2 * p) * cq:(2 * p + 1) * cq], o[(2 * p + 1) * cq:(2 * p + 2) * cq])
             for p in range(G // 2)]
    return jnp.concatenate(pairs, axis=1)


def _unstack_heads_lo(o, cq):
    first_half = lax.broadcasted_iota(jnp.int32, (cq, LANES), 1) < HEAD_DIM
    G = o.shape[0] // cq
    pairs = [jnp.where(first_half, o[(2 * p) * cq:(2 * p + 1) * cq],
                       pltpu.roll(o[(2 * p + 1) * cq:(2 * p + 2) * cq], HEAD_DIM, 1))
             for p in range(G // 2)]
    return jnp.concatenate(pairs, axis=1)


def _banded_kernel(*refs, G, nprev, window, use_sink, qb, unroll):
    if use_sink:
        sink_ref, q_ref, k_ref, v_ref, o_ref = refs
    else:
        q_ref, k_ref, v_ref, o_ref = refs
    Cq = QBLK
    nk = (nprev + 1) * Cq
    hk = pl.program_id(1)

    def block(i, carry):
        n = pl.program_id(2) * qb + i
        r0 = pl.multiple_of(i * Cq, Cq)
        start = pl.multiple_of(jnp.maximum(n - nprev, 0) * Cq, Cq)
        kwin = k_ref[pl.ds(start, nk), :]
        vwin = v_ref[pl.ds(start, nk), :]
        lhs = _stack_heads(q_ref[pl.ds(r0, Cq), :])
        s_all = _dot_nt(lhs, kwin)
        t = n * Cq + lax.broadcasted_iota(jnp.int32, (Cq, nk), 0)
        rel = t - (start + lax.broadcasted_iota(jnp.int32, (Cq, nk), 1))
        valid = jnp.logical_and(rel >= 0, rel < window)
        outs = []
        for g in range(G):
            s = jnp.where(valid, s_all[g * Cq:(g + 1) * Cq], NEG_INF)
            m = jnp.max(s, axis=-1, keepdims=True)
            if use_sink:
                sink = sink_ref[hk * G + g]
                m = jnp.maximum(m, sink)
            p = jnp.exp((s - m).astype(BF16))
            acc = _dot(p, vwin)
            denom = pltpu.roll(acc, HEAD_DIM, 1)
            if use_sink:
                denom = denom + jnp.exp(sink - m)
            else:
                denom = jnp.maximum(denom, SOFTMAX_FLOOR)
            outs.append(acc / denom)
        o_ref[pl.ds(r0, Cq), :] = _unstack_heads_lo(jnp.concatenate(outs, axis=0), Cq).astype(BF16)
        return carry

    lax.fori_loop(0, qb, block, 0, unroll=unroll)


def banded_attention(q, k2, v2, sinks, B, T, *, G, nprev, window, unroll, qb=4):
    Hk = k2.shape[1]
    qb = min(qb, T // QBLK)
    use_sink = sinks is not None
    kern = functools.partial(_banded_kernel, G=G, nprev=nprev, window=window, use_sink=use_sink, qb=qb, unroll=unroll)
    in_specs = [pl.BlockSpec((None, qb * QBLK, G * HEAD_DIM), lambda b, h, n: (b, n, h)),
                pl.BlockSpec((None, None, T, LANES), lambda b, h, n: (b, h, 0, 0)),
                pl.BlockSpec((None, None, T, LANES), lambda b, h, n: (b, h, 0, 0))]
    args = [q, k2, v2]
    if use_sink:
        in_specs = [pl.BlockSpec(memory_space=pltpu.SMEM)] + in_specs
        args = [sinks.astype(F32)] + args
    return pl.pallas_call(
        kern,
        grid=(B, Hk, T // (qb * QBLK)),
        in_specs=in_specs,
        out_specs=pl.BlockSpec((None, qb * QBLK, G * HEAD_DIM), lambda b, h, n: (b, n, h)),
        out_shape=jax.ShapeDtypeStruct((B, T, Hk * G * HEAD_DIM), BF16),
        compiler_params=_params("parallel", "parallel", "arbitrary"),
        name="banded_attention",
    )(*args)


def _nsa_compress_kernel(*refs, norm, npairs):
    x_refs = refs[:npairs]
    pe_ref, w1_ref, wbig_ref, w2_ref, gain_ref, bd_ref, o_ref = refs[npairs:]
    S = NSA_CMP_STRIDE
    ng = x_refs[0].shape[0] // S
    nh = o_ref.shape[0]
    u = [None, None]
    for r in range(S):
        xr = jnp.concatenate([x[pl.ds(r, ng, stride=S), :] for x in x_refs], axis=1).astype(BF16)
        for a in range(2):
            part = _dot(xr, wbig_ref[a, r])
            u[a] = part if u[a] is None else u[a] + part
    pe8 = jnp.broadcast_to(pe_ref[...], (8, pe_ref.shape[1]))
    const = _dot(pe8, w1_ref[...])[:1]
    pre = u[0] + pltpu.roll(u[1], ng - 1, 0) + jnp.concatenate([const] * nh, axis=1)
    hid = (pre * jax.nn.sigmoid(pre)).astype(BF16)
    hw = hid.shape[1] // nh
    for h in range(nh):
        y = _dot(hid[:, h * hw:(h + 1) * hw], w2_ref[...])
        if norm:
            ss = _dot((y * y).astype(BF16), bd_ref[...])
            y = y * lax.rsqrt(ss * (1.0 / HEAD_DIM) + NORM_EPS) * gain_ref[...]
        o_ref[h] = y.astype(BF16)


def nsa_compress(xs, pe, w1, w2, gain, Hk, *, norm):
    B, T, _ = xs[0].shape
    W = len(xs) * LANES
    S = NSA_CMP_STRIDE
    hid = w1.shape[1]
    bd = jnp.asarray(np.kron(np.eye(2), np.ones((HEAD_DIM, HEAD_DIM))), BF16)
    gain2 = jnp.concatenate([gain, gain]).reshape(1, LANES).astype(F32)
    w2d = jnp.concatenate([w2, w2], axis=1).astype(BF16)
    w1b = w1.astype(BF16)
    wbig = jnp.einsum("ardc,kh->arkdhc", w1b.reshape(2, S, HEAD_DIM, hid), jnp.eye(Hk, dtype=BF16))
    wbig = wbig.reshape(2, S, W, Hk * hid)
    return pl.pallas_call(
        functools.partial(_nsa_compress_kernel, norm=norm, npairs=len(xs)),
        grid=(B,),
        in_specs=[pl.BlockSpec((None, T, LANES), lambda b: (b, 0, 0))] * len(xs) + [
                  _resident((1, w1.shape[0])),
                  _resident(w1.shape),
                  _resident(wbig.shape),
                  _resident((hid, LANES)),
                  _resident((1, LANES)),
                  _resident((LANES, LANES))],
        out_specs=pl.BlockSpec((None, Hk, T // S, LANES), lambda b: (b, 0, 0, 0)),
        out_shape=jax.ShapeDtypeStruct((B, Hk, T // S, LANES), BF16),
        compiler_params=_params("parallel"),
        name="nsa_compress",
    )(*xs, pe.reshape(1, w1.shape[0]).astype(BF16), w1b, wbig, w2d, gain2, bd)


def _topk_rows_bias(v, n_sel):
    nslab = v.shape[0] // 8
    sub = lax.broadcasted_iota(jnp.int32, (8, v.shape[1]), 0).astype(F32)
    vals = [v[8 * k:8 * k + 8] for k in range(nslab)]
    ids = [sub + 8.0 * k for k in range(nslab)]
    bias = [jnp.full(sub.shape, MASK_BIAS, F32)] * nslab
    for _ in range(n_sel):
        cand = list(zip(vals, ids))
        while len(cand) > 1:
            nxt = []
            for (va, ia), (vb, ib) in zip(cand[0::2], cand[1::2]):
                take_b = vb > va
                nxt.append((jnp.where(take_b, vb, va), jnp.where(take_b, ib, ia)))
            cand = nxt + ([cand[-1]] if len(cand) % 2 else [])
        bv, bi = cand[0]
        for shift in (4, 2, 1):
            rv, ri = pltpu.roll(bv, shift, 0), pltpu.roll(bi, shift, 0)
            take_r = jnp.logical_or(rv > bv, jnp.logical_and(rv == bv, ri < bi))
            bv, bi = jnp.where(take_r, rv, bv), jnp.where(take_r, ri, bi)
        first = [ids[k] == bi for k in range(nslab)]
        vals = [jnp.where(first[k], REMOVED, vals[k]) for k in range(nslab)]
        bias = [jnp.where(first[k], 0.0, bias[k]) for k in range(nslab)]
    return jnp.concatenate(bias, axis=0)


def _nsa_cmp_kernel(q_ref, kc_ref, vc_ref, ov_ref, oc_ref, mneg_ref, imp_ref, *, n_sel, qb):
    Cq = QBLK
    G = q_ref.shape[1] // HEAD_DIM
    ncp = kc_ref.shape[0]
    step = pl.program_id(2)

    def attend(width):
        for i in range(qb):
            n = step * qb + i
            rows = slice(i * Cq, (i + 1) * Cq)
            s_all = _dot_nt(_stack_heads(q_ref[rows, :]), kc_ref[:width, :])
            t = n * Cq + lax.broadcasted_iota(jnp.int32, (Cq, width), 0)
            cmp_end = lax.broadcasted_iota(jnp.int32, (Cq, width), 1) * NSA_CMP_STRIDE + (NSA_CMP_LEN - 1)
            valid = cmp_end <= t
            outs = []
            psum = None
            for g in range(G):
                s = jnp.where(valid, s_all[g * Cq:(g + 1) * Cq], NEG_INF)
                p = jnp.exp(s - jnp.maximum(jnp.max(s, axis=-1, keepdims=True), 0.1 * NEG_INF))
                p = p / jnp.maximum(jnp.sum(p, axis=-1, keepdims=True), SOFTMAX_FLOOR)
                outs.append(_dot(p.astype(BF16), vc_ref[:width, :]))
                psum = p if psum is None else psum + p
            oc_ref[rows, :] = _unstack_heads(jnp.concatenate(outs, axis=0), Cq).astype(BF16)
            hi = psum.astype(BF16)
            lo = (psum - hi.astype(F32)).astype(BF16)
            ov = ov_ref[:width, :]
            imp_ref[i] = _dot(hi, ov) + _dot(lo, ov)

    n_valid = ((step + 1) * qb * Cq - NSA_CMP_LEN) // NSA_CMP_STRIDE + 1
    groups = (n_valid + LANES - 1) // LANES
    widths = list(range(LANES, ncp, LANES)) + [ncp]
    for gi, width in enumerate(widths):
        last = gi == len(widths) - 1
        pl.when(groups >= gi + 1 if last else groups == gi + 1)(functools.partial(attend, width))

    for i in range(qb):
        n = step * qb + i
        rows = slice(i * Cq, (i + 1) * Cq)
        imp = imp_ref[i].T
        nsb = imp.shape[0]
        j = lax.broadcasted_iota(jnp.int32, (nsb, Cq), 0)
        tt = n * Cq + lax.broadcasted_iota(jnp.int32, (nsb, Cq), 1)
        cur = tt // NSA_SEL_LEN
        forced = jnp.logical_or(j == 0, jnp.logical_or(j == cur, j == cur - 1))
        v = jnp.where(forced, NSA_FORCE_SCORE, imp)
        v = jnp.where(j * NSA_SEL_LEN <= tt, v, NEG_INF)
        mneg_ref[rows, :] = _topk_rows_bias(v, n_sel).T.astype(BF16)


def nsa_cmp_topk(q, kcmp, vcmp, ov, B, T, n_sel, qb=2):
    Hk = kcmp.shape[1]
    GW = q.shape[-1] // Hk
    ncp = kcmp.shape[2]
    cmp_spec = pl.BlockSpec((None, None, ncp, LANES), lambda b, h, n: (b, h, 0, 0))
    tq = qb * QBLK
    return pl.pallas_call(
        functools.partial(_nsa_cmp_kernel, n_sel=n_sel, qb=qb),
        grid=(B, Hk, T // tq),
        in_specs=[pl.BlockSpec((None, tq, GW), lambda b, h, n: (b, n, h)),
                  cmp_spec, cmp_spec,
                  pl.BlockSpec((ncp, LANES), lambda b, h, n: (0, 0))],
        out_specs=[pl.BlockSpec((None, tq, GW), lambda b, h, n: (b, n, h)),
                   pl.BlockSpec((None, None, tq, LANES), lambda b, h, n: (b, h, n, 0))],
        out_shape=[jax.ShapeDtypeStruct((B, T, q.shape[-1]), BF16),
                   jax.ShapeDtypeStruct((B, Hk, T, LANES), BF16)],
        scratch_shapes=[pltpu.VMEM((qb, QBLK, LANES), F32)],
        compiler_params=_params("parallel", "parallel", "arbitrary"),
        name="nsa_cmp_topk",
    )(q, kcmp, vcmp, ov)


def _nsa_sel_kernel(q_ref, mneg_ref, k_ref, e_ref, v_ref, o_ref, lhs_ref, m_ref, acc_ref):
    Cq = q_ref.shape[0]
    HP = k_ref.shape[0]
    GW = q_ref.shape[1] // HP
    n = pl.program_id(2)
    rows = lhs_ref.shape[1]
    for h in range(HP):
        lhs_ref[h] = _stack_heads(q_ref[:, h * GW:(h + 1) * GW], extra=mneg_ref[h])
    m_ref[...] = jnp.full(m_ref.shape, NEG_INF, F32)
    acc_ref[...] = jnp.zeros_like(acc_ref)

    def chunk(off, kc, causal):
        for h in range(HP):
            rhs = jnp.concatenate([k_ref[h, pl.ds(off, kc), :], e_ref[pl.ds(off, kc), :]], axis=1)
            s = _dot_nt(lhs_ref[h], rhs)
            if causal:
                t = n * Cq + (lax.broadcasted_iota(jnp.int32, (rows, kc), 0) & (Cq - 1))
                s = jnp.where(off + lax.broadcasted_iota(jnp.int32, (rows, kc), 1) <= t, s, NEG_INF)
            m_prev = m_ref[h]
            m_new = jnp.maximum(m_prev, jnp.max(s, axis=-1, keepdims=True))
            p = jnp.exp((s - jnp.concatenate([m_new] * (kc // LANES), axis=1)).astype(BF16))
            acc_ref[h] = jnp.exp(m_prev - m_new) * acc_ref[h] + _dot(p, v_ref[h, pl.ds(off, kc), :])
            m_ref[h] = m_new

    per_chunk = SEL_CHUNK // Cq
    nmain = n // per_chunk

    def body(c, carry):
        chunk(pl.multiple_of(c * SEL_CHUNK, SEL_CHUNK), SEL_CHUNK, False)
        return carry

    lax.fori_loop(0, nmain, body, 0)
    tail0 = pl.multiple_of(nmain * SEL_CHUNK, SEL_CHUNK)
    chunk(tail0, SEL_TAIL, True)
    for i in range(1, SEL_CHUNK // SEL_TAIL):
        @pl.when(n - nmain * per_chunk >= i * (SEL_TAIL // Cq))
        def _():
            chunk(tail0 + i * SEL_TAIL, SEL_TAIL, True)

    for h in range(HP):
        acc = acc_ref[h]
        o = acc / jnp.maximum(pltpu.roll(acc, HEAD_DIM, 1), SOFTMAX_FLOOR)
        o_ref[:, h * GW:(h + 1) * GW] = _unstack_heads_lo(o, Cq).astype(BF16)


def nsa_selected(q, mneg, k2, e, v2, B, T, heads_per_step=2):
    Hk = k2.shape[1]
    HP = heads_per_step
    GW = q.shape[-1] // Hk
    G = GW // HEAD_DIM
    kv_spec = pl.BlockSpec((None, HP, T, LANES), lambda b, h, n: (b, h, 0, 0))
    tq = SEL_Q
    return pl.pallas_call(
        _nsa_sel_kernel,
        grid=(B, Hk // HP, T // tq),
        in_specs=[pl.BlockSpec((None, tq, HP * GW), lambda b, h, n: (b, n, h)),
                  pl.BlockSpec((None, HP, tq, LANES), lambda b, h, n: (b, h, n, 0)),
                  kv_spec,
                  _resident((T, LANES)),
                  kv_spec],
        out_specs=pl.BlockSpec((None, tq, HP * GW), lambda b, h, n: (b, n, h)),
        out_shape=jax.ShapeDtypeStruct((B, T, q.shape[-1]), BF16),
        scratch_shapes=[pltpu.VMEM((HP, G * tq, 2 * LANES), BF16),
                        pltpu.VMEM((HP, G * tq, LANES), F32),
                        pltpu.VMEM((HP, G * tq, LANES), F32)],
        compiler_params=_params("parallel", "parallel", "arbitrary"),
        name="nsa_selected",
    )(q, mneg, k2, e, v2)


def retention_mixer(x, mod, norm_g, w_in, w_out, B, T):
    H = RET_HEADS
    D = x.shape[1]
    dk = D // H
    nqk = 2 * H * dk
    w_bf = w_in.astype(BF16)
    order = np.concatenate([np.arange(0, dk, 2), np.arange(1, dk, 2)])
    pick = jnp.asarray(np.arange(dk)[:, None] == order[None, :], BF16)
    w_qk = jnp.einsum("dhk,kj->dhj", w_bf[:, :nqk].reshape(D, 2 * H, dk), pick, preferred_element_type=BF16)
    w_perm = jnp.concatenate([w_qk.reshape(D, nqk), w_bf[:, nqk:]], axis=1)
    cos, sin = retention_tables(T, dk)
    scales = (1.0,) * H + (dk ** -0.5,) * H
    proj = normmod_matmul(x, norm_g, mod, 0, w_perm, T, rot=(cos, sin, scales))
    y = retention_core(proj.reshape(B, T, -1), B, T)
    y = y.reshape(B * T, -1)
    return x, ([(y, y.shape[1], 0)], w_out.astype(BF16))


def swa_mixer(x, mod, norm_g, w_in, q_norm_g, k_norm_g, sinks, w_out, B, T):
    D = x.shape[1]
    Hk = SWA_KV_HEADS
    G = D // HEAD_DIM // Hk
    proj = normmod_matmul(x, norm_g, mod, 0, w_in.astype(BF16), T, tn=256)
    tables = rope_tables(T)
    nq = D // LANES
    nkv = Hk * HEAD_DIM // LANES
    plan = [(j, 0, True, Q_SCALE, ("flat", 0, j)) for j in range(nq)]
    plan += [(nq + j, 1, True, 1.0, ("heads", 1, 2 * j, False)) for j in range(nkv)]
    plan += [(nq + nkv + j, None, False, 1.0, ("heads", 2, 2 * j, True)) for j in range(nkv)]
    q, k2, v2 = head_prep(proj, jnp.stack([q_norm_g, k_norm_g]), tables, plan, [(nq, BF16)], [Hk, Hk], B, T)
    o = banded_attention(q.reshape(B, T, D), k2, v2, sinks, B, T, G=G, nprev=SWA_WINDOW // QBLK, window=SWA_WINDOW,
                         unroll=2)
    return x, ([(o.reshape(B * T, D), D, 0)], w_out.astype(BF16))


def nsa_mixer(x, mod, norm_g, w_in, q_norm_g, k_norm_g, cmp_pe, cmp_w1, cmp_w2, w_out, B, T):
    D = x.shape[1]
    Hk = NSA_KV_HEADS
    H = D // HEAD_DIM
    G = H // Hk
    kvw = Hk * HEAD_DIM
    n_main = D + 6 * kvw
    n_pad = -(-w_in.shape[1] // LANES) * LANES
    w_pad = jnp.pad(w_in, ((0, 0), (0, n_pad - w_in.shape[1]))).astype(BF16)
    proj = normmod_matmul(x, norm_g, mod, 0, w_pad, T, tn=LANES * 3)
    tables = rope_tables(T)
    nq = D // LANES
    nkv = kvw // LANES
    col = lambda i: nq + i * nkv
    plan = [(j, 0, True, Q_SCALE, ("flat", 0, j)) for j in range(nq)]
    for j in range(nkv):
        plan += [(col(0) + j, None, False, 1.0, ("flat", 1 + j, 0)),
                 (col(1) + j, None, False, 1.0, ("flat", 1 + nkv + j, 0)),
                 (col(2) + j, 1, True, 1.0, ("heads", 1 + 2 * nkv, 2 * j, False)),
                 (col(3) + j, None, False, 1.0, ("heads", 2 + 2 * nkv, 2 * j, True)),
                 (col(4) + j, 2, True, 1.0, ("heads", 3 + 2 * nkv, 2 * j, False)),
                 (col(5) + j, None, False, 1.0, ("heads", 4 + 2 * nkv, 2 * j, True))]
    gains = jnp.stack([q_norm_g, k_norm_g[1], k_norm_g[2]])
    outs = head_prep(proj, gains, tables, plan, [(nq, BF16)] + [(1, F32)] * (2 * nkv), [Hk] * 4, B, T)
    q = outs[0].reshape(B, T, D)
    kc = [a.reshape(B, T, LANES) for a in outs[1:1 + nkv]]
    vc = [a.reshape(B, T, LANES) for a in outs[1 + nkv:1 + 2 * nkv]]
    ks2, vs2, kw2, vw2 = outs[1 + 2 * nkv:]

    S = NSA_CMP_STRIDE
    kcmp = nsa_compress(kc, cmp_pe[0], cmp_w1[0], cmp_w2[0], k_norm_g[0], Hk, norm=True)
    vcmp = nsa_compress(vc, cmp_pe[1], cmp_w1[1], cmp_w2[1], k_norm_g[0], Hk, norm=False)

    ncp = T // S
    ns = T // NSA_SEL_LEN
    cs = np.arange(ncp)[:, None] * S
    js = np.arange(LANES)[None, :]
    overlap = ((cs < js * NSA_SEL_LEN + NSA_SEL_LEN) & (cs + NSA_CMP_LEN > js * NSA_SEL_LEN)
               & (js < ns) & (np.arange(ncp)[:, None] < ncp - 1))
    ov = jnp.asarray(overlap, BF16)
    oc, mneg = nsa_cmp_topk(q, kcmp, vcmp, ov, B, T, min(NSA_N_SEL, ns))

    blk_onehot = jnp.asarray(np.arange(T)[:, None] // NSA_SEL_LEN == js, BF16)
    os_ = nsa_selected(q, mneg, ks2, blk_onehot, vs2, B, T)
    ow = banded_attention(q, kw2, vw2, None, B, T, G=G, nprev=NSA_WINDOW // QBLK, window=NSA_WINDOW, unroll=True)

    rows = np.arange(LANES)[:, None]
    cols = np.arange(3 * D)[None, :]
    expand = jnp.asarray(rows == 3 * ((cols % D) // HEAD_DIM) + cols // D, BF16)
    branch = lambda a: (a.reshape(B * T, D), D, 0)
    ys = [branch(oc), branch(os_), branch(ow), (proj, LANES, n_main // LANES), (expand,)]
    return x, (ys, w_out.astype(BF16))


def kernel(x, c, l0_mod_w, l0_mod_b, l0_norm1_g, l0_ret_w_in, l0_ret_w_out, l0_norm2_g, l0_ffn_w_in, l0_ffn_w_out, l1_mod_w, l1_mod_b, l1_norm1_g, l1_swa_w_in, l1_swa_q_norm_g, l1_swa_k_norm_g, l1_swa_sinks, l1_swa_w_out, l1_norm2_g, l1_ffn_w_in, l1_ffn_w_out, l2_mod_w, l2_mod_b, l2_norm1_g, l2_nsa_w_in, l2_nsa_q_norm_g, l2_nsa_k_norm_g, l2_nsa_cmp_pe, l2_nsa_cmp_w1, l2_nsa_cmp_w2, l2_nsa_w_out, l2_norm2_g, l2_ffn_w_in, l2_ffn_w_out, l3_mod_w, l3_mod_b, l3_norm1_g, l3_ret_w_in, l3_ret_w_out, l3_norm2_g, l3_ffn_w_in, l3_ffn_w_out):
    B, T, D = x.shape
    layers = [
        (l0_mod_w, l0_mod_b, l0_norm1_g, retention_mixer, (l0_ret_w_in, l0_ret_w_out), l0_norm2_g, l0_ffn_w_in, l0_ffn_w_out),
        (l1_mod_w, l1_mod_b, l1_norm1_g, swa_mixer,
         (l1_swa_w_in, l1_swa_q_norm_g, l1_swa_k_norm_g, l1_swa_sinks, l1_swa_w_out), l1_norm2_g, l1_ffn_w_in, l1_ffn_w_out),
        (l2_mod_w, l2_mod_b, l2_norm1_g, nsa_mixer,
         (l2_nsa_w_in, l2_nsa_q_norm_g, l2_nsa_k_norm_g, l2_nsa_cmp_pe, l2_nsa_cmp_w1, l2_nsa_cmp_w2, l2_nsa_w_out),
         l2_norm2_g, l2_ffn_w_in, l2_ffn_w_out),
        (l3_mod_w, l3_mod_b, l3_norm1_g, retention_mixer, (l3_ret_w_in, l3_ret_w_out), l3_norm2_g, l3_ffn_w_in, l3_ffn_w_out),
    ]
    h = x.reshape(B * T, D)
    for mod_w, mod_b, norm1_g, mixer, mixer_params, norm2_g, ffn_w_in, ffn_w_out in layers:
        mod = adaln_mod(c, mod_w, mod_b)
        h, mix = mixer(h, mod, norm1_g, *mixer_params, B, T)
        h = ffn_block(h, norm2_g, mod, ffn_w_in.astype(BF16), ffn_w_out.astype(BF16), T, mix=mix)
    return h.reshape(B, T, D)
```

```python
import functools

import jax
import jax.numpy as jnp
import numpy as np
from jax import lax
from jax.experimental import pallas as pl
from jax.experimental.pallas import tpu as pltpu

F32 = jnp.float32
BF16 = jnp.bfloat16

NORM_EPS = 1e-6
NEG_INF = -1e30
SOFTMAX_FLOOR = 1e-30
REMOVED = -3e38
MASK_BIAS = -(2.0 ** 30)

V7X_VMEM_LIMIT_BYTES = 56 * 2 ** 20
LANES = 128
QBLK = 128

ROPE_THETA = 500000.0
RET_ROT_BASE = 10000.0
RET_HEADS = 4
RET_CHUNK = 256
SWA_KV_HEADS = 2
SWA_WINDOW = 128
NSA_KV_HEADS = 4
NSA_CMP_LEN = 32
NSA_CMP_STRIDE = 16
NSA_SEL_LEN = 64
NSA_N_SEL = 16
NSA_WINDOW = 512
NSA_FORCE_SCORE = 1e4
HEAD_DIM = 64
SEL_CHUNK = 1024
SEL_TAIL = 512
SEL_Q = 512
Q_SCALE = HEAD_DIM ** -0.5

NT_DIMS = (((1,), (1,)), ((), ()))


def _params(*sem):
    return pltpu.CompilerParams(dimension_semantics=sem, vmem_limit_bytes=V7X_VMEM_LIMIT_BYTES)


def _dot(a, b):
    return jnp.dot(a, b, preferred_element_type=F32)


def _dot_nt(a, b):
    return lax.dot_general(a, b, NT_DIMS, preferred_element_type=F32)


def _mod_kernel(c_ref, w_ref, b_ref, o_ref):
    c = c_ref[...]
    s = c * jax.nn.sigmoid(c)
    nb = s.shape[0]
    s8 = jnp.concatenate([s, jnp.zeros((8 - nb, s.shape[1]), F32)], axis=0)
    r = jnp.dot(s8, w_ref[...], preferred_element_type=F32, precision=lax.Precision.HIGHEST)
    o_ref[...] = r[:nb] + b_ref[...]


def adaln_mod(c, mod_w, mod_b):
    B, D = c.shape
    N = mod_w.shape[1]
    tn = N // 4
    out = pl.pallas_call(
        _mod_kernel,
        grid=(N // tn,),
        in_specs=[pl.BlockSpec((B, D), lambda j: (0, 0)),
                  pl.BlockSpec((D, tn), lambda j: (0, j)),
                  pl.BlockSpec((1, tn), lambda j: (0, j))],
        out_specs=pl.BlockSpec((B, tn), lambda j: (0, j)),
        out_shape=jax.ShapeDtypeStruct((B, N), F32),
        compiler_params=_params("arbitrary"),
        name="adaln_mod",
    )(c, mod_w, mod_b.reshape(1, N))
    return out.reshape(B, 6, 1, D)


def _normmod(x, g, scale, shift):
    ms = jnp.mean(x * x, axis=-1, keepdims=True)
    y = x * lax.rsqrt(ms + NORM_EPS) * g
    return y * (1.0 + scale) + shift


def _resident(shape):
    return pl.BlockSpec(shape, lambda *_: (0,) * len(shape), pipeline_mode=pl.Buffered(1))


def _normmod_matmul_kernel(x_ref, g_ref, scale_ref, shift_ref, w_ref, *rest, tn, rot_scales):
    h = _normmod(x_ref[...], g_ref[...], scale_ref[...], shift_ref[...]).astype(BF16)
    o_ref = rest[-1]
    if rot_scales:
        cos, sin = rest[0][...], rest[1][...]
        hw = 2 * cos.shape[1]
    for c in range(w_ref.shape[1] // tn):
        cols = slice(c * tn, (c + 1) * tn)
        acc = _dot(h, w_ref[:, cols])
        if c * tn < len(rot_scales) * (hw if rot_scales else 0):
            parts = []
            for hd in range(tn // hw):
                x1 = acc[:, hd * hw:hd * hw + hw // 2]
                x2 = acc[:, hd * hw + hw // 2:(hd + 1) * hw]
                sc = rot_scales[c * tn // hw + hd]
                parts += [(x1 * cos - x2 * sin) * sc, (x2 * cos + x1 * sin) * sc]
            acc = jnp.concatenate(parts, axis=1)
        o_ref[:, cols] = acc.astype(o_ref.dtype)


def normmod_matmul(x, g, mod, shift_idx, w, T, tm=512, tn=512, rot=None):
    BT, D = x.shape
    N = w.shape[1]
    tm = min(tm, T)
    tn = tn if N % tn == 0 else LANES
    tpb = T // tm
    in_specs = [pl.BlockSpec((tm, D), lambda i: (i, 0)),
                _resident((1, D)),
                pl.BlockSpec((None, None, 1, D), lambda i: (i // tpb, shift_idx + 1, 0, 0)),
                pl.BlockSpec((None, None, 1, D), lambda i: (i // tpb, shift_idx, 0, 0)),
                _resident((D, N))]
    args = [x, g.reshape(1, D), mod, mod, w]
    rot_scales = ()
    if rot is not None:
        cos, sin, rot_scales = rot
        in_specs += [pl.BlockSpec((tm, cos.shape[1]), lambda i: (i % tpb, 0))] * 2
        args += [cos, sin]
    return pl.pallas_call(
        functools.partial(_normmod_matmul_kernel, tn=tn, rot_scales=tuple(rot_scales)),
        grid=(BT // tm,),
        in_specs=in_specs,
        out_specs=pl.BlockSpec((tm, N), lambda i: (i, 0)),
        out_shape=jax.ShapeDtypeStruct((BT, N), BF16),
        compiler_params=_params("parallel"),
        name="normmod_matmul",
    )(*args)


def _ffn_kernel(x_ref, g_ref, scale_ref, shift_ref, gate_ref, wi_ref, wo_ref, *rest, tc):
    o_ref = rest[-1]
    x = x_ref[...]
    if len(rest) > 1:
        *ys, wm_ref, gate1_ref = rest[:-1]
        if len(ys) == 1:
            y = ys[0][...]
        else:
            oc_ref, os_ref, ow_ref, gates_ref, e_ref = ys
            D = oc_ref.shape[1]
            sg = jax.nn.sigmoid(gates_ref[...].astype(F32)).astype(BF16)
            gx = _dot(sg, e_ref[...])
            y = (gx[:, :D] * oc_ref[...].astype(F32) + gx[:, D:2 * D] * os_ref[...].astype(F32)
                 + gx[:, 2 * D:] * ow_ref[...].astype(F32)).astype(BF16)
        x = x + gate1_ref[...] * _dot(y, wm_ref[...])
    h = _normmod(x, g_ref[...], scale_ref[...], shift_ref[...]).astype(BF16)
    F = wo_ref.shape[0]
    acc = None
    for c in range(F // tc):
        a = _dot(h, wi_ref[:, c * tc:(c + 1) * tc])
        b = _dot(h, wi_ref[:, F + c * tc:F + (c + 1) * tc])
        u = (a * jax.nn.sigmoid(a) * b).astype(BF16)
        part = _dot(u, wo_ref[c * tc:(c + 1) * tc, :])
        acc = part if acc is None else acc + part
    o_ref[...] = x + gate_ref[...] * acc


def ffn_block(x, g, mod, w_in, w_out, T, mix=None, tm=512, tc=256):
    BT, D = x.shape
    F = w_out.shape[0]
    tm = min(tm, T)
    tpb = T // tm
    modspec = lambda idx: pl.BlockSpec((None, None, 1, D), lambda i: (i // tpb, idx, 0, 0))
    in_specs = [pl.BlockSpec((tm, D), lambda i: (i, 0)),
                _resident((1, D)),
                modspec(4), modspec(3), modspec(5),
                _resident((D, 2 * F)),
                _resident((F, D))]
    args = [x, g.reshape(1, D), mod, mod, mod, w_in, w_out]
    if mix is not None:
        ys, w_mix = mix
        for y in ys:
            if len(y) == 1:
                in_specs.append(_resident(y[0].shape))
            else:
                in_specs.append(pl.BlockSpec((tm, y[1]), functools.partial(lambda i, j: (i, j), j=y[2])))
            args.append(y[0])
        in_specs += [_resident(w_mix.shape), modspec(2)]
        args += [w_mix, mod]
    return pl.pallas_call(
        functools.partial(_ffn_kernel, tc=tc),
        grid=(BT // tm,),
        in_specs=in_specs,
        out_specs=pl.BlockSpec((tm, D), lambda i: (i, 0)),
        out_shape=jax.ShapeDtypeStruct((BT, D), F32),
        compiler_params=_params("parallel"),
        name="ffn_block",
    )(*args)


def _ret_core_kernel(q_ref, k_ref, v_ref, g_ref, o_ref, state_ref, *, chunks):
    C = RET_CHUNK

    @pl.when(pl.program_id(2) == 0)
    def _():
        state_ref[...] = jnp.zeros_like(state_ref)

    head = pl.program_id(1)
    log_gammas = [float(np.log(1.0 - 2.0 ** (-5.0 - i))) for i in range(RET_HEADS)]
    lg = jnp.float32(log_gammas[-1])
    for i in range(RET_HEADS - 1):
        lg = jnp.where(head == i, log_gammas[i], lg)
    rel = (lax.broadcasted_iota(jnp.int32, (C, C), 0) - lax.broadcasted_iota(jnp.int32, (C, C), 1)).astype(F32)
    intra = jnp.where(rel >= 0, jnp.exp(lg * jnp.maximum(rel, 0.0)), 0.0)
    idx = lax.broadcasted_iota(jnp.int32, (C, 1), 0).astype(F32)
    q_decay = jnp.exp(lg * (idx + 1.0))
    k_decay = jnp.exp(lg * (C - 1.0 - idx))
    state_decay = jnp.exp(lg * float(C))

    for c in range(chunks):
        sl = slice(c * C, (c + 1) * C)
        qb = q_ref[sl, :]
        kb = k_ref[sl, :]
        v = v_ref[sl, :]
        scores = _dot_nt(qb, kb) * intra
        inner = _dot(scores.astype(BF16), v)
        state = state_ref[...]
        cross = _dot(qb, state.astype(BF16)) * q_decay
        kd_t = (kb.astype(F32) * k_decay).T.astype(BF16)
        state_ref[...] = state * state_decay + _dot(kd_t, v)
        out = inner + cross
        ms = jnp.mean(out * out, axis=-1, keepdims=True)
        y = out * lax.rsqrt(ms + NORM_EPS)
        g = g_ref[sl, :]
        o_ref[sl, :] = y.astype(BF16) * (g * jax.nn.sigmoid(g))


def retention_core(proj, B, T, tb=1024):
    H = RET_HEADS
    DK = proj.shape[-1] // (6 * H)
    DV = 2 * DK
    tb = min(tb, T)
    kern = functools.partial(_ret_core_kernel, chunks=tb // RET_CHUNK)
    return pl.pallas_call(
        kern,
        grid=(B, H, T // tb),
        in_specs=[pl.BlockSpec((None, tb, DK), lambda b, h, t: (b, t, h)),
                  pl.BlockSpec((None, tb, DK), lambda b, h, t: (b, t, H + h)),
                  pl.BlockSpec((None, tb, DV), lambda b, h, t: (b, t, H + h)),
                  pl.BlockSpec((None, tb, DV), lambda b, h, t: (b, t, 2 * H + h))],
        out_specs=pl.BlockSpec((None, tb, DV), lambda b, h, t: (b, t, h)),
        out_shape=jax.ShapeDtypeStruct((B, T, H * DV), BF16),
        scratch_shapes=[pltpu.VMEM((DK, DV), F32)],
        compiler_params=_params("parallel", "parallel", "arbitrary"),
        name="retention_core",
    )(proj, proj, proj, proj)


def _head_prep_kernel(x_ref, gains_ref, c_ref, s1_ref, s2_ref, bd_ref, *out_refs, plan):
    c, s1, s2 = c_ref[...], s1_ref[...], s2_ref[...]
    bd = bd_ref[...]
    first_half = lax.broadcasted_iota(jnp.int32, c.shape, 1) < HEAD_DIM
    for src, gain_row, rope, scale, dst in plan:
        x = x_ref[:, src * LANES:(src + 1) * LANES].astype(F32)
        if gain_row is not None:
            ss = _dot((x * x).astype(BF16), bd)
            x = x * lax.rsqrt(ss * (1.0 / HEAD_DIM) + NORM_EPS) * gains_ref[gain_row:gain_row + 1, :]
        if rope:
            x = x * c + pltpu.roll(x, LANES - 8, 1) * s1 + pltpu.roll(x, 8, 1) * s2
        if scale != 1.0:
            x = x * scale
        if dst[0] == "flat":
            out = out_refs[dst[1]]
            out[:, dst[2] * LANES:(dst[2] + 1) * LANES] = x.astype(out.dtype)
        else:
            _, out, head0, ones = dst
            swapped = pltpu.roll(x, HEAD_DIM, 1)
            a = jnp.where(first_half, x, 1.0 if ones else swapped)
            b = jnp.where(first_half, swapped, 1.0 if ones else x)
            out_refs[out][head0] = a.astype(BF16)
            out_refs[out][head0 + 1] = b.astype(BF16)


def head_prep(proj, gains, tables, plan, flat_outs, head_outs, B, T, tm=512):
    BT, N = proj.shape
    tm = min(tm, T)
    tpb = T // tm
    bd = jnp.asarray(np.kron(np.eye(2), np.ones((HEAD_DIM, HEAD_DIM))), BF16)
    gains2 = jnp.concatenate([gains, gains], axis=1).astype(F32)
    tab = pl.BlockSpec((tm, LANES), lambda i: (i % tpb, 0))
    out_specs, out_shape = [], []
    for tiles, dtype in flat_outs:
        out_specs.append(pl.BlockSpec((tm, tiles * LANES), lambda i: (i, 0)))
        out_shape.append(jax.ShapeDtypeStruct((BT, tiles * LANES), dtype))
    for nh in head_outs:
        out_specs.append(pl.BlockSpec((None, nh, tm, LANES), lambda i: (i // tpb, 0, i % tpb, 0)))
        out_shape.append(jax.ShapeDtypeStruct((B, nh, T, LANES), BF16))
    return pl.pallas_call(
        functools.partial(_head_prep_kernel, plan=tuple(plan)),
        grid=(BT // tm,),
        in_specs=[pl.BlockSpec((tm, N), lambda i: (i, 0)),
                  _resident(gains2.shape),
                  tab, tab, tab,
                  _resident((LANES, LANES))],
        out_specs=out_specs,
        out_shape=out_shape,
        compiler_params=_params("parallel"),
        name="head_prep",
    )(proj, gains2, *tables, bd)


def rope_tables(T):
    half = HEAD_DIM // 4 // 2
    inv = ROPE_THETA ** (-jnp.arange(half, dtype=F32) / half)
    lane = np.arange(LANES) % HEAD_DIM
    lo, hi = lane < half, (lane >= half) & (lane < 2 * half)
    inv_lane = jnp.where(lo | hi, inv[lane % half], 0.0)
    ang = jnp.arange(T).astype(F32)[:, None] * inv_lane[None, :]
    sin = jnp.sin(ang)
    return jnp.cos(ang), jnp.where(lo, -sin, 0.0), jnp.where(hi, sin, 0.0)


def retention_tables(T, dk):
    inv = 1.0 / (RET_ROT_BASE ** jnp.linspace(0.0, 1.0, dk // 2, dtype=F32))
    ang = jnp.arange(T).astype(F32)[:, None] * inv[None, :]
    return jnp.cos(ang), jnp.sin(ang)


def _stack_heads(q, extra=None):
    first_half = lax.broadcasted_iota(jnp.int32, (q.shape[0], LANES), 1) < HEAD_DIM
    zero = jnp.zeros((q.shape[0], LANES), q.dtype)
    rows = []
    for p in range(q.shape[1] // LANES):
        qp = q[:, p * LANES:(p + 1) * LANES]
        for part in (jnp.where(first_half, qp, zero), jnp.where(first_half, zero, qp)):
            rows.append(part if extra is None else jnp.concatenate([part, extra], axis=1))
    return jnp.concatenate(rows, axis=0)


def _unstack_heads(o, cq):
    first_half = lax.broadcasted_iota(jnp.int32, (cq, LANES), 1) < HEAD_DIM
    G = o.shape[0] // cq
    pairs = [jnp.where(first_half, o[(2 * p) * cq:(2 * p + 1) * cq], o[(2 * p + 1) * cq:(2 * p + 2) * cq])
             for p in range(G // 2)]
    return jnp.concatenate(pairs, axis=1)


def _unstack_heads_lo(o, cq):
    first_half = lax.broadcasted_iota(jnp.int32, (cq, LANES), 1) < HEAD_DIM
    G = o.shape[0] // cq
    pairs = [jnp.where(first_half, o[(2 * p) * cq:(2 * p + 1) * cq],
                       pltpu.roll(o[(2 * p + 1) * cq:(2 * p + 2) * cq], HEAD_DIM, 1))
             for p in range(G // 2)]
    return jnp.concatenate(pairs, axis=1)


def _banded_kernel(*refs, G, nprev, window, use_sink, qb, unroll):
    if use_sink:
        sink_ref, q_ref, k_ref, v_ref, o_ref = refs
    else:
        q_ref, k_ref, v_ref, o_ref = refs
    Cq = QBLK
    nk = (nprev + 1) * Cq
    hk = pl.program_id(1)

    def block(i, carry):
        n = pl.program_id(2) * qb + i
        r0 = pl.multiple_of(i * Cq, Cq)
        start = pl.multiple_of(jnp.maximum(n - nprev, 0) * Cq, Cq)
        kwin = k_ref[pl.ds(start, nk), :]
        vwin = v_ref[pl.ds(start, nk), :]
        lhs = _stack_heads(q_ref[pl.ds(r0, Cq), :])
        s_all = _dot_nt(lhs, kwin)
        t = n * Cq + lax.broadcasted_iota(jnp.int32, (Cq, nk), 0)
        rel = t - (start + lax.broadcasted_iota(jnp.int32, (Cq, nk), 1))
        valid = jnp.logical_and(rel >= 0, rel < window)
        outs = []
        for g in range(G):
            s = jnp.where(valid, s_all[g * Cq:(g + 1) * Cq], NEG_INF)
            m = jnp.max(s, axis=-1, keepdims=True)
            if use_sink:
                sink = sink_ref[hk * G + g]
                m = jnp.maximum(m, sink)
            p = jnp.exp((s - m).astype(BF16))
            acc = _dot(p, vwin)
            denom = pltpu.roll(acc, HEAD_DIM, 1)
            if use_sink:
                denom = denom + jnp.exp(sink - m)
            else:
                denom = jnp.maximum(denom, SOFTMAX_FLOOR)
            outs.append(acc / denom)
        o_ref[pl.ds(r0, Cq), :] = _unstack_heads_lo(jnp.concatenate(outs, axis=0), Cq).astype(BF16)
        return carry

    lax.fori_loop(0, qb, block, 0, unroll=unroll)


def banded_attention(q, k2, v2, sinks, B, T, *, G, nprev, window, unroll, qb=4):
    Hk = k2.shape[1]
    qb = min(qb, T // QBLK)
    use_sink = sinks is not None
    kern = functools.partial(_banded_kernel, G=G, nprev=nprev, window=window, use_sink=use_sink, qb=qb, unroll=unroll)
    in_specs = [pl.BlockSpec((None, qb * QBLK, G * HEAD_DIM), lambda b, h, n: (b, n, h)),
                pl.BlockSpec((None, None, T, LANES), lambda b, h, n: (b, h, 0, 0)),
                pl.BlockSpec((None, None, T, LANES), lambda b, h, n: (b, h, 0, 0))]
    args = [q, k2, v2]
    if use_sink:
        in_specs = [pl.BlockSpec(memory_space=pltpu.SMEM)] + in_specs
        args = [sinks.astype(F32)] + args
    return pl.pallas_call(
        kern,
        grid=(B, Hk, T // (qb * QBLK)),
        in_specs=in_specs,
        out_specs=pl.BlockSpec((None, qb * QBLK, G * HEAD_DIM), lambda b, h, n: (b, n, h)),
        out_shape=jax.ShapeDtypeStruct((B, T, Hk * G * HEAD_DIM), BF16),
        compiler_params=_params("parallel", "parallel", "arbitrary"),
        name="banded_attention",
    )(*args)


def _nsa_compress_kernel(*refs, norm, npairs):
    x_refs = refs[:npairs]
    pe_ref, w1_ref, wbig_ref, w2_ref, gain_ref, bd_ref, o_ref = refs[npairs:]
    S = NSA_CMP_STRIDE
    ng = x_refs[0].shape[0] // S
    nh = o_ref.shape[0]
    u = [None, None]
    for r in range(S):
        xr = jnp.concatenate([x[pl.ds(r, ng, stride=S), :] for x in x_refs], axis=1).astype(BF16)
        for a in range(2):
            part = _dot(xr, wbig_ref[a, r])
            u[a] = part if u[a] is None else u[a] + part
    pe8 = jnp.broadcast_to(pe_ref[...], (8, pe_ref.shape[1]))
    const = _dot(pe8, w1_ref[...])[:1]
    pre = u[0] + pltpu.roll(u[1], ng - 1, 0) + jnp.concatenate([const] * nh, axis=1)
    hid = (pre * jax.nn.sigmoid(pre)).astype(BF16)
    hw = hid.shape[1] // nh
    for h in range(nh):
        y = _dot(hid[:, h * hw:(h + 1) * hw], w2_ref[...])
        if norm:
            ss = _dot((y * y).astype(BF16), bd_ref[...])
            y = y * lax.rsqrt(ss * (1.0 / HEAD_DIM) + NORM_EPS) * gain_ref[...]
        o_ref[h] = y.astype(BF16)


def nsa_compress(xs, pe, w1, w2, gain, Hk, *, norm):
    B, T, _ = xs[0].shape
    W = len(xs) * LANES
    S = NSA_CMP_STRIDE
    hid = w1.shape[1]
    bd = jnp.asarray(np.kron(np.eye(2), np.ones((HEAD_DIM, HEAD_DIM))), BF16)
    gain2 = jnp.concatenate([gain, gain]).reshape(1, LANES).astype(F32)
    w2d = jnp.concatenate([w2, w2], axis=1).astype(BF16)
    w1b = w1.astype(BF16)
    wbig = jnp.einsum("ardc,kh->arkdhc", w1b.reshape(2, S, HEAD_DIM, hid), jnp.eye(Hk, dtype=BF16))
    wbig = wbig.reshape(2, S, W, Hk * hid)
    return pl.pallas_call(
        functools.partial(_nsa_compress_kernel, norm=norm, npairs=len(xs)),
        grid=(B,),
        in_specs=[pl.BlockSpec((None, T, LANES), lambda b: (b, 0, 0))] * len(xs) + [
                  _resident((1, w1.shape[0])),
                  _resident(w1.shape),
                  _resident(wbig.shape),
                  _resident((hid, LANES)),
                  _resident((1, LANES)),
                  _resident((LANES, LANES))],
        out_specs=pl.BlockSpec((None, Hk, T // S, LANES), lambda b: (b, 0, 0, 0)),
        out_shape=jax.ShapeDtypeStruct((B, Hk, T // S, LANES), BF16),
        compiler_params=_params("parallel"),
        name="nsa_compress",
    )(*xs, pe.reshape(1, w1.shape[0]).astype(BF16), w1b, wbig, w2d, gain2, bd)


def _topk_rows_bias(v, n_sel):
    nslab = v.shape[0] // 8
    sub = lax.broadcasted_iota(jnp.int32, (8, v.shape[1]), 0).astype(F32)
    vals = [v[8 * k:8 * k + 8] for k in range(nslab)]
    ids = [sub + 8.0 * k for k in range(nslab)]
    bias = [jnp.full(sub.shape, MASK_BIAS, F32)] * nslab
    for _ in range(n_sel):
        cand = list(zip(vals, ids))
        while len(cand) > 1:
            nxt = []
            for (va, ia), (vb, ib) in zip(cand[0::2], cand[1::2]):
                take_b = vb > va
                nxt.append((jnp.where(take_b, vb, va), jnp.where(take_b, ib, ia)))
            cand = nxt + ([cand[-1]] if len(cand) % 2 else [])
        bv, bi = cand[0]
        for shift in (4, 2, 1):
            rv, ri = pltpu.roll(bv, shift, 0), pltpu.roll(bi, shift, 0)
            take_r = jnp.logical_or(rv > bv, jnp.logical_and(rv == bv, ri < bi))
            bv, bi = jnp.where(take_r, rv, bv), jnp.where(take_r, ri, bi)
        first = [ids[k] == bi for k in range(nslab)]
        vals = [jnp.where(first[k], REMOVED, vals[k]) for k in range(nslab)]
        bias = [jnp.where(first[k], 0.0, bias[k]) for k in range(nslab)]
    return jnp.concatenate(bias, axis=0)


def _nsa_cmp_kernel(q_ref, kc_ref, vc_ref, ov_ref, oc_ref, mneg_ref, imp_ref, *, n_sel, qb):
    Cq = QBLK
    G = q_ref.shape[1] // HEAD_DIM
    ncp = kc_ref.shape[0]
    step = pl.program_id(2)

    def attend(width):
        for i in range(qb):
            n = step * qb + i
            rows = slice(i * Cq, (i + 1) * Cq)
            s_all = _dot_nt(_stack_heads(q_ref[rows, :]), kc_ref[:width, :])
            t = n * Cq + lax.broadcasted_iota(jnp.int32, (Cq, width), 0)
            cmp_end = lax.broadcasted_iota(jnp.int32, (Cq, width), 1) * NSA_CMP_STRIDE + (NSA_CMP_LEN - 1)
            valid = cmp_end <= t
            outs = []
            psum = None
            for g in range(G):
                s = jnp.where(valid, s_all[g * Cq:(g + 1) * Cq], NEG_INF)
                p = jnp.exp(s - jnp.maximum(jnp.max(s, axis=-1, keepdims=True), 0.1 * NEG_INF))
                p = p / jnp.maximum(jnp.sum(p, axis=-1, keepdims=True), SOFTMAX_FLOOR)
                outs.append(_dot(p.astype(BF16), vc_ref[:width, :]))
                psum = p if psum is None else psum + p
            oc_ref[rows, :] = _unstack_heads(jnp.concatenate(outs, axis=0), Cq).astype(BF16)
            hi = psum.astype(BF16)
            lo = (psum - hi.astype(F32)).astype(BF16)
            ov = ov_ref[:width, :]
            imp_ref[i] = _dot(hi, ov) + _dot(lo, ov)

    n_valid = ((step + 1) * qb * Cq - NSA_CMP_LEN) // NSA_CMP_STRIDE + 1
    groups = (n_valid + LANES - 1) // LANES
    widths = list(range(LANES, ncp, LANES)) + [ncp]
    for gi, width in enumerate(widths):
        last = gi == len(widths) - 1
        pl.when(groups >= gi + 1 if last else groups == gi + 1)(functools.partial(attend, width))

    for i in range(qb):
        n = step * qb + i
        rows = slice(i * Cq, (i + 1) * Cq)
        imp = imp_ref[i].T
        nsb = imp.shape[0]
        j = lax.broadcasted_iota(jnp.int32, (nsb, Cq), 0)
        tt = n * Cq + lax.broadcasted_iota(jnp.int32, (nsb, Cq), 1)
        cur = tt // NSA_SEL_LEN
        forced = jnp.logical_or(j == 0, jnp.logical_or(j == cur, j == cur - 1))
        v = jnp.where(forced, NSA_FORCE_SCORE, imp)
        v = jnp.where(j * NSA_SEL_LEN <= tt, v, NEG_INF)
        mneg_ref[rows, :] = _topk_rows_bias(v, n_sel).T.astype(BF16)


def nsa_cmp_topk(q, kcmp, vcmp, ov, B, T, n_sel, qb=4):
    Hk = kcmp.shape[1]
    GW = q.shape[-1] // Hk
    ncp = kcmp.shape[2]
    cmp_spec = pl.BlockSpec((None, None, ncp, LANES), lambda b, h, n: (b, h, 0, 0))
    tq = qb * QBLK
    return pl.pallas_call(
        functools.partial(_nsa_cmp_kernel, n_sel=n_sel, qb=qb),
        grid=(B, Hk, T // tq),
        in_specs=[pl.BlockSpec((None, tq, GW), lambda b, h, n: (b, n, h)),
                  cmp_spec, cmp_spec,
                  pl.BlockSpec((ncp, LANES), lambda b, h, n: (0, 0))],
        out_specs=[pl.BlockSpec((None, tq, GW), lambda b, h, n: (b, n, h)),
                   pl.BlockSpec((None, None, tq, LANES), lambda b, h, n: (b, h, n, 0))],
        out_shape=[jax.ShapeDtypeStruct((B, T, q.shape[-1]), BF16),
                   jax.ShapeDtypeStruct((B, Hk, T, LANES), BF16)],
        scratch_shapes=[pltpu.VMEM((qb, QBLK, LANES), F32)],
        compiler_params=_params("parallel", "parallel", "arbitrary"),
        name="nsa_cmp_topk",
    )(q, kcmp, vcmp, ov)


def _nsa_sel_kernel(q_ref, mneg_ref, k_ref, e_ref, v_ref, o_ref, lhs_ref, m_ref, acc_ref):
    Cq = q_ref.shape[0]
    HP = k_ref.shape[0]
    GW = q_ref.shape[1] // HP
    n = pl.program_id(2)
    rows = lhs_ref.shape[1]
    for h in range(HP):
        lhs_ref[h] = _stack_heads(q_ref[:, h * GW:(h + 1) * GW], extra=mneg_ref[h])
    m_ref[...] = jnp.full(m_ref.shape, NEG_INF, F32)
    acc_ref[...] = jnp.zeros_like(acc_ref)

    def chunk(off, kc, causal):
        for h in range(HP):
            rhs = jnp.concatenate([k_ref[h, pl.ds(off, kc), :], e_ref[pl.ds(off, kc), :]], axis=1)
            s = _dot_nt(lhs_ref[h], rhs)
            if causal:
                t = n * Cq + (lax.broadcasted_iota(jnp.int32, (rows, kc), 0) & (Cq - 1))
                s = jnp.where(off + lax.broadcasted_iota(jnp.int32, (rows, kc), 1) <= t, s, NEG_INF)
            m_prev = m_ref[h]
            m_new = jnp.maximum(m_prev, jnp.max(s, axis=-1, keepdims=True))
            p = jnp.exp((s - jnp.concatenate([m_new] * (kc // LANES), axis=1)).astype(BF16))
            acc_ref[h] = jnp.exp(m_prev - m_new) * acc_ref[h] + _dot(p, v_ref[h, pl.ds(off, kc), :])
            m_ref[h] = m_new

    per_chunk = SEL_CHUNK // Cq
    nmain = n // per_chunk

    def body(c, carry):
        chunk(pl.multiple_of(c * SEL_CHUNK, SEL_CHUNK), SEL_CHUNK, False)
        return carry

    lax.fori_loop(0, nmain, body, 0)
    tail0 = pl.multiple_of(nmain * SEL_CHUNK, SEL_CHUNK)
    chunk(tail0, SEL_TAIL, True)
    for i in range(1, SEL_CHUNK // SEL_TAIL):
        @pl.when(n - nmain * per_chunk >= i * (SEL_TAIL // Cq))
        def _():
            chunk(tail0 + i * SEL_TAIL, SEL_TAIL, True)

    for h in range(HP):
        acc = acc_ref[h]
        o = acc / jnp.maximum(pltpu.roll(acc, HEAD_DIM, 1), SOFTMAX_FLOOR)
        o_ref[:, h * GW:(h + 1) * GW] = _unstack_heads_lo(o, Cq).astype(BF16)


def nsa_selected(q, mneg, k2, e, v2, B, T, heads_per_step=2):
    Hk = k2.shape[1]
    HP = heads_per_step
    GW = q.shape[-1] // Hk
    G = GW // HEAD_DIM
    kv_spec = pl.BlockSpec((None, HP, T, LANES), lambda b, h, n: (b, h, 0, 0))
    tq = SEL_Q
    return pl.pallas_call(
        _nsa_sel_kernel,
        grid=(B, Hk // HP, T // tq),
        in_specs=[pl.BlockSpec((None, tq, HP * GW), lambda b, h, n: (b, n, h)),
                  pl.BlockSpec((None, HP, tq, LANES), lambda b, h, n: (b, h, n, 0)),
                  kv_spec,
                  _resident((T, LANES)),
                  kv_spec],
        out_specs=pl.BlockSpec((None, tq, HP * GW), lambda b, h, n: (b, n, h)),
        out_shape=jax.ShapeDtypeStruct((B, T, q.shape[-1]), BF16),
        scratch_shapes=[pltpu.VMEM((HP, G * tq, 2 * LANES), BF16),
                        pltpu.VMEM((HP, G * tq, LANES), F32),
                        pltpu.VMEM((HP, G * tq, LANES), F32)],
        compiler_params=_params("parallel", "parallel", "arbitrary"),
        name="nsa_selected",
    )(q, mneg, k2, e, v2)


def retention_mixer(x, mod, norm_g, w_in, w_out, B, T):
    H = RET_HEADS
    D = x.shape[1]
    dk = D // H
    nqk = 2 * H * dk
    w_bf = w_in.astype(BF16)
    order = np.concatenate([np.arange(0, dk, 2), np.arange(1, dk, 2)])
    pick = jnp.asarray(np.arange(dk)[:, None] == order[None, :], BF16)
    w_qk = jnp.einsum("dhk,kj->dhj", w_bf[:, :nqk].reshape(D, 2 * H, dk), pick, preferred_element_type=BF16)
    w_perm = jnp.concatenate([w_qk.reshape(D, nqk), w_bf[:, nqk:]], axis=1)
    cos, sin = retention_tables(T, dk)
    scales = (1.0,) * H + (dk ** -0.5,) * H
    proj = normmod_matmul(x, norm_g, mod, 0, w_perm, T, rot=(cos, sin, scales))
    y = retention_core(proj.reshape(B, T, -1), B, T)
    y = y.reshape(B * T, -1)
    return x, ([(y, y.shape[1], 0)], w_out.astype(BF16))


def swa_mixer(x, mod, norm_g, w_in, q_norm_g, k_norm_g, sinks, w_out, B, T):
    D = x.shape[1]
    Hk = SWA_KV_HEADS
    G = D // HEAD_DIM // Hk
    proj = normmod_matmul(x, norm_g, mod, 0, w_in.astype(BF16), T, tn=256)
    tables = rope_tables(T)
    nq = D // LANES
    nkv = Hk * HEAD_DIM // LANES
    plan = [(j, 0, True, Q_SCALE, ("flat", 0, j)) for j in range(nq)]
    plan += [(nq + j, 1, True, 1.0, ("heads", 1, 2 * j, False)) for j in range(nkv)]
    plan += [(nq + nkv + j, None, False, 1.0, ("heads", 2, 2 * j, True)) for j in range(nkv)]
    q, k2, v2 = head_prep(proj, jnp.stack([q_norm_g, k_norm_g]), tables, plan, [(nq, BF16)], [Hk, Hk], B, T)
    o = banded_attention(q.reshape(B, T, D), k2, v2, sinks, B, T, G=G, nprev=SWA_WINDOW // QBLK, window=SWA_WINDOW,
                         unroll=2)
    return x, ([(o.reshape(B * T, D), D, 0)], w_out.astype(BF16))


def nsa_mixer(x, mod, norm_g, w_in, q_norm_g, k_norm_g, cmp_pe, cmp_w1, cmp_w2, w_out, B, T):
    D = x.shape[1]
    Hk = NSA_KV_HEADS
    H = D // HEAD_DIM
    G = H // Hk
    kvw = Hk * HEAD_DIM
    n_main = D + 6 * kvw
    n_pad = -(-w_in.shape[1] // LANES) * LANES
    w_pad = jnp.pad(w_in, ((0, 0), (0, n_pad - w_in.shape[1]))).astype(BF16)
    proj = normmod_matmul(x, norm_g, mod, 0, w_pad, T, tn=LANES * 3)
    tables = rope_tables(T)
    nq = D // LANES
    nkv = kvw // LANES
    col = lambda i: nq + i * nkv
    plan = [(j, 0, True, Q_SCALE, ("flat", 0, j)) for j in range(nq)]
    for j in range(nkv):
        plan += [(col(0) + j, None, False, 1.0, ("flat", 1 + j, 0)),
                 (col(1) + j, None, False, 1.0, ("flat", 1 + nkv + j, 0)),
                 (col(2) + j, 1, True, 1.0, ("heads", 1 + 2 * nkv, 2 * j, False)),
                 (col(3) + j, None, False, 1.0, ("heads", 2 + 2 * nkv, 2 * j, True)),
                 (col(4) + j, 2, True, 1.0, ("heads", 3 + 2 * nkv, 2 * j, False)),
                 (col(5) + j, None, False, 1.0, ("heads", 4 + 2 * nkv, 2 * j, True))]
    gains = jnp.stack([q_norm_g, k_norm_g[1], k_norm_g[2]])
    outs = head_prep(proj, gains, tables, plan, [(nq, BF16)] + [(1, F32)] * (2 * nkv), [Hk] * 4, B, T)
    q = outs[0].reshape(B, T, D)
    kc = [a.reshape(B, T, LANES) for a in outs[1:1 + nkv]]
    vc = [a.reshape(B, T, LANES) for a in outs[1 + nkv:1 + 2 * nkv]]
    ks2, vs2, kw2, vw2 = outs[1 + 2 * nkv:]

    S = NSA_CMP_STRIDE
    kcmp = nsa_compress(kc, cmp_pe[0], cmp_w1[0], cmp_w2[0], k_norm_g[0], Hk, norm=True)
    vcmp = nsa_compress(vc, cmp_pe[1], cmp_w1[1], cmp_w2[1], k_norm_g[0], Hk, norm=False)

    ncp = T // S
    ns = T // NSA_SEL_LEN
    cs = np.arange(ncp)[:, None] * S
    js = np.arange(LANES)[None, :]
    overlap = ((cs < js * NSA_SEL_LEN + NSA_SEL_LEN) & (cs + NSA_CMP_LEN > js * NSA_SEL_LEN)
               & (js < ns) & (np.arange(ncp)[:, None] < ncp - 1))
    ov = jnp.asarray(overlap, BF16)
    oc, mneg = nsa_cmp_topk(q, kcmp, vcmp, ov, B, T, min(NSA_N_SEL, ns))

    blk_onehot = jnp.asarray(np.arange(T)[:, None] // NSA_SEL_LEN == js, BF16)
    os_ = nsa_selected(q, mneg, ks2, blk_onehot, vs2, B, T)
    ow = banded_attention(q, kw2, vw2, None, B, T, G=G, nprev=NSA_WINDOW // QBLK, window=NSA_WINDOW, unroll=True)

    rows = np.arange(LANES)[:, None]
    cols = np.arange(3 * D)[None, :]
    expand = jnp.asarray(rows == 3 * ((cols % D) // HEAD_DIM) + cols // D, BF16)
    branch = lambda a: (a.reshape(B * T, D), D, 0)
    ys = [branch(oc), branch(os_), branch(ow), (proj, LANES, n_main // LANES), (expand,)]
    return x, (ys, w_out.astype(BF16))


def kernel(x, c, l0_mod_w, l0_mod_b, l0_norm1_g, l0_ret_w_in, l0_ret_w_out, l0_norm2_g, l0_ffn_w_in, l0_ffn_w_out, l1_mod_w, l1_mod_b, l1_norm1_g, l1_swa_w_in, l1_swa_q_norm_g, l1_swa_k_norm_g, l1_swa_sinks, l1_swa_w_out, l1_norm2_g, l1_ffn_w_in, l1_ffn_w_out, l2_mod_w, l2_mod_b, l2_norm1_g, l2_nsa_w_in, l2_nsa_q_norm_g, l2_nsa_k_norm_g, l2_nsa_cmp_pe, l2_nsa_cmp_w1, l2_nsa_cmp_w2, l2_nsa_w_out, l2_norm2_g, l2_ffn_w_in, l2_ffn_w_out, l3_mod_w, l3_mod_b, l3_norm1_g, l3_ret_w_in, l3_ret_w_out, l3_norm2_g, l3_ffn_w_in, l3_ffn_w_out):
    B, T, D = x.shape
    layers = [
        (l0_mod_w, l0_mod_b, l0_norm1_g, retention_mixer, (l0_ret_w_in, l0_ret_w_out), l0_norm2_g, l0_ffn_w_in, l0_ffn_w_out),
        (l1_mod_w, l1_mod_b, l1_norm1_g, swa_mixer,
         (l1_swa_w_in, l1_swa_q_norm_g, l1_swa_k_norm_g, l1_swa_sinks, l1_swa_w_out), l1_norm2_g, l1_ffn_w_in, l1_ffn_w_out),
        (l2_mod_w, l2_mod_b, l2_norm1_g, nsa_mixer,
         (l2_nsa_w_in, l2_nsa_q_norm_g, l2_nsa_k_norm_g, l2_nsa_cmp_pe, l2_nsa_cmp_w1, l2_nsa_cmp_w2, l2_nsa_w_out),
         l2_norm2_g, l2_ffn_w_in, l2_ffn_w_out),
        (l3_mod_w, l3_mod_b, l3_norm1_g, retention_mixer, (l3_ret_w_in, l3_ret_w_out), l3_norm2_g, l3_ffn_w_in, l3_ffn_w_out),
    ]
    h = x.reshape(B * T, D)
    for mod_w, mod_b, norm1_g, mixer, mixer_params, norm2_g, ffn_w_in, ffn_w_out in layers:
        mod = adaln_mod(c, mod_w, mod_b)
        h, mix = mixer(h, mod, norm1_g, *mixer_params, B, T)
        h = ffn_block(h, norm2_g, mod, ffn_w_in.astype(BF16), ffn_w_out.astype(BF16), T, mix=mix)
    return h.reshape(B, T, D)
```

```python
import functools

import jax
import jax.numpy as jnp
import numpy as np
from jax import lax
from jax.experimental import pallas as pl
from jax.experimental.pallas import tpu as pltpu

F32 = jnp.float32
BF16 = jnp.bfloat16

NORM_EPS = 1e-6
NEG_INF = -1e30
SOFTMAX_FLOOR = 1e-30
REMOVED = -3e38
MASK_BIAS = -(2.0 ** 30)

V7X_VMEM_LIMIT_BYTES = 56 * 2 ** 20
LANES = 128
QBLK = 128

ROPE_THETA = 500000.0
RET_ROT_BASE = 10000.0
RET_HEADS = 4
RET_CHUNK = 256
SWA_KV_HEADS = 2
SWA_WINDOW = 128
NSA_KV_HEADS = 4
NSA_CMP_LEN = 32
NSA_CMP_STRIDE = 16
NSA_SEL_LEN = 64
NSA_N_SEL = 16
NSA_WINDOW = 512
NSA_FORCE_SCORE = 1e4
HEAD_DIM = 64
SEL_CHUNK = 1024
SEL_TAIL = 512
SEL_Q = 512
Q_SCALE = HEAD_DIM ** -0.5

NT_DIMS = (((1,), (1,)), ((), ()))


def _params(*sem):
    return pltpu.CompilerParams(dimension_semantics=sem, vmem_limit_bytes=V7X_VMEM_LIMIT_BYTES)


def _dot(a, b):
    return jnp.dot(a, b, preferred_element_type=F32)


def _dot_nt(a, b):
    return lax.dot_general(a, b, NT_DIMS, preferred_element_type=F32)


def _mod_kernel(c_ref, w_ref, b_ref, o_ref):
    c = c_ref[...]
    s = c * jax.nn.sigmoid(c)
    nb = s.shape[0]
    s8 = jnp.concatenate([s, jnp.zeros((8 - nb, s.shape[1]), F32)], axis=0)
    r = jnp.dot(s8, w_ref[...], preferred_element_type=F32, precision=lax.Precision.HIGHEST)
    o_ref[...] = r[:nb] + b_ref[...]


def adaln_mod(c, mod_w, mod_b):
    B, D = c.shape
    N = mod_w.shape[1]
    tn = N // 4
    out = pl.pallas_call(
        _mod_kernel,
        grid=(N // tn,),
        in_specs=[pl.BlockSpec((B, D), lambda j: (0, 0)),
                  pl.BlockSpec((D, tn), lambda j: (0, j)),
                  pl.BlockSpec((1, tn), lambda j: (0, j))],
        out_specs=pl.BlockSpec((B, tn), lambda j: (0, j)),
        out_shape=jax.ShapeDtypeStruct((B, N), F32),
        compiler_params=_params("arbitrary"),
        name="adaln_mod",
    )(c, mod_w, mod_b.reshape(1, N))
    return out.reshape(B, 6, 1, D)


def _normmod(x, g, scale, shift):
    ms = jnp.mean(x * x, axis=-1, keepdims=True)
    y = x * lax.rsqrt(ms + NORM_EPS) * g
    return y * (1.0 + scale) + shift


def _resident(shape):
    return pl.BlockSpec(shape, lambda *_: (0,) * len(shape), pipeline_mode=pl.Buffered(1))


def _normmod_matmul_kernel(x_ref, g_ref, scale_ref, shift_ref, w_ref, *rest, tn, rot_scales):
    h = _normmod(x_ref[...], g_ref[...], scale_ref[...], shift_ref[...]).astype(BF16)
    o_ref = rest[-1]
    if rot_scales:
        cos, sin = rest[0][...], rest[1][...]
        hw = 2 * cos.shape[1]
    for c in range(w_ref.shape[1] // tn):
        cols = slice(c * tn, (c + 1) * tn)
        acc = _dot(h, w_ref[:, cols])
        if c * tn < len(rot_scales) * (hw if rot_scales else 0):
            parts = []
            for hd in range(tn // hw):
                x1 = acc[:, hd * hw:hd * hw + hw // 2]
                x2 = acc[:, hd * hw + hw // 2:(hd + 1) * hw]
                sc = rot_scales[c * tn // hw + hd]
                parts += [(x1 * cos - x2 * sin) * sc, (x2 * cos + x1 * sin) * sc]
            acc = jnp.concatenate(parts, axis=1)
        o_ref[:, cols] = acc.astype(o_ref.dtype)


def normmod_matmul(x, g, mod, shift_idx, w, T, tm=512, tn=512, rot=None):
    BT, D = x.shape
    N = w.shape[1]
    tm = min(tm, T)
    tn = tn if N % tn == 0 else LANES
    tpb = T // tm
    in_specs = [pl.BlockSpec((tm, D), lambda i: (i, 0)),
                _resident((1, D)),
                pl.BlockSpec((None, None, 1, D), lambda i: (i // tpb, shift_idx + 1, 0, 0)),
                pl.BlockSpec((None, None, 1, D), lambda i: (i // tpb, shift_idx, 0, 0)),
                _resident((D, N))]
    args = [x, g.reshape(1, D), mod, mod, w]
    rot_scales = ()
    if rot is not None:
        cos, sin, rot_scales = rot
        in_specs += [pl.BlockSpec((tm, cos.shape[1]), lambda i: (i % tpb, 0))] * 2
        args += [cos, sin]
    return pl.pallas_call(
        functools.partial(_normmod_matmul_kernel, tn=tn, rot_scales=tuple(rot_scales)),
        grid=(BT // tm,),
        in_specs=in_specs,
        out_specs=pl.BlockSpec((tm, N), lambda i: (i, 0)),
        out_shape=jax.ShapeDtypeStruct((BT, N), BF16),
        compiler_params=_params("parallel"),
        name="normmod_matmul",
    )(*args)


def _ffn_kernel(x_ref, g_ref, scale_ref, shift_ref, gate_ref, wi_ref, wo_ref, *rest, tc):
    o_ref = rest[-1]
    x = x_ref[...]
    if len(rest) > 1:
        *ys, wm_ref, gate1_ref = rest[:-1]
        if len(ys) == 1:
            y = ys[0][...]
        else:
            oc_ref, os_ref, ow_ref, gates_ref, e_ref = ys
            D = oc_ref.shape[1]
            sg = jax.nn.sigmoid(gates_ref[...].astype(F32)).astype(BF16)
            gx = _dot(sg, e_ref[...])
            y = (gx[:, :D] * oc_ref[...].astype(F32) + gx[:, D:2 * D] * os_ref[...].astype(F32)
                 + gx[:, 2 * D:] * ow_ref[...].astype(F32)).astype(BF16)
        x = x + gate1_ref[...] * _dot(y, wm_ref[...])
    h = _normmod(x, g_ref[...], scale_ref[...], shift_ref[...]).astype(BF16)
    F = wo_ref.shape[0]
    acc = None
    for c in range(F // tc):
        a = _dot(h, wi_ref[:, c * tc:(c + 1) * tc])
        b = _dot(h, wi_ref[:, F + c * tc:F + (c + 1) * tc])
        u = (a * jax.nn.sigmoid(a) * b).astype(BF16)
        part = _dot(u, wo_ref[c * tc:(c + 1) * tc, :])
        acc = part if acc is None else acc + part
    o_ref[...] = x + gate_ref[...] * acc


def ffn_block(x, g, mod, w_in, w_out, T, mix=None, tm=512, tc=256):
    BT, D = x.shape
    F = w_out.shape[0]
    tm = min(tm, T)
    tpb = T // tm
    modspec = lambda idx: pl.BlockSpec((None, None, 1, D), lambda i: (i // tpb, idx, 0, 0))
    in_specs = [pl.BlockSpec((tm, D), lambda i: (i, 0)),
                _resident((1, D)),
                modspec(4), modspec(3), modspec(5),
                _resident((D, 2 * F)),
                _resident((F, D))]
    args = [x, g.reshape(1, D), mod, mod, mod, w_in, w_out]
    if mix is not None:
        ys, w_mix = mix
        for y in ys:
            if len(y) == 1:
                in_specs.append(_resident(y[0].shape))
            else:
                in_specs.append(pl.BlockSpec((tm, y[1]), functools.partial(lambda i, j: (i, j), j=y[2])))
            args.append(y[0])
        in_specs += [_resident(w_mix.shape), modspec(2)]
        args += [w_mix, mod]
    return pl.pallas_call(
        functools.partial(_ffn_kernel, tc=tc),
        grid=(BT // tm,),
        in_specs=in_specs,
        out_specs=pl.BlockSpec((tm, D), lambda i: (i, 0)),
        out_shape=jax.ShapeDtypeStruct((BT, D), F32),
        compiler_params=_params("parallel"),
        name="ffn_block",
    )(*args)


def _ret_core_kernel(q_ref, k_ref, v_ref, g_ref, o_ref, state_ref, *, chunks):
    C = RET_CHUNK

    @pl.when(pl.program_id(2) == 0)
    def _():
        state_ref[...] = jnp.zeros_like(state_ref)

    head = pl.program_id(1)
    log_gammas = [float(np.log(1.0 - 2.0 ** (-5.0 - i))) for i in range(RET_HEADS)]
    lg = jnp.float32(log_gammas[-1])
    for i in range(RET_HEADS - 1):
        lg = jnp.where(head == i, log_gammas[i], lg)
    rel = (lax.broadcasted_iota(jnp.int32, (C, C), 0) - lax.broadcasted_iota(jnp.int32, (C, C), 1)).astype(F32)
    intra = jnp.where(rel >= 0, jnp.exp(lg * jnp.maximum(rel, 0.0)), 0.0)
    idx = lax.broadcasted_iota(jnp.int32, (C, 1), 0).astype(F32)
    q_decay = jnp.exp(lg * (idx + 1.0))
    k_decay = jnp.exp(lg * (C - 1.0 - idx))
    state_decay = jnp.exp(lg * float(C))

    for c in range(chunks):
        sl = slice(c * C, (c + 1) * C)
        qb = q_ref[sl, :]
        kb = k_ref[sl, :]
        v = v_ref[sl, :]
        scores = _dot_nt(qb, kb) * intra
        inner = _dot(scores.astype(BF16), v)
        state = state_ref[...]
        cross = _dot(qb, state.astype(BF16)) * q_decay
        kd_t = (kb.astype(F32) * k_decay).T.astype(BF16)
        state_ref[...] = state * state_decay + _dot(kd_t, v)
        out = inner + cross
        ms = jnp.mean(out * out, axis=-1, keepdims=True)
        y = out * lax.rsqrt(ms + NORM_EPS)
        g = g_ref[sl, :]
        o_ref[sl, :] = y.astype(BF16) * (g * jax.nn.sigmoid(g))


def retention_core(proj, B, T, tb=1024):
    H = RET_HEADS
    DK = proj.shape[-1] // (6 * H)
    DV = 2 * DK
    tb = min(tb, T)
    kern = functools.partial(_ret_core_kernel, chunks=tb // RET_CHUNK)
    return pl.pallas_call(
        kern,
        grid=(B, H, T // tb),
        in_specs=[pl.BlockSpec((None, tb, DK), lambda b, h, t: (b, t, h)),
                  pl.BlockSpec((None, tb, DK), lambda b, h, t: (b, t, H + h)),
                  pl.BlockSpec((None, tb, DV), lambda b, h, t: (b, t, H + h)),
                  pl.BlockSpec((None, tb, DV), lambda b, h, t: (b, t, 2 * H + h))],
        out_specs=pl.BlockSpec((None, tb, DV), lambda b, h, t: (b, t, h)),
        out_shape=jax.ShapeDtypeStruct((B, T, H * DV), BF16),
        scratch_shapes=[pltpu.VMEM((DK, DV), F32)],
        compiler_params=_params("parallel", "parallel", "arbitrary"),
        name="retention_core",
    )(proj, proj, proj, proj)


def _head_prep_kernel(x_ref, gains_ref, c_ref, s1_ref, s2_ref, bd_ref, *out_refs, plan):
    c, s1, s2 = c_ref[...], s1_ref[...], s2_ref[...]
    bd = bd_ref[...]
    first_half = lax.broadcasted_iota(jnp.int32, c.shape, 1) < HEAD_DIM
    for src, gain_row, rope, scale, dst in plan:
        x = x_ref[:, src * LANES:(src + 1) * LANES].astype(F32)
        if gain_row is not None:
            ss = _dot((x * x).astype(BF16), bd)
            x = x * lax.rsqrt(ss * (1.0 / HEAD_DIM) + NORM_EPS) * gains_ref[gain_row:gain_row + 1, :]
        if rope:
            x = x * c + pltpu.roll(x, LANES - 8, 1) * s1 + pltpu.roll(x, 8, 1) * s2
        if scale != 1.0:
            x = x * scale
        if dst[0] == "flat":
            out = out_refs[dst[1]]
            out[:, dst[2] * LANES:(dst[2] + 1) * LANES] = x.astype(out.dtype)
        else:
            _, out, head0, ones = dst
            swapped = pltpu.roll(x, HEAD_DIM, 1)
            a = jnp.where(first_half, x, 1.0 if ones else swapped)
            b = jnp.where(first_half, swapped, 1.0 if ones else x)
            out_refs[out][head0] = a.astype(BF16)
            out_refs[out][head0 + 1] = b.astype(BF16)


def head_prep(proj, gains, tables, plan, flat_outs, head_outs, B, T, tm=512):
    BT, N = proj.shape
    tm = min(tm, T)
    tpb = T // tm
    bd = jnp.asarray(np.kron(np.eye(2), np.ones((HEAD_DIM, HEAD_DIM))), BF16)
    gains2 = jnp.concatenate([gains, gains], axis=1).astype(F32)
    tab = pl.BlockSpec((tm, LANES), lambda i: (i % tpb, 0))
    out_specs, out_shape = [], []
    for tiles, dtype in flat_outs:
        out_specs.append(pl.BlockSpec((tm, tiles * LANES), lambda i: (i, 0)))
        out_shape.append(jax.ShapeDtypeStruct((BT, tiles * LANES), dtype))
    for nh in head_outs:
        out_specs.append(pl.BlockSpec((None, nh, tm, LANES), lambda i: (i // tpb, 0, i % tpb, 0)))
        out_shape.append(jax.ShapeDtypeStruct((B, nh, T, LANES), BF16))
    return pl.pallas_call(
        functools.partial(_head_prep_kernel, plan=tuple(plan)),
        grid=(BT // tm,),
        in_specs=[pl.BlockSpec((tm, N), lambda i: (i, 0)),
                  _resident(gains2.shape),
                  tab, tab, tab,
                  _resident((LANES, LANES))],
        out_specs=out_specs,
        out_shape=out_shape,
        compiler_params=_params("parallel"),
        name="head_prep",
    )(proj, gains2, *tables, bd)


def rope_tables(T):
    half = HEAD_DIM // 4 // 2
    inv = ROPE_THETA ** (-jnp.arange(half, dtype=F32) / half)
    lane = np.arange(LANES) % HEAD_DIM
    lo, hi = lane < half, (lane >= half) & (lane < 2 * half)
    inv_lane = jnp.where(lo | hi, inv[lane % half], 0.0)
    ang = jnp.arange(T).astype(F32)[:, None] * inv_lane[None, :]
    sin = jnp.sin(ang)
    return jnp.cos(ang), jnp.where(lo, -sin, 0.0), jnp.where(hi, sin, 0.0)


def retention_tables(T, dk):
    inv = 1.0 / (RET_ROT_BASE ** jnp.linspace(0.0, 1.0, dk // 2, dtype=F32))
    ang = jnp.arange(T).astype(F32)[:, None] * inv[None, :]
    return jnp.cos(ang), jnp.sin(ang)


def _stack_heads(q, extra=None):
    first_half = lax.broadcasted_iota(jnp.int32, (q.shape[0], LANES), 1) < HEAD_DIM
    zero = jnp.zeros((q.shape[0], LANES), q.dtype)
    rows = []
    for p in range(q.shape[1] // LANES):
        qp = q[:, p * LANES:(p + 1) * LANES]
        for part in (jnp.where(first_half, qp, zero), jnp.where(first_half, zero, qp)):
            rows.append(part if extra is None else jnp.concatenate([part, extra], axis=1))
    return jnp.concatenate(rows, axis=0)


def _unstack_heads(o, cq):
    first_half = lax.broadcasted_iota(jnp.int32, (cq, LANES), 1) < HEAD_DIM
    G = o.shape[0] // cq
    pairs = [jnp.where(first_half, o[(2 * p) * cq:(2 * p + 1) * cq], o[(2 * p + 1) * cq:(2 * p + 2) * cq])
             for p in range(G // 2)]
    return jnp.concatenate(pairs, axis=1)


def _unstack_heads_lo(o, cq):
    first_half = lax.broadcasted_iota(jnp.int32, (cq, LANES), 1) < HEAD_DIM
    G = o.shape[0] // cq
    pairs = [jnp.where(first_half, o[(2 * p) * cq:(2 * p + 1) * cq],
                       pltpu.roll(o[(2 * p + 1) * cq:(2 * p + 2) * cq], HEAD_DIM, 1))
             for p in range(G // 2)]
    return jnp.concatenate(pairs, axis=1)


def _banded_kernel(*refs, G, nprev, window, use_sink, qb, unroll):
    if use_sink:
        sink_ref, q_ref, k_ref, v_ref, o_ref = refs
    else:
        q_ref, k_ref, v_ref, o_ref = refs
    Cq = QBLK
    nk = (nprev + 1) * Cq
    hk = pl.program_id(1)

    def block(i, carry):
        n = pl.program_id(2) * qb + i
        r0 = pl.multiple_of(i * Cq, Cq)
        start = pl.multiple_of(jnp.maximum(n - nprev, 0) * Cq, Cq)
        kwin = k_ref[pl.ds(start, nk), :]
        vwin = v_ref[pl.ds(start, nk), :]
        lhs = _stack_heads(q_ref[pl.ds(r0, Cq), :])
        s_all = jnp.concatenate([_dot_nt(lhs[g * Cq:(g + 1) * Cq], kwin) for g in range(G)], axis=0)
        t = n * Cq + lax.broadcasted_iota(jnp.int32, (Cq, nk), 0)
        rel = t - (start + lax.broadcasted_iota(jnp.int32, (Cq, nk), 1))
        valid = jnp.logical_and(rel >= 0, rel < window)
        outs = []
        for g in range(G):
            s = jnp.where(valid, s_all[g * Cq:(g + 1) * Cq], NEG_INF)
            m = jnp.max(s, axis=-1, keepdims=True)
            if use_sink:
                sink = sink_ref[hk * G + g]
                m = jnp.maximum(m, sink)
            p = jnp.exp((s - m).astype(BF16))
            acc = _dot(p, vwin)
            denom = pltpu.roll(acc, HEAD_DIM, 1)
            if use_sink:
                denom = denom + jnp.exp(sink - m)
            else:
                denom = jnp.maximum(denom, SOFTMAX_FLOOR)
            outs.append(acc / denom)
        o_ref[pl.ds(r0, Cq), :] = _unstack_heads_lo(jnp.concatenate(outs, axis=0), Cq).astype(BF16)
        return carry

    lax.fori_loop(0, qb, block, 0, unroll=unroll)


def banded_attention(q, k2, v2, sinks, B, T, *, G, nprev, window, unroll, qb=4):
    Hk = k2.shape[1]
    qb = min(qb, T // QBLK)
    use_sink = sinks is not None
    kern = functools.partial(_banded_kernel, G=G, nprev=nprev, window=window, use_sink=use_sink, qb=qb, unroll=unroll)
    in_specs = [pl.BlockSpec((None, qb * QBLK, G * HEAD_DIM), lambda b, h, n: (b, n, h)),
                pl.BlockSpec((None, None, T, LANES), lambda b, h, n: (b, h, 0, 0)),
                pl.BlockSpec((None, None, T, LANES), lambda b, h, n: (b, h, 0, 0))]
    args = [q, k2, v2]
    if use_sink:
        in_specs = [pl.BlockSpec(memory_space=pltpu.SMEM)] + in_specs
        args = [sinks.astype(F32)] + args
    return pl.pallas_call(
        kern,
        grid=(B, Hk, T // (qb * QBLK)),
        in_specs=in_specs,
        out_specs=pl.BlockSpec((None, qb * QBLK, G * HEAD_DIM), lambda b, h, n: (b, n, h)),
        out_shape=jax.ShapeDtypeStruct((B, T, Hk * G * HEAD_DIM), BF16),
        compiler_params=_params("parallel", "parallel", "arbitrary"),
        name="banded_attention",
    )(*args)


def _nsa_compress_kernel(*refs, norm, npairs):
    x_refs = refs[:npairs]
    pe_ref, w1_ref, wbig_ref, w2_ref, gain_ref, bd_ref, o_ref = refs[npairs:]
    S = NSA_CMP_STRIDE
    ng = x_refs[0].shape[0] // S
    nh = o_ref.shape[0]
    u = [None, None]
    for r in range(S):
        xr = jnp.concatenate([x[pl.ds(r, ng, stride=S), :] for x in x_refs], axis=1).astype(BF16)
        for a in range(2):
            part = _dot(xr, wbig_ref[a, r])
            u[a] = part if u[a] is None else u[a] + part
    pe8 = jnp.broadcast_to(pe_ref[...], (8, pe_ref.shape[1]))
    const = _dot(pe8, w1_ref[...])[:1]
    pre = u[0] + pltpu.roll(u[1], ng - 1, 0) + jnp.concatenate([const] * nh, axis=1)
    hid = (pre * jax.nn.sigmoid(pre)).astype(BF16)
    hw = hid.shape[1] // nh
    for h in range(nh):
        y = _dot(hid[:, h * hw:(h + 1) * hw], w2_ref[...])
        if norm:
            ss = _dot((y * y).astype(BF16), bd_ref[...])
            y = y * lax.rsqrt(ss * (1.0 / HEAD_DIM) + NORM_EPS) * gain_ref[...]
        o_ref[h] = y.astype(BF16)


def nsa_compress(xs, pe, w1, w2, gain, Hk, *, norm):
    B, T, _ = xs[0].shape
    W = len(xs) * LANES
    S = NSA_CMP_STRIDE
    hid = w1.shape[1]
    bd = jnp.asarray(np.kron(np.eye(2), np.ones((HEAD_DIM, HEAD_DIM))), BF16)
    gain2 = jnp.concatenate([gain, gain]).reshape(1, LANES).astype(F32)
    w2d = jnp.concatenate([w2, w2], axis=1).astype(BF16)
    w1b = w1.astype(BF16)
    wbig = jnp.einsum("ardc,kh->arkdhc", w1b.reshape(2, S, HEAD_DIM, hid), jnp.eye(Hk, dtype=BF16))
    wbig = wbig.reshape(2, S, W, Hk * hid)
    return pl.pallas_call(
        functools.partial(_nsa_compress_kernel, norm=norm, npairs=len(xs)),
        grid=(B,),
        in_specs=[pl.BlockSpec((None, T, LANES), lambda b: (b, 0, 0))] * len(xs) + [
                  _resident((1, w1.shape[0])),
                  _resident(w1.shape),
                  _resident(wbig.shape),
                  _resident((hid, LANES)),
                  _resident((1, LANES)),
                  _resident((LANES, LANES))],
        out_specs=pl.BlockSpec((None, Hk, T // S, LANES), lambda b: (b, 0, 0, 0)),
        out_shape=jax.ShapeDtypeStruct((B, Hk, T // S, LANES), BF16),
        compiler_params=_params("parallel"),
        name="nsa_compress",
    )(*xs, pe.reshape(1, w1.shape[0]).astype(BF16), w1b, wbig, w2d, gain2, bd)


def _topk_rows_bias(v, n_sel):
    nslab = v.shape[0] // 8
    sub = lax.broadcasted_iota(jnp.int32, (8, v.shape[1]), 0).astype(F32)
    vals = [v[8 * k:8 * k + 8] for k in range(nslab)]
    ids = [sub + 8.0 * k for k in range(nslab)]
    bias = [jnp.full(sub.shape, MASK_BIAS, F32)] * nslab
    for _ in range(n_sel):
        cand = list(zip(vals, ids))
        while len(cand) > 1:
            nxt = []
            for (va, ia), (vb, ib) in zip(cand[0::2], cand[1::2]):
                take_b = vb > va
                nxt.append((jnp.where(take_b, vb, va), jnp.where(take_b, ib, ia)))
            cand = nxt + ([cand[-1]] if len(cand) % 2 else [])
        bv, bi = cand[0]
        for shift in (4, 2, 1):
            rv, ri = pltpu.roll(bv, shift, 0), pltpu.roll(bi, shift, 0)
            take_r = jnp.logical_or(rv > bv, jnp.logical_and(rv == bv, ri < bi))
            bv, bi = jnp.where(take_r, rv, bv), jnp.where(take_r, ri, bi)
        first = [ids[k] == bi for k in range(nslab)]
        vals = [jnp.where(first[k], REMOVED, vals[k]) for k in range(nslab)]
        bias = [jnp.where(first[k], 0.0, bias[k]) for k in range(nslab)]
    return jnp.concatenate(bias, axis=0)


def _nsa_cmp_kernel(q_ref, kc_ref, vc_ref, ov_ref, oc_ref, mneg_ref, imp_ref, *, n_sel, qb):
    Cq = QBLK
    G = q_ref.shape[1] // HEAD_DIM
    ncp = kc_ref.shape[0]
    step = pl.program_id(2)

    def attend(width):
        for i in range(qb):
            n = step * qb + i
            rows = slice(i * Cq, (i + 1) * Cq)
            s_all = _dot_nt(_stack_heads(q_ref[rows, :]), kc_ref[:width, :])
            t = n * Cq + lax.broadcasted_iota(jnp.int32, (Cq, width), 0)
            cmp_end = lax.broadcasted_iota(jnp.int32, (Cq, width), 1) * NSA_CMP_STRIDE + (NSA_CMP_LEN - 1)
            valid = cmp_end <= t
            outs = []
            psum = None
            for g in range(G):
                s = jnp.where(valid, s_all[g * Cq:(g + 1) * Cq], NEG_INF)
                p = jnp.exp(s - jnp.maximum(jnp.max(s, axis=-1, keepdims=True), 0.1 * NEG_INF))
                p = p / jnp.maximum(jnp.sum(p, axis=-1, keepdims=True), SOFTMAX_FLOOR)
                outs.append(_dot(p.astype(BF16), vc_ref[:width, :]))
                psum = p if psum is None else psum + p
            oc_ref[rows, :] = _unstack_heads(jnp.concatenate(outs, axis=0), Cq).astype(BF16)
            hi = psum.astype(BF16)
            lo = (psum - hi.astype(F32)).astype(BF16)
            ov = ov_ref[:width, :]
            imp_ref[i] = _dot(hi, ov) + _dot(lo, ov)

    n_valid = ((step + 1) * qb * Cq - NSA_CMP_LEN) // NSA_CMP_STRIDE + 1
    groups = (n_valid + LANES - 1) // LANES
    widths = list(range(LANES, ncp, LANES)) + [ncp]
    for gi, width in enumerate(widths):
        last = gi == len(widths) - 1
        pl.when(groups >= gi + 1 if last else groups == gi + 1)(functools.partial(attend, width))

    for i in range(qb):
        n = step * qb + i
        rows = slice(i * Cq, (i + 1) * Cq)
        imp = imp_ref[i].T
        nsb = imp.shape[0]
        j = lax.broadcasted_iota(jnp.int32, (nsb, Cq), 0)
        tt = n * Cq + lax.broadcasted_iota(jnp.int32, (nsb, Cq), 1)
        cur = tt // NSA_SEL_LEN
        forced = jnp.logical_or(j == 0, jnp.logical_or(j == cur, j == cur - 1))
        v = jnp.where(forced, NSA_FORCE_SCORE, imp)
        v = jnp.where(j * NSA_SEL_LEN <= tt, v, NEG_INF)
        mneg_ref[rows, :] = _topk_rows_bias(v, n_sel).T.astype(BF16)


def nsa_cmp_topk(q, kcmp, vcmp, ov, B, T, n_sel, qb=4):
    Hk = kcmp.shape[1]
    GW = q.shape[-1] // Hk
    ncp = kcmp.shape[2]
    cmp_spec = pl.BlockSpec((None, None, ncp, LANES), lambda b, h, n: (b, h, 0, 0))
    tq = qb * QBLK
    return pl.pallas_call(
        functools.partial(_nsa_cmp_kernel, n_sel=n_sel, qb=qb),
        grid=(B, Hk, T // tq),
        in_specs=[pl.BlockSpec((None, tq, GW), lambda b, h, n: (b, n, h)),
                  cmp_spec, cmp_spec,
                  pl.BlockSpec((ncp, LANES), lambda b, h, n: (0, 0))],
        out_specs=[pl.BlockSpec((None, tq, GW), lambda b, h, n: (b, n, h)),
                   pl.BlockSpec((None, None, tq, LANES), lambda b, h, n: (b, h, n, 0))],
        out_shape=[jax.ShapeDtypeStruct((B, T, q.shape[-1]), BF16),
                   jax.ShapeDtypeStruct((B, Hk, T, LANES), BF16)],
        scratch_shapes=[pltpu.VMEM((qb, QBLK, LANES), F32)],
        compiler_params=_params("parallel", "parallel", "arbitrary"),
        name="nsa_cmp_topk",
    )(q, kcmp, vcmp, ov)


def _nsa_sel_kernel(q_ref, mneg_ref, k_ref, e_ref, v_ref, o_ref, lhs_ref, m_ref, acc_ref):
    Cq = q_ref.shape[0]
    HP = k_ref.shape[0]
    GW = q_ref.shape[1] // HP
    n = pl.program_id(2)
    rows = lhs_ref.shape[1]
    for h in range(HP):
        lhs_ref[h] = _stack_heads(q_ref[:, h * GW:(h + 1) * GW], extra=mneg_ref[h])
    m_ref[...] = jnp.full(m_ref.shape, NEG_INF, F32)
    acc_ref[...] = jnp.zeros_like(acc_ref)

    def chunk(off, kc, causal):
        for h in range(HP):
            rhs = jnp.concatenate([k_ref[h, pl.ds(off, kc), :], e_ref[pl.ds(off, kc), :]], axis=1)
            s = _dot_nt(lhs_ref[h], rhs)
            if causal:
                t = n * Cq + (lax.broadcasted_iota(jnp.int32, (rows, kc), 0) & (Cq - 1))
                s = jnp.where(off + lax.broadcasted_iota(jnp.int32, (rows, kc), 1) <= t, s, NEG_INF)
            m_prev = m_ref[h]
            m_new = jnp.maximum(m_prev, jnp.max(s, axis=-1, keepdims=True))
            p = jnp.exp((s - jnp.concatenate([m_new] * (kc // LANES), axis=1)).astype(BF16))
            acc_ref[h] = jnp.exp(m_prev - m_new) * acc_ref[h] + _dot(p, v_ref[h, pl.ds(off, kc), :])
            m_ref[h] = m_new

    per_chunk = SEL_CHUNK // Cq
    nmain = n // per_chunk

    def body(c, carry):
        chunk(pl.multiple_of(c * SEL_CHUNK, SEL_CHUNK), SEL_CHUNK, False)
        return carry

    lax.fori_loop(0, nmain, body, 0)
    tail0 = pl.multiple_of(nmain * SEL_CHUNK, SEL_CHUNK)
    chunk(tail0, SEL_TAIL, True)
    for i in range(1, SEL_CHUNK // SEL_TAIL):
        @pl.when(n - nmain * per_chunk >= i * (SEL_TAIL // Cq))
        def _():
            chunk(tail0 + i * SEL_TAIL, SEL_TAIL, True)

    for h in range(HP):
        acc = acc_ref[h]
        o = acc / jnp.maximum(pltpu.roll(acc, HEAD_DIM, 1), SOFTMAX_FLOOR)
        o_ref[:, h * GW:(h + 1) * GW] = _unstack_heads_lo(o, Cq).astype(BF16)


def nsa_selected(q, mneg, k2, e, v2, B, T, heads_per_step=2):
    Hk = k2.shape[1]
    HP = heads_per_step
    GW = q.shape[-1] // Hk
    G = GW // HEAD_DIM
    kv_spec = pl.BlockSpec((None, HP, T, LANES), lambda b, h, n: (b, h, 0, 0))
    tq = SEL_Q
    return pl.pallas_call(
        _nsa_sel_kernel,
        grid=(B, Hk // HP, T // tq),
        in_specs=[pl.BlockSpec((None, tq, HP * GW), lambda b, h, n: (b, n, h)),
                  pl.BlockSpec((None, HP, tq, LANES), lambda b, h, n: (b, h, n, 0)),
                  kv_spec,
                  _resident((T, LANES)),
                  kv_spec],
        out_specs=pl.BlockSpec((None, tq, HP * GW), lambda b, h, n: (b, n, h)),
        out_shape=jax.ShapeDtypeStruct((B, T, q.shape[-1]), BF16),
        scratch_shapes=[pltpu.VMEM((HP, G * tq, 2 * LANES), BF16),
                        pltpu.VMEM((HP, G * tq, LANES), F32),
                        pltpu.VMEM((HP, G * tq, LANES), F32)],
        compiler_params=_params("parallel", "parallel", "arbitrary"),
        name="nsa_selected",
    )(q, mneg, k2, e, v2)


def retention_mixer(x, mod, norm_g, w_in, w_out, B, T):
    H = RET_HEADS
    D = x.shape[1]
    dk = D // H
    nqk = 2 * H * dk
    w_bf = w_in.astype(BF16)
    order = np.concatenate([np.arange(0, dk, 2), np.arange(1, dk, 2)])
    pick = jnp.asarray(np.arange(dk)[:, None] == order[None, :], BF16)
    w_qk = jnp.einsum("dhk,kj->dhj", w_bf[:, :nqk].reshape(D, 2 * H, dk), pick, preferred_element_type=BF16)
    w_perm = jnp.concatenate([w_qk.reshape(D, nqk), w_bf[:, nqk:]], axis=1)
    cos, sin = retention_tables(T, dk)
    scales = (1.0,) * H + (dk ** -0.5,) * H
    proj = normmod_matmul(x, norm_g, mod, 0, w_perm, T, rot=(cos, sin, scales))
    y = retention_core(proj.reshape(B, T, -1), B, T)
    y = y.reshape(B * T, -1)
    return x, ([(y, y.shape[1], 0)], w_out.astype(BF16))


def swa_mixer(x, mod, norm_g, w_in, q_norm_g, k_norm_g, sinks, w_out, B, T):
    D = x.shape[1]
    Hk = SWA_KV_HEADS
    G = D // HEAD_DIM // Hk
    proj = normmod_matmul(x, norm_g, mod, 0, w_in.astype(BF16), T, tn=256)
    tables = rope_tables(T)
    nq = D // LANES
    nkv = Hk * HEAD_DIM // LANES
    plan = [(j, 0, True, Q_SCALE, ("flat", 0, j)) for j in range(nq)]
    plan += [(nq + j, 1, True, 1.0, ("heads", 1, 2 * j, False)) for j in range(nkv)]
    plan += [(nq + nkv + j, None, False, 1.0, ("heads", 2, 2 * j, True)) for j in range(nkv)]
    q, k2, v2 = head_prep(proj, jnp.stack([q_norm_g, k_norm_g]), tables, plan, [(nq, BF16)], [Hk, Hk], B, T)
    o = banded_attention(q.reshape(B, T, D), k2, v2, sinks, B, T, G=G, nprev=SWA_WINDOW // QBLK, window=SWA_WINDOW,
                         unroll=2)
    return x, ([(o.reshape(B * T, D), D, 0)], w_out.astype(BF16))


def nsa_mixer(x, mod, norm_g, w_in, q_norm_g, k_norm_g, cmp_pe, cmp_w1, cmp_w2, w_out, B, T):
    D = x.shape[1]
    Hk = NSA_KV_HEADS
    H = D // HEAD_DIM
    G = H // Hk
    kvw = Hk * HEAD_DIM
    n_main = D + 6 * kvw
    n_pad = -(-w_in.shape[1] // LANES) * LANES
    w_pad = jnp.pad(w_in, ((0, 0), (0, n_pad - w_in.shape[1]))).astype(BF16)
    proj = normmod_matmul(x, norm_g, mod, 0, w_pad, T, tn=LANES * 3)
    tables = rope_tables(T)
    nq = D // LANES
    nkv = kvw // LANES
    col = lambda i: nq + i * nkv
    plan = [(j, 0, True, Q_SCALE, ("flat", 0, j)) for j in range(nq)]
    for j in range(nkv):
        plan += [(col(0) + j, None, False, 1.0, ("flat", 1 + j, 0)),
                 (col(1) + j, None, False, 1.0, ("flat", 1 + nkv + j, 0)),
                 (col(2) + j, 1, True, 1.0, ("heads", 1 + 2 * nkv, 2 * j, False)),
                 (col(3) + j, None, False, 1.0, ("heads", 2 + 2 * nkv, 2 * j, True)),
                 (col(4) + j, 2, True, 1.0, ("heads", 3 + 2 * nkv, 2 * j, False)),
                 (col(5) + j, None, False, 1.0, ("heads", 4 + 2 * nkv, 2 * j, True))]
    gains = jnp.stack([q_norm_g, k_norm_g[1], k_norm_g[2]])
    outs = head_prep(proj, gains, tables, plan, [(nq, BF16)] + [(1, F32)] * (2 * nkv), [Hk] * 4, B, T)
    q = outs[0].reshape(B, T, D)
    kc = [a.reshape(B, T, LANES) for a in outs[1:1 + nkv]]
    vc = [a.reshape(B, T, LANES) for a in outs[1 + nkv:1 + 2 * nkv]]
    ks2, vs2, kw2, vw2 = outs[1 + 2 * nkv:]

    S = NSA_CMP_STRIDE
    kcmp = nsa_compress(kc, cmp_pe[0], cmp_w1[0], cmp_w2[0], k_norm_g[0], Hk, norm=True)
    vcmp = nsa_compress(vc, cmp_pe[1], cmp_w1[1], cmp_w2[1], k_norm_g[0], Hk, norm=False)

    ncp = T // S
    ns = T // NSA_SEL_LEN
    cs = np.arange(ncp)[:, None] * S
    js = np.arange(LANES)[None, :]
    overlap = ((cs < js * NSA_SEL_LEN + NSA_SEL_LEN) & (cs + NSA_CMP_LEN > js * NSA_SEL_LEN)
               & (js < ns) & (np.arange(ncp)[:, None] < ncp - 1))
    ov = jnp.asarray(overlap, BF16)
    oc, mneg = nsa_cmp_topk(q, kcmp, vcmp, ov, B, T, min(NSA_N_SEL, ns))

    blk_onehot = jnp.asarray(np.arange(T)[:, None] // NSA_SEL_LEN == js, BF16)
    os_ = nsa_selected(q, mneg, ks2, blk_onehot, vs2, B, T)
    ow = banded_attention(q, kw2, vw2, None, B, T, G=G, nprev=NSA_WINDOW // QBLK, window=NSA_WINDOW, unroll=True)

    rows = np.arange(LANES)[:, None]
    cols = np.arange(3 * D)[None, :]
    expand = jnp.asarray(rows == 3 * ((cols % D) // HEAD_DIM) + cols // D, BF16)
    branch = lambda a: (a.reshape(B * T, D), D, 0)
    ys = [branch(oc), branch(os_), branch(ow), (proj, LANES, n_main // LANES), (expand,)]
    return x, (ys, w_out.astype(BF16))


def kernel(x, c, l0_mod_w, l0_mod_b, l0_norm1_g, l0_ret_w_in, l0_ret_w_out, l0_norm2_g, l0_ffn_w_in, l0_ffn_w_out, l1_mod_w, l1_mod_b, l1_norm1_g, l1_swa_w_in, l1_swa_q_norm_g, l1_swa_k_norm_g, l1_swa_sinks, l1_swa_w_out, l1_norm2_g, l1_ffn_w_in, l1_ffn_w_out, l2_mod_w, l2_mod_b, l2_norm1_g, l2_nsa_w_in, l2_nsa_q_norm_g, l2_nsa_k_norm_g, l2_nsa_cmp_pe, l2_nsa_cmp_w1, l2_nsa_cmp_w2, l2_nsa_w_out, l2_norm2_g, l2_ffn_w_in, l2_ffn_w_out, l3_mod_w, l3_mod_b, l3_norm1_g, l3_ret_w_in, l3_ret_w_out, l3_norm2_g, l3_ffn_w_in, l3_ffn_w_out):
    B, T, D = x.shape
    layers = [
        (l0_mod_w, l0_mod_b, l0_norm1_g, retention_mixer, (l0_ret_w_in, l0_ret_w_out), l0_norm2_g, l0_ffn_w_in, l0_ffn_w_out),
        (l1_mod_w, l1_mod_b, l1_norm1_g, swa_mixer,
         (l1_swa_w_in, l1_swa_q_norm_g, l1_swa_k_norm_g, l1_swa_sinks, l1_swa_w_out), l1_norm2_g, l1_ffn_w_in, l1_ffn_w_out),
        (l2_mod_w, l2_mod_b, l2_norm1_g, nsa_mixer,
         (l2_nsa_w_in, l2_nsa_q_norm_g, l2_nsa_k_norm_g, l2_nsa_cmp_pe, l2_nsa_cmp_w1, l2_nsa_cmp_w2, l2_nsa_w_out),
         l2_norm2_g, l2_ffn_w_in, l2_ffn_w_out),
        (l3_mod_w, l3_mod_b, l3_norm1_g, retention_mixer, (l3_ret_w_in, l3_ret_w_out), l3_norm2_g, l3_ffn_w_in, l3_ffn_w_out),
    ]
    h = x.reshape(B * T, D)
    for mod_w, mod_b, norm1_g, mixer, mixer_params, norm2_g, ffn_w_in, ffn_w_out in layers:
        mod = adaln_mod(c, mod_w, mod_b)
        h, mix = mixer(h, mod, norm1_g, *mixer_params, B, T)
        h = ffn_block(h, norm2_g, mod, ffn_w_in.astype(BF16), ffn_w_out.astype(BF16), T, mix=mix)
    return h.reshape(B, T, D)
```
